```python
import math
import jax
import jax.numpy as jnp
from jax import lax
import numpy as np

D_MODEL = 2048
BATCH = 8
SEQ = 4096
DEPTH = 4

MEM_LEN = 256
N_BRANCH = 4
BRANCH_WIDTH = D_MODEL // 2
HEAD_DIM = 128
RG_WIDTH = BRANCH_WIDTH
RG_BLOCK = HEAD_DIM
RG_BLOCKS = RG_WIDTH // RG_BLOCK
RG_CONV = 4
RG_C = 8.0
SB_HEADS = BRANCH_WIDTH // HEAD_DIM
SB_HEAD_DIM = HEAD_DIM
SB_WIDTH = SB_HEADS * SB_HEAD_DIM
MLA_HEADS = BRANCH_WIDTH // HEAD_DIM
MLA_NOPE = 128
MLA_ROPE = 64
MLA_QK = MLA_NOPE + MLA_ROPE
MLA_V = 128
MLA_Q_RANK = 512
MLA_KV_RANK = 256
DN_HEADS = BRANCH_WIDTH // HEAD_DIM
DN_HEAD_DIM = HEAD_DIM
DN_WIDTH = DN_HEADS * DN_HEAD_DIM
DN_CONV = 4
DN_CHUNK = 64
XA_HEADS = 4
XA_HEAD_DIM = 128
XA_WIDTH = XA_HEADS * XA_HEAD_DIM
D_FF = 5632
N_EXPERTS = 8
TOP_K = 2
D_FF_EXPERT = 2816
N_DENSE = (DEPTH + 1) // 2
N_MOE = DEPTH // 2
Q_BLOCK = 128
ROPE_BASE = 10000.0
NORM_EPS = 1e-6
NEG_INF = -1e30
IN_SIZES = (RG_WIDTH, RG_WIDTH, 3 * SB_WIDTH, MLA_Q_RANK, MLA_KV_RANK, MLA_ROPE,
            3 * DN_WIDTH, DN_WIDTH, DN_HEADS, DN_HEADS, N_BRANCH * D_MODEL)
IN_WIDTH = sum(IN_SIZES)

kernel_name = "hybrid_gated_merge_trunk"


def _rms_norm(x, g):
    xf = x.astype(jnp.float32)
    y = xf * lax.rsqrt(jnp.mean(xf * xf, axis=-1, keepdims=True) + NORM_EPS)
    return (y * g.astype(jnp.float32)).astype(x.dtype)


def _l2_norm(x):
    xf = x.astype(jnp.float32)
    return xf * lax.rsqrt(jnp.sum(xf * xf, axis=-1, keepdims=True) + NORM_EPS)


def _split_cols(t, sizes):
    bounds = np.cumsum(sizes)[:-1].tolist()
    return jnp.split(t, bounds, axis=-1)


def _heads(t, n_heads):
    b, s, _ = t.shape
    return t.reshape(b, s, n_heads, -1).transpose(0, 2, 1, 3)


def _merge_heads(t):
    b, h, s, d = t.shape
    return t.transpose(0, 2, 1, 3).reshape(b, s, h * d)


def _causal_dwconv(x, w):
    k = w.shape[0]
    return lax.conv_general_dilated(
        x, w[:, None, :].astype(x.dtype), window_strides=(1,), padding=[(k - 1, 0)],
        dimension_numbers=("NWC", "WIO", "NWC"), feature_group_count=x.shape[-1])


def _rope(x, pos):
    half = x.shape[-1] // 2
    inv_freq = ROPE_BASE ** (-jnp.arange(half, dtype=jnp.float32) / half)
    ang = pos.astype(jnp.float32)[:, None] * inv_freq[None, :]
    cos, sin = jnp.cos(ang), jnp.sin(ang)
    xf = x.astype(jnp.float32)
    x1, x2 = xf[..., :half], xf[..., half:]
    return jnp.concatenate([x1 * cos - x2 * sin, x2 * cos + x1 * sin], axis=-1).astype(x.dtype)


def _sweep_query_blocks(block_fn, q):
    b, h, s, d = q.shape
    nb = s // Q_BLOCK
    qb = jnp.moveaxis(q.reshape(b, h, nb, Q_BLOCK, d), 2, 0)
    starts = jnp.arange(nb, dtype=jnp.int32) * Q_BLOCK
    out = lax.map(lambda args: block_fn(*args), (qb, starts))
    return jnp.moveaxis(out, 0, 2).reshape(b, h, s, out.shape[-1])


def _rg_lru(x, w_a, b_a, w_x, b_x, lam):
    b, s, _ = x.shape
    xf = x.astype(jnp.float32)
    xb = xf.reshape(b, s, RG_BLOCKS, RG_BLOCK)
    r = jax.nn.sigmoid(jnp.einsum("bsgi,gij->bsgj", xb, w_a.astype(jnp.float32)).reshape(b, s, -1)
                       + b_a.astype(jnp.float32))
    i = jax.nn.sigmoid(jnp.einsum("bsgi,gij->bsgj", xb, w_x.astype(jnp.float32)).reshape(b, s, -1)
                       + b_x.astype(jnp.float32))
    log_a = -RG_C * r * jax.nn.softplus(-lam.astype(jnp.float32))
    a = jnp.exp(log_a)
    inp = jnp.sqrt(-jnp.expm1(2.0 * log_a)) * (i * xf)

    def combine(left, right):
        a_l, h_l = left
        a_r, h_r = right
        return a_l * a_r, a_r * h_l + h_r

    _, h = lax.associative_scan(combine, (a, inp), axis=1)
    return h


def _stick_breaking_attention(q, k, v):
    s_len = q.shape[2]
    scale = SB_HEAD_DIM ** -0.5
    kpos = jnp.arange(s_len, dtype=jnp.int32)

    def block(qb, start):
        qpos = start + jnp.arange(Q_BLOCK, dtype=jnp.int32)
        z = jnp.einsum("bhqd,bhkd->bhqk", qb, k, preferred_element_type=jnp.float32) * scale
        visible = kpos[None, :] < qpos[:, None]
        log_keep = jnp.where(visible, jax.nn.log_sigmoid(-z), 0.0)
        log_rest = lax.cumsum(log_keep, axis=3, reverse=True) - log_keep
        w = jnp.where(visible, jnp.exp(jax.nn.log_sigmoid(z) + log_rest), 0.0)
        return jnp.einsum("bhqk,bhkd->bhqd", w.astype(v.dtype), v)

    return _sweep_query_blocks(block, q)


def _latent_attention(q_nope, q_rope, k_nope, k_rope, v):
    s_len = q_nope.shape[2]
    scale = MLA_QK ** -0.5
    kpos = jnp.arange(s_len, dtype=jnp.int32)

    def block(qb, start):
        qpos = start + jnp.arange(Q_BLOCK, dtype=jnp.int32)
        qn, qr = qb[..., :MLA_NOPE], qb[..., MLA_NOPE:]
        sc = (jnp.einsum("bhqd,bhkd->bhqk", qn, k_nope, preferred_element_type=jnp.float32)
              + jnp.einsum("bhqr,bkr->bhqk", qr, k_rope, preferred_element_type=jnp.float32)) * scale
        visible = kpos[None, :] <= qpos[:, None]
        p = jax.nn.softmax(jnp.where(visible, sc, NEG_INF), axis=-1)
        return jnp.einsum("bhqk,bhkd->bhqd", p.astype(v.dtype), v)

    return _sweep_query_blocks(block, jnp.concatenate([q_nope, q_rope], axis=-1))


def _gated_delta_rule(q, k, v, log_decay, beta):
    b, h, s, dk = q.shape
    dv = v.shape[-1]
    n, c = s // DN_CHUNK, DN_CHUNK
    q = q * dk ** -0.5

    def chunks(t):
        return t.reshape(b, h, n, c, *t.shape[3:])

    q, k, v, g, beta = chunks(q), chunks(k), chunks(v), chunks(log_decay), chunks(beta)
    g = jnp.cumsum(g, axis=-1)
    lower = jnp.tril(jnp.ones((c, c), dtype=bool))
    strict = jnp.tril(jnp.ones((c, c), dtype=bool), -1)
    gdiff = g[..., :, None] - g[..., None, :]
    decay_mat = jnp.where(lower, jnp.exp(jnp.where(lower, gdiff, 0.0)), 0.0)
    kb = k * beta[..., None]
    vb = v * beta[..., None]
    a_mat = jnp.where(strict, jnp.einsum("bhnid,bhnjd->bhnij", kb, k) * decay_mat, 0.0)
    t_mat = a_mat + jnp.eye(c, dtype=a_mat.dtype)
    u = lax.linalg.triangular_solve(t_mat, vb, left_side=True, lower=True, unit_diagonal=True)
    w = lax.linalg.triangular_solve(t_mat, kb * jnp.exp(g)[..., None], left_side=True, lower=True,
                                    unit_diagonal=True)
    intra = jnp.einsum("bhnid,bhnjd->bhnij", q, k) * decay_mat
    q_dec = q * jnp.exp(g)[..., None]
    k_dec = k * jnp.exp(g[..., -1:] - g)[..., None]
    chunk_decay = jnp.exp(g[..., -1])

    def step(state, xs):
        u_c, w_c, q_c, k_c, intra_c, dec_c = xs
        v_new = u_c - jnp.einsum("bhcd,bhde->bhce", w_c, state)
        o_c = jnp.einsum("bhcd,bhde->bhce", q_c, state) + jnp.einsum("bhij,bhje->bhie", intra_c, v_new)
        state = state * dec_c[..., None, None] + jnp.einsum("bhcd,bhce->bhde", k_c, v_new)
        return state, o_c

    xs = tuple(jnp.moveaxis(t, 2, 0) for t in (u, w, q_dec, k_dec, intra, chunk_decay))
    state0 = jnp.zeros((b, h, dk, dv), jnp.float32)
    _, o = lax.scan(step, state0, xs)
    return jnp.moveaxis(o, 0, 2).reshape(b, h, s, dv)


def _hybrid_mixer(h, pos, w_in, rg_conv_w, rg_conv_b, rg_w_a, rg_b_a, rg_w_x, rg_b_x, rg_lambda,
                  sb_q_gain, sb_k_gain, mla_cq_gain, mla_w_uq, mla_ckv_gain, mla_w_ukv, mla_q_gain,
                  mla_k_gain, dn_conv_w, dn_a_log, dn_dt_bias, dn_out_gain, w_branch, w_out):
    b, s, _ = h.shape
    f32 = jnp.float32
    (rg_x, rg_gate, sb_qkv, mla_cq, mla_ckv, mla_kr, dn_qkv, dn_z, dn_beta, dn_dt,
     gate_logits) = _split_cols(h @ w_in, IN_SIZES)

    rg_u = _causal_dwconv(rg_x, rg_conv_w) + rg_conv_b
    rg_h = _rg_lru(rg_u, rg_w_a, rg_b_a, rg_w_x, rg_b_x, rg_lambda)
    y_a = (rg_h * jax.nn.gelu(rg_gate.astype(f32))).astype(h.dtype)

    sb_q, sb_k, sb_v = jnp.split(sb_qkv, 3, axis=-1)
    y_b = _merge_heads(_stick_breaking_attention(
        _rms_norm(_heads(sb_q, SB_HEADS), sb_q_gain),
        _rms_norm(_heads(sb_k, SB_HEADS), sb_k_gain),
        _heads(sb_v, SB_HEADS)))

    q = _heads(_rms_norm(mla_cq, mla_cq_gain) @ mla_w_uq, MLA_HEADS)
    q_nope = _rms_norm(q[..., :MLA_NOPE], mla_q_gain[:MLA_NOPE])
    q_rope = _rope(_rms_norm(q[..., MLA_NOPE:], mla_q_gain[MLA_NOPE:]), pos)
    kv = _heads(_rms_norm(mla_ckv, mla_ckv_gain) @ mla_w_ukv, MLA_HEADS)
    k_nope = _rms_norm(kv[..., :MLA_NOPE], mla_k_gain[:MLA_NOPE])
    mla_v = kv[..., MLA_NOPE:]
    k_rope = _rope(_rms_norm(mla_kr, mla_k_gain[MLA_NOPE:]), pos)
    y_c = _merge_heads(_latent_attention(q_nope, q_rope, k_nope, k_rope, mla_v))

    dn = jax.nn.silu(_causal_dwconv(dn_qkv, dn_conv_w).astype(f32))
    dq, dk, dv = jnp.split(dn, 3, axis=-1)
    beta = jax.nn.sigmoid(dn_beta.astype(f32)).transpose(0, 2, 1)
    log_decay = (-jnp.exp(dn_a_log.astype(f32))
                 * jax.nn.softplus(dn_dt.astype(f32) + dn_dt_bias.astype(f32))).transpose(0, 2, 1)
    o = _gated_delta_rule(_l2_norm(_heads(dq, DN_HEADS)), _l2_norm(_heads(dk, DN_HEADS)),
                          _heads(dv, DN_HEADS), log_decay, beta)
    o = _rms_norm(o, dn_out_gain) * jax.nn.silu(_heads(dn_z, DN_HEADS).astype(f32))
    y_d = _merge_heads(o).astype(h.dtype)

    gates = jax.nn.sigmoid(gate_logits.astype(f32)).reshape(b, s, N_BRANCH, -1)
    merged = (gates[:, :, 0] * (y_a @ w_branch[0])
              + gates[:, :, 1] * (y_b @ w_branch[1])
              + gates[:, :, 2] * (y_c @ w_branch[2])
              + gates[:, :, 3] * (y_d @ w_branch[3]))
    return merged.astype(h.dtype) @ w_out


def _memory_attention(hx, mem_n, wq, wkv, q_gain, k_gain, wo):
    b, s, _ = hx.shape
    m = mem_n.shape[1]
    q = _rms_norm((hx @ wq).reshape(b, s, XA_HEADS, XA_HEAD_DIM), q_gain)
    kv = (mem_n @ wkv).reshape(b, m, 2, XA_HEADS, XA_HEAD_DIM)
    k = _rms_norm(kv[:, :, 0], k_gain)
    v = kv[:, :, 1]
    sc = jnp.einsum("bshd,bmhd->bhsm", q, k, preferred_element_type=jnp.float32) * XA_HEAD_DIM ** -0.5
    p = jax.nn.softmax(sc, axis=-1)
    o = jnp.einsum("bhsm,bmhd->bshd", p.astype(v.dtype), v)
    return o.reshape(b, s, XA_WIDTH) @ wo


def _swiglu(h, w_gu, w_down):
    gate, up = jnp.split(h @ w_gu, 2, axis=-1)
    return (jax.nn.silu(gate) * up) @ w_down


def _moe(h, router, w_gu, w_down):
    logits = jnp.einsum("bsd,de->bse", h, router, preferred_element_type=jnp.float32)
    top_vals, top_idx = lax.top_k(logits, TOP_K)
    top_w = jax.nn.softmax(top_vals, axis=-1)
    combine = jnp.sum(jax.nn.one_hot(top_idx, N_EXPERTS, dtype=jnp.float32) * top_w[..., None], axis=-2)
    y = jnp.zeros(h.shape, jnp.float32)
    for e in range(N_EXPERTS):
        y = y + combine[..., e:e + 1] * _swiglu(h, w_gu[e], w_down[e]).astype(jnp.float32)
    return y.astype(h.dtype)


def setup_inputs(seed: int = 0) -> dict:
    keys = jax.random.split(jax.random.key(seed), 64)
    counter = [0]
    f32 = jnp.float32

    def next_key():
        k = keys[counter[0]]
        counter[0] += 1
        return k

    def normal(shape, scale):
        return scale * jax.random.normal(next_key(), shape, f32)

    def gain(shape):
        return 1.0 + normal(shape, 0.02)

    def uniform(shape, lo, hi):
        return jax.random.uniform(next_key(), shape, f32, lo, hi)

    out_scale = (2 * DEPTH) ** -0.5
    lru_a = uniform((DEPTH, RG_WIDTH), 0.9, 0.999) ** (1.0 / RG_C)
    dt = jnp.exp(uniform((DEPTH, DN_HEADS), math.log(1e-3), math.log(1e-1)))
    return {
        "x": normal((BATCH, SEQ, D_MODEL), 1.0),
        "mem": normal((BATCH, MEM_LEN, D_MODEL), 1.0),
        "mix_norm": gain((DEPTH, D_MODEL)),
        "w_in": normal((DEPTH, D_MODEL, IN_WIDTH), D_MODEL ** -0.5),
        "rg_conv_w": normal((DEPTH, RG_CONV, RG_WIDTH), RG_CONV ** -0.5),
        "rg_conv_b": normal((DEPTH, RG_WIDTH), 0.02),
        "rg_w_a": normal((DEPTH, RG_BLOCKS, RG_BLOCK, RG_BLOCK), RG_BLOCK ** -0.5),
        "rg_b_a": normal((DEPTH, RG_WIDTH), 0.02),
        "rg_w_x": normal((DEPTH, RG_BLOCKS, RG_BLOCK, RG_BLOCK), RG_BLOCK ** -0.5),
        "rg_b_x": normal((DEPTH, RG_WIDTH), 0.02),
        "rg_lambda": jnp.log(lru_a) - jnp.log1p(-lru_a),
        "sb_q_gain": gain((DEPTH, SB_HEAD_DIM)),
        "sb_k_gain": gain((DEPTH, SB_HEAD_DIM)),
        "mla_cq_gain": gain((DEPTH, MLA_Q_RANK)),
        "mla_w_uq": normal((DEPTH, MLA_Q_RANK, MLA_HEADS * MLA_QK), MLA_Q_RANK ** -0.5),
        "mla_ckv_gain": gain((DEPTH, MLA_KV_RANK)),
        "mla_w_ukv": normal((DEPTH, MLA_KV_RANK, MLA_HEADS * (MLA_NOPE + MLA_V)), MLA_KV_RANK ** -0.5),
        "mla_q_gain": gain((DEPTH, MLA_QK)),
        "mla_k_gain": gain((DEPTH, MLA_QK)),
        "dn_conv_w": normal((DEPTH, DN_CONV, 3 * DN_WIDTH), DN_CONV ** -0.5),
        "dn_a_log": jnp.log(uniform((DEPTH, DN_HEADS), 1.0, 16.0)),
        "dn_dt_bias": dt + jnp.log(-jnp.expm1(-dt)),
        "dn_out_gain": gain((DEPTH, DN_HEAD_DIM)),
        "w_branch": normal((DEPTH, N_BRANCH, BRANCH_WIDTH, D_MODEL), BRANCH_WIDTH ** -0.5),
        "w_out": normal((DEPTH, D_MODEL, D_MODEL), out_scale * D_MODEL ** -0.5),
        "xa_norm": gain((DEPTH, D_MODEL)),
        "mem_norm": gain((DEPTH, D_MODEL)),
        "xa_wq": normal((DEPTH, D_MODEL, XA_WIDTH), D_MODEL ** -0.5),
        "xa_wkv": normal((DEPTH, D_MODEL, 2 * XA_WIDTH), D_MODEL ** -0.5),
        "xa_q_gain": gain((DEPTH, XA_HEAD_DIM)),
        "xa_k_gain": gain((DEPTH, XA_HEAD_DIM)),
        "xa_wo": normal((DEPTH, XA_WIDTH, D_MODEL), out_scale * XA_WIDTH ** -0.5),
        "ffn_norm": gain((DEPTH, D_MODEL)),
        "ffn_w_gu": normal((N_DENSE, D_MODEL, 2 * D_FF), D_MODEL ** -0.5),
        "ffn_w_down": normal((N_DENSE, D_FF, D_MODEL), out_scale * D_FF ** -0.5),
        "moe_router": normal((N_MOE, D_MODEL, N_EXPERTS), D_MODEL ** -0.5),
        "moe_w_gu": normal((N_MOE, N_EXPERTS, D_MODEL, 2 * D_FF_EXPERT), D_MODEL ** -0.5),
        "moe_w_down": normal((N_MOE, N_EXPERTS, D_FF_EXPERT, D_MODEL), out_scale * D_FF_EXPERT ** -0.5),
    }


def reference(x, mem, mix_norm, w_in, rg_conv_w, rg_conv_b, rg_w_a, rg_b_a, rg_w_x, rg_b_x, rg_lambda,
              sb_q_gain, sb_k_gain, mla_cq_gain, mla_w_uq, mla_ckv_gain, mla_w_ukv, mla_q_gain, mla_k_gain,
              dn_conv_w, dn_a_log, dn_dt_bias, dn_out_gain, w_branch, w_out,
              xa_norm, mem_norm, xa_wq, xa_wkv, xa_q_gain, xa_k_gain, xa_wo,
              ffn_norm, ffn_w_gu, ffn_w_down, moe_router, moe_w_gu, moe_w_down):
    pos = jnp.arange(x.shape[1], dtype=jnp.int32)
    for layer in range(DEPTH):
        h = _rms_norm(x, mix_norm[layer])
        x = x + _hybrid_mixer(
            h, pos, w_in[layer], rg_conv_w[layer], rg_conv_b[layer], rg_w_a[layer], rg_b_a[layer],
            rg_w_x[layer], rg_b_x[layer], rg_lambda[layer], sb_q_gain[layer], sb_k_gain[layer],
            mla_cq_gain[layer], mla_w_uq[layer], mla_ckv_gain[layer], mla_w_ukv[layer],
            mla_q_gain[layer], mla_k_gain[layer], dn_conv_w[layer], dn_a_log[layer],
            dn_dt_bias[layer], dn_out_gain[layer], w_branch[layer], w_out[layer])
        x = x + _memory_attention(
            _rms_norm(x, xa_norm[layer]), _rms_norm(mem, mem_norm[layer]), xa_wq[layer],
            xa_wkv[layer], xa_q_gain[layer], xa_k_gain[layer], xa_wo[layer])
        h = _rms_norm(x, ffn_norm[layer])
        if layer % 2 == 0:
            x = x + _swiglu(h, ffn_w_gu[layer // 2], ffn_w_down[layer // 2])
        else:
            x = x + _moe(h, moe_router[layer // 2], moe_w_gu[layer // 2], moe_w_down[layer // 2])
    return x
```

```python
import functools
import math

import jax
import jax.numpy as jnp
import numpy as np
from jax import lax
from jax.experimental import pallas as pl
from jax.experimental.pallas import tpu as pltpu

F32 = jnp.float32
BF16 = jnp.bfloat16
HIGHEST = lax.Precision.HIGHEST

HEAD_DIM = 128
LANES = 128
MLA_ROPE = 64
RG_C = 8.0
ROPE_BASE = 10000.0
NORM_EPS = 1e-6
NEG_INF = -1e30
DN_CHUNK = 128
TOP_K = 2
VMEM_LIMIT = 56 * 1024 * 1024


def _cparams(*sem):
    return pltpu.CompilerParams(dimension_semantics=sem, vmem_limit_bytes=VMEM_LIMIT)


def _pcall(body, **kwargs):
    fn = getattr(body, "func", body)
    return pl.pallas_call(body, name=fn.__name__.strip("_").replace("_body", ""), **kwargs)


def _pick(n, prefs):
    for p in prefs:
        if p <= n and n % p == 0:
            return p
    return n


def _rms(x, g):
    return x * lax.rsqrt(jnp.mean(x * x, axis=-1, keepdims=True) + NORM_EPS) * g


def _sigmoid(x):
    return 1.0 / (1.0 + jnp.exp(-x))


def _softplus(x):
    return jnp.maximum(x, 0.0) + jnp.log(1.0 + jnp.exp(-jnp.abs(x)))


def _dot(a, b, precision=None):
    return jnp.dot(a, b, preferred_element_type=F32, precision=precision)


def _dot_nt(a, b, precision=None):
    return lax.dot_general(a, b, (((1,), (1,)), ((), ())), preferred_element_type=F32,
                           precision=precision)


def _norm_mm_body(x_ref, g_ref, w_ref, o_ref, xn_ref, *, precision):
    @pl.when(pl.program_id(1) == 0)
    def _():
        xn_ref[...] = _rms(x_ref[...].astype(F32), g_ref[...]).astype(xn_ref.dtype)

    o_ref[...] = _dot(xn_ref[...], w_ref[...], precision).astype(o_ref.dtype)


def _norm_matmul(x, g, w, *, out_dtype, tm, tn, precision=None):
    m, k = x.shape
    n = w.shape[1]
    return _pcall(
        functools.partial(_norm_mm_body, precision=precision),
        grid=(m // tm, n // tn),
        in_specs=[pl.BlockSpec((tm, k), lambda i, j: (i, 0)),
                  pl.BlockSpec((1, k), lambda i, j: (0, 0)),
                  pl.BlockSpec((k, tn), lambda i, j: (0, j))],
        out_specs=pl.BlockSpec((tm, tn), lambda i, j: (i, j)),
        out_shape=jax.ShapeDtypeStruct((m, n), out_dtype),
        scratch_shapes=[pltpu.VMEM((tm, k), w.dtype)],
        compiler_params=_cparams("parallel", "arbitrary"),
    )(x, g.reshape(1, k), w)


def _swiglu_body(x_ref, g_ref, wg_ref, wu_ref, o_ref, xn_ref):
    @pl.when(pl.program_id(1) == 0)
    def _():
        xn_ref[...] = _rms(x_ref[...], g_ref[...]).astype(BF16)

    xn = xn_ref[...]
    a = _dot(xn, wg_ref[...])
    b = _dot(xn, wu_ref[...])
    o_ref[...] = (a * _sigmoid(a) * b).astype(o_ref.dtype)


def _norm_swiglu(x, g, w_gu, *, tm, tn):
    m, k = x.shape
    f = w_gu.shape[1] // 2
    nj = f // tn
    return _pcall(
        _swiglu_body,
        grid=(m // tm, nj),
        in_specs=[pl.BlockSpec((tm, k), lambda i, j: (i, 0)),
                  pl.BlockSpec((1, k), lambda i, j: (0, 0)),
                  pl.BlockSpec((k, tn), lambda i, j: (0, j)),
                  pl.BlockSpec((k, tn), lambda i, j: (0, j + nj))],
        out_specs=pl.BlockSpec((tm, tn), lambda i, j: (i, j)),
        out_shape=jax.ShapeDtypeStruct((m, f), BF16),
        scratch_shapes=[pltpu.VMEM((tm, k), BF16)],
        compiler_params=_cparams("parallel", "arbitrary"),
    )(x, g.reshape(1, k), w_gu, w_gu)


def _moe_swiglu_body(x_ref, g_ref, c_ref, wg_ref, wu_ref, o_ref, xn_ref):
    e = pl.program_id(1)

    @pl.when((e == 0) & (pl.program_id(2) == 0))
    def _():
        xn_ref[...] = _rms(x_ref[...], g_ref[...]).astype(BF16)

    xn = xn_ref[...]
    a = _dot(xn, wg_ref[...])
    b = _dot(xn, wu_ref[...])
    c = c_ref[...]
    lane = lax.broadcasted_iota(jnp.int32, c.shape, 1)
    ce = jnp.sum(jnp.where(lane == e, c, 0.0), axis=-1, keepdims=True)
    o_ref[...] = (a * _sigmoid(a) * b * ce).astype(o_ref.dtype)


def _moe_swiglu(x, g, combine, w_gu, *, tm, tn):
    m, k = x.shape
    ne = w_gu.shape[0]
    f = w_gu.shape[2] // 2
    nj = f // tn
    return _pcall(
        _moe_swiglu_body,
        grid=(m // tm, ne, nj),
        in_specs=[pl.BlockSpec((tm, k), lambda i, e, j: (i, 0)),
                  pl.BlockSpec((1, k), lambda i, e, j: (0, 0)),
                  pl.BlockSpec((tm, LANES), lambda i, e, j: (i, 0)),
                  pl.BlockSpec((None, k, tn), lambda i, e, j: (e, 0, j)),
                  pl.BlockSpec((None, k, tn), lambda i, e, j: (e, 0, j + nj))],
        out_specs=pl.BlockSpec((tm, tn), lambda i, e, j: (i, e * nj + j)),
        out_shape=jax.ShapeDtypeStruct((m, ne * f), BF16),
        scratch_shapes=[pltpu.VMEM((tm, k), BF16)],
        compiler_params=_cparams("parallel", "arbitrary", "arbitrary"),
    )(x, g.reshape(1, k), combine, w_gu, w_gu)


def _mm_res_body(a_ref, w_ref, r_ref, o_ref, acc_ref, *, nk):
    kk = pl.program_id(2)

    @pl.when(kk == 0)
    def _():
        acc_ref[...] = jnp.zeros_like(acc_ref)

    acc_ref[...] += _dot(a_ref[...], w_ref[...])

    @pl.when(kk == nk - 1)
    def _():
        o_ref[...] = r_ref[...] + acc_ref[...]


def _matmul_residual(a, w, r, *, tm, tn, tk):
    m, k = a.shape
    n = w.shape[1]
    nk = k // tk
    return _pcall(
        functools.partial(_mm_res_body, nk=nk),
        grid=(m // tm, n // tn, nk),
        in_specs=[pl.BlockSpec((tm, tk), lambda i, j, kk: (i, kk)),
                  pl.BlockSpec((tk, tn), lambda i, j, kk: (kk, j)),
                  pl.BlockSpec((tm, tn), lambda i, j, kk: (i, j))],
        out_specs=pl.BlockSpec((tm, tn), lambda i, j, kk: (i, j)),
        out_shape=jax.ShapeDtypeStruct((m, n), F32),
        scratch_shapes=[pltpu.VMEM((tm, tn), F32)],
        compiler_params=_cparams("parallel", "parallel", "arbitrary"),
    )(a, w, r)


def _causal_conv(xbuf, x, cw, ts):
    kk = cw.shape[0]
    xbuf[8:8 + ts, :] = x
    y = cw[kk - 1:kk, :] * x
    for d in range(1, kk):
        y = y + cw[kk - 1 - d:kk - d, :] * xbuf[8 - d:8 - d + ts, :]
    xbuf[0:8, :] = xbuf[ts:ts + 8, :]
    return y


def _rglru_body(x_ref, gate_ref, cw_ref, cb_ref, wa_ref, ba_ref, wx_ref, bx_ref, lam_ref, o_ref,
                xbuf, hcar, *, ts, width):
    @pl.when(pl.program_id(1) == 0)
    def _():
        xbuf[0:8, :] = jnp.zeros((8, width), F32)
        hcar[...] = jnp.zeros_like(hcar)

    u = _causal_conv(xbuf, x_ref[...].astype(F32), cw_ref[...], ts) + cb_ref[...]
    ub = u.astype(BF16)
    rs, is_ = [], []
    for g in range(width // HEAD_DIM):
        blk = ub[:, g * HEAD_DIM:(g + 1) * HEAD_DIM]
        rs.append(_dot(blk, wa_ref[g]))
        is_.append(_dot(blk, wx_ref[g]))
    r = _sigmoid(jnp.concatenate(rs, axis=-1) + ba_ref[...])
    ig = _sigmoid(jnp.concatenate(is_, axis=-1) + bx_ref[...])
    log_a = (-RG_C) * r * _softplus(-lam_ref[...])
    a = jnp.exp(log_a)
    b = jnp.sqrt(1.0 - jnp.exp(2.0 * log_a)) * (ig * u)
    rows = lax.broadcasted_iota(jnp.int32, (ts, 1), 0)
    d = 1
    while d < ts:
        keep = rows >= d
        b = jnp.where(keep, a * pltpu.roll(b, d, 0) + b, b)
        a = jnp.where(keep, a * pltpu.roll(a, d, 0), a)
        d *= 2
    h = a * hcar[0:1, :] + b
    hcar[0:1, :] = h[ts - 1:ts, :]
    gt = gate_ref[...].astype(F32)
    gelu = 0.5 * gt * (1.0 + jnp.tanh(math.sqrt(2.0 / math.pi) * (gt + 0.044715 * (gt * gt * gt))))
    o_ref[...] = (h * gelu).astype(o_ref.dtype)


def _rglru(proj, nb, seq, conv_w, conv_b, w_a, b_a, w_x, b_x, lam, *, ts):
    width = conv_w.shape[1]
    nt = seq // ts
    row = lambda v: v.reshape(1, width)
    full2 = lambda shp: pl.BlockSpec(shp, lambda b, t: (0,) * len(shp))
    return _pcall(
        functools.partial(_rglru_body, ts=ts, width=width),
        grid=(nb, nt),
        in_specs=[pl.BlockSpec((ts, width), lambda b, t: (b * nt + t, 0)),
                  pl.BlockSpec((ts, width), lambda b, t: (b * nt + t, 1)),
                  full2(conv_w.shape), full2((1, width)), full2(w_a.shape), full2((1, width)),
                  full2(w_x.shape), full2((1, width)), full2((1, width))],
        out_specs=pl.BlockSpec((ts, width), lambda b, t: (b * nt + t, 0)),
        out_shape=jax.ShapeDtypeStruct((nb * seq, width), BF16),
        scratch_shapes=[pltpu.VMEM((ts + 8, width), F32), pltpu.VMEM((8, width), F32)],
        compiler_params=_cparams("parallel", "arbitrary"),
    )(proj, proj, conv_w, row(conv_b), w_a.astype(BF16), row(b_a), w_x.astype(BF16), row(b_x), row(lam))


def _sb_body(q_ref, k_ref, v_ref, qg_ref, kg_ref, o_ref, kn_ref, *, tq, scale):
    qi = pl.program_id(2)

    @pl.when(qi == 0)
    def _():
        kn_ref[...] = _rms(k_ref[...].astype(F32), kg_ref[...]).astype(BF16)

    qn = (_rms(q_ref[...].astype(F32), qg_ref[...]) * scale).astype(BF16)
    rr = lax.broadcasted_iota(jnp.int32, (tq, tq), 0)
    cc = lax.broadcasted_iota(jnp.int32, (tq, tq), 1)
    upper = (rr > cc).astype(BF16)

    def block(j, carry, diag):
        acc, run = carry
        ks = pl.multiple_of(j * tq, tq)
        z = _dot_nt(qn, kn_ref[pl.ds(ks, tq), :])
        sp = _softplus(z)
        lk = -sp
        if diag:
            vis = cc < rr
            lk = jnp.where(vis, lk, 0.0)
        hi = lk.astype(BF16)
        lo = (lk - hi.astype(F32)).astype(BF16)
        cs = _dot(hi, upper) + _dot(lo, upper)
        logw = z - sp + cs + run
        w = jnp.exp(logw)
        if diag:
            w = jnp.where(vis, w, 0.0)
        acc = acc + _dot(w.astype(BF16), v_ref[pl.ds(ks, tq), :])
        run = run + jnp.sum(lk, axis=-1, keepdims=True)
        return acc, run

    carry = block(qi, (jnp.zeros((tq, HEAD_DIM), F32), jnp.zeros((tq, 1), F32)), True)
    acc, _ = lax.fori_loop(0, qi, lambda s, c: block(qi - 1 - s, c, False), carry)
    o_ref[...] = acc.astype(o_ref.dtype)


def _sb_attention(proj3, q_gain, k_gain, *, heads, qblk, kblk, vblk, tq):
    nb, seq, _ = proj3.shape
    return _pcall(
        functools.partial(_sb_body, tq=tq, scale=HEAD_DIM ** -0.5),
        grid=(nb, heads, seq // tq),
        in_specs=[pl.BlockSpec((None, tq, HEAD_DIM), lambda b, h, i: (b, i, qblk + h)),
                  pl.BlockSpec((None, seq, HEAD_DIM), lambda b, h, i: (b, 0, kblk + h)),
                  pl.BlockSpec((None, seq, HEAD_DIM), lambda b, h, i: (b, 0, vblk + h)),
                  pl.BlockSpec((1, HEAD_DIM), lambda b, h, i: (0, 0)),
                  pl.BlockSpec((1, HEAD_DIM), lambda b, h, i: (0, 0))],
        out_specs=pl.BlockSpec((None, tq, HEAD_DIM), lambda b, h, i: (b, i, h)),
        out_shape=jax.ShapeDtypeStruct((nb, seq, heads * HEAD_DIM), BF16),
        scratch_shapes=[pltpu.VMEM((seq, HEAD_DIM), BF16)],
        compiler_params=_cparams("parallel", "parallel", "arbitrary"),
    )(proj3, proj3, proj3, q_gain.reshape(1, HEAD_DIM), k_gain.reshape(1, HEAD_DIM))


def _rope_pad(r, gain, cos_t, sin_a, sin_b):
    ms = jnp.sum(r * r, axis=-1, keepdims=True) * (1.0 / MLA_ROPE)
    rn = r * lax.rsqrt(ms + NORM_EPS) * gain
    return rn * cos_t + pltpu.roll(rn, LANES - MLA_ROPE // 2, 1) * sin_a + pltpu.roll(rn, MLA_ROPE // 2, 1) * sin_b


def _mla_q_body(cq_ref, g_ref, w_ref, qg_ref, cos_ref, sa_ref, sb_ref, on_ref, or_ref, xn_ref):
    @pl.when(pl.program_id(1) == 0)
    def _():
        xn_ref[...] = _rms(cq_ref[...].astype(F32), g_ref[...]).astype(BF16)

    q = _dot(xn_ref[...], w_ref[...])
    qg = qg_ref[...]
    on_ref[...] = _rms(q[:, :HEAD_DIM], qg[:, :HEAD_DIM]).astype(on_ref.dtype)
    or_ref[...] = _rope_pad(q[:, HEAD_DIM:], qg[:, HEAD_DIM:], cos_ref[...], sa_ref[...],
                            sb_ref[...]).astype(or_ref.dtype)


def _mla_q(proj, cq_gain, w_q, q_gain2, tabs, *, heads, cq_blk, nb, seq, tm):
    rank = w_q.shape[0]
    nt = seq // tm
    tab = pl.BlockSpec((tm, LANES), lambda i, j: (i % nt, 0))
    out = pl.BlockSpec((tm, HEAD_DIM), lambda i, j: (i, j))
    shp = jax.ShapeDtypeStruct((nb * seq, heads * HEAD_DIM), BF16)
    return _pcall(
        _mla_q_body,
        grid=(nb * nt, heads),
        in_specs=[pl.BlockSpec((tm, rank), lambda i, j: (i, cq_blk)),
                  pl.BlockSpec((1, rank), lambda i, j: (0, 0)),
                  pl.BlockSpec((rank, 2 * HEAD_DIM), lambda i, j: (0, j)),
                  pl.BlockSpec((1, 2 * HEAD_DIM), lambda i, j: (0, 0)),
                  tab, tab, tab],
        out_specs=[out, out],
        out_shape=[shp, shp],
        scratch_shapes=[pltpu.VMEM((tm, rank), BF16)],
        compiler_params=_cparams("parallel", "arbitrary"),
    )(proj, cq_gain.reshape(1, rank), w_q, q_gain2, *tabs)


def _mla_kv_body(ckv_ref, g_ref, w_ref, kg_ref, kr_ref, krg_ref, cos_ref, sa_ref, sb_ref,
                 kn_ref, v_ref, kro_ref, xn_ref):
    @pl.when(pl.program_id(1) == 0)
    def _():
        xn_ref[...] = _rms(ckv_ref[...].astype(F32), g_ref[...]).astype(BF16)
        kro_ref[...] = _rope_pad(kr_ref[...].astype(F32), krg_ref[...], cos_ref[...], sa_ref[...],
                                 sb_ref[...]).astype(kro_ref.dtype)

    kv = _dot(xn_ref[...], w_ref[...])
    kn_ref[...] = _rms(kv[:, :HEAD_DIM], kg_ref[...]).astype(kn_ref.dtype)
    v_ref[...] = kv[:, HEAD_DIM:].astype(v_ref.dtype)


def _mla_kv(proj, ckv_gain, w_kv, k_gain_n, k_gain_r, tabs, *, heads, ckv_blk, kr_blk, nb, seq, tm):
    rank = w_kv.shape[0]
    nt = seq // tm
    tab = pl.BlockSpec((tm, LANES), lambda i, j: (i % nt, 0))
    out = pl.BlockSpec((tm, HEAD_DIM), lambda i, j: (i, j))
    shp = jax.ShapeDtypeStruct((nb * seq, heads * HEAD_DIM), BF16)
    return _pcall(
        _mla_kv_body,
        grid=(nb * nt, heads),
        in_specs=[pl.BlockSpec((tm, rank), lambda i, j: (i, ckv_blk)),
                  pl.BlockSpec((1, rank), lambda i, j: (0, 0)),
                  pl.BlockSpec((rank, 2 * HEAD_DIM), lambda i, j: (0, j)),
                  pl.BlockSpec((1, HEAD_DIM), lambda i, j: (0, 0)),
                  pl.BlockSpec((tm, LANES), lambda i, j: (i, kr_blk)),
                  pl.BlockSpec((1, LANES), lambda i, j: (0, 0)),
                  tab, tab, tab],
        out_specs=[out, out, pl.BlockSpec((tm, LANES), lambda i, j: (i, 0))],
        out_shape=[shp, shp, jax.ShapeDtypeStruct((nb * seq, LANES), BF16)],
        scratch_shapes=[pltpu.VMEM((tm, rank), BF16)],
        compiler_params=_cparams("parallel", "arbitrary"),
    )(proj, ckv_gain.reshape(1, rank), w_kv, k_gain_n, proj, k_gain_r, *tabs)


def _mla_attn_body(qn_ref, qr_ref, kn_ref, kr_ref, v_ref, o_ref, *, tq):
    qi = pl.program_id(2)
    qn = qn_ref[...]
    qr = qr_ref[...]
    rr = lax.broadcasted_iota(jnp.int32, (tq, tq), 0)
    cc = lax.broadcasted_iota(jnp.int32, (tq, tq), 1)

    def block(j, carry, diag):
        m, l, acc = carry
        ks = pl.multiple_of(j * tq, tq)
        s = _dot_nt(qn, kn_ref[pl.ds(ks, tq), :]) + _dot_nt(qr, kr_ref[pl.ds(ks, tq), :])
        if diag:
            s = jnp.where(cc <= rr, s, NEG_INF)
        m_new = jnp.maximum(m, jnp.max(s, axis=-1, keepdims=True))
        alpha = jnp.exp(m - m_new)
        p = jnp.exp(s - m_new)
        l = alpha * l + jnp.sum(p, axis=-1, keepdims=True)
        acc = alpha * acc + _dot(p.astype(BF16), v_ref[pl.ds(ks, tq), :])
        return m_new, l, acc

    init = (jnp.full((tq, 1), NEG_INF, F32), jnp.zeros((tq, 1), F32), jnp.zeros((tq, HEAD_DIM), F32))
    carry = lax.fori_loop(0, qi, lambda j, c: block(j, c, False), init)
    _, l, acc = block(qi, carry, True)
    o_ref[...] = (acc / l).astype(o_ref.dtype)


def _mla_attention(qn, qr, kn, kr, v, *, heads, tq):
    nb, seq, _ = qn.shape
    qspec = pl.BlockSpec((None, tq, HEAD_DIM), lambda b, h, i: (b, i, h))
    kspec = pl.BlockSpec((None, seq, HEAD_DIM), lambda b, h, i: (b, 0, h))
    return _pcall(
        functools.partial(_mla_attn_body, tq=tq),
        grid=(nb, heads, seq // tq),
        in_specs=[qspec, qspec, kspec,
                  pl.BlockSpec((None, seq, LANES), lambda b, h, i: (b, 0, 0)), kspec],
        out_specs=qspec,
        out_shape=jax.ShapeDtypeStruct((nb, seq, heads * HEAD_DIM), BF16),
        compiler_params=_cparams("parallel", "parallel", "arbitrary"),
    )(qn, qr, kn, kr, v)


def _dn_prep_body(x_ref, bd_ref, cw_ref, alog_ref, dtb_ref, o_ref, gb_ref, xbuf, *, ts, heads):
    width = 3 * heads * HEAD_DIM

    @pl.when(pl.program_id(1) == 0)
    def _():
        xbuf[0:8, :] = jnp.zeros((8, width), F32)

    y = _causal_conv(xbuf, x_ref[...].astype(F32), cw_ref[...], ts)
    y = y * _sigmoid(y)
    qscale = HEAD_DIM ** -0.5
    for hh in range(3 * heads):
        blk = y[:, hh * HEAD_DIM:(hh + 1) * HEAD_DIM]
        if hh < 2 * heads:
            blk = blk * lax.rsqrt(jnp.sum(blk * blk, axis=-1, keepdims=True) + NORM_EPS)
            if hh < heads:
                blk = blk * qscale
        o_ref[:, hh * HEAD_DIM:(hh + 1) * HEAD_DIM] = blk
    bd = bd_ref[...]
    beta = _sigmoid(bd[:, :LANES])
    g = -jnp.exp(alog_ref[...]) * _softplus(bd[:, LANES:] + dtb_ref[...])
    rows = lax.broadcasted_iota(jnp.int32, (ts, 1), 0) % DN_CHUNK
    d = 1
    while d < DN_CHUNK:
        g = g + jnp.where(rows >= d, pltpu.roll(g, d, 0), 0.0)
        d *= 2
    gb_ref[:, :LANES] = g
    gb_ref[:, LANES:] = beta


def _dn_prep(proj, bd, conv_w, a_log, dt_bias, *, heads, qkv_blk, nb, seq, ts):
    width = 3 * heads * HEAD_DIM
    nt = seq // ts
    pad = lambda v: jnp.zeros((1, LANES), F32).at[0, :heads].set(v)
    return _pcall(
        functools.partial(_dn_prep_body, ts=ts, heads=heads),
        grid=(nb, nt),
        in_specs=[pl.BlockSpec((ts, width), lambda b, t: (b * nt + t, qkv_blk)),
                  pl.BlockSpec((ts, 2 * LANES), lambda b, t: (b * nt + t, 0)),
                  pl.BlockSpec(conv_w.shape, lambda b, t: (0, 0)),
                  pl.BlockSpec((1, LANES), lambda b, t: (0, 0)),
                  pl.BlockSpec((1, LANES), lambda b, t: (0, 0))],
        out_specs=[pl.BlockSpec((ts, width), lambda b, t: (b * nt + t, 0)),
                   pl.BlockSpec((ts, 2 * LANES), lambda b, t: (b * nt + t, 0))],
        out_shape=[jax.ShapeDtypeStruct((nb * seq, width), F32),
                   jax.ShapeDtypeStruct((nb * seq, 2 * LANES), F32)],
        scratch_shapes=[pltpu.VMEM((ts + 8, width), F32)],
        compiler_params=_cparams("parallel", "arbitrary"),
    )(proj, bd, conv_w, pad(a_log), pad(dt_bias))


def _dn_body(qkv_ref, gb_ref, z_ref, og_ref, o_ref, state, *, heads, solve_precision):
    c = DN_CHUNK
    hd = HEAD_DIM

    @pl.when(pl.program_id(1) == 0)
    def _():
        state[...] = jnp.zeros_like(state)

    rr = lax.broadcasted_iota(jnp.int32, (c, c), 0)
    cc = lax.broadcasted_iota(jnp.int32, (c, c), 1)
    lower = rr >= cc
    strict = rr > cc
    eye = (rr == cc).astype(F32)
    ones = jnp.ones((c, c), F32)
    gb = gb_ref[...]
    for h in range(heads):
        q = qkv_ref[:, h * hd:(h + 1) * hd]
        k = qkv_ref[:, (heads + h) * hd:(heads + h + 1) * hd]
        v = qkv_ref[:, (2 * heads + h) * hd:(2 * heads + h + 1) * hd]
        gcol = gb[:, h:h + 1]
        bcol = gb[:, LANES + h:LANES + h + 1]
        gcb = jnp.broadcast_to(gcol, (c, c))
        grow = _dot(ones, gcb * eye, HIGHEST)
        dec = jnp.where(lower, jnp.exp(jnp.where(lower, gcb - grow, 0.0)), 0.0)
        kb = k * bcol
        vb = v * bcol
        n_mat = -jnp.where(strict, _dot_nt(kb, k) * dec, 0.0)
        p_mat = eye + n_mat
        pw = n_mat
        for _ in range(int(math.log2(c)) - 1):
            pw = _dot(pw, pw, solve_precision)
            p_mat = p_mat + _dot(p_mat, pw, solve_precision)
        eg = jnp.exp(gcol)
        u = _dot(p_mat, vb, solve_precision)
        w = _dot(p_mat, kb * eg, solve_precision)
        intra = _dot_nt(q, k) * dec
        glast = gcol[c - 1:c, :]
        s_h = state[h]
        v_new = u - _dot(w, s_h)
        o = _dot(q * eg, s_h) + _dot(intra, v_new)
        k_dec = k * jnp.exp(glast - gcol)
        state[h] = s_h * jnp.exp(glast) + _dot(k_dec.T, v_new)
        zz = z_ref[:, h * hd:(h + 1) * hd].astype(F32)
        o_ref[:, h * hd:(h + 1) * hd] = (_rms(o, og_ref[...]) * (zz * _sigmoid(zz))).astype(o_ref.dtype)


def _delta_rule(qkv, gb, proj, out_gain, *, heads, z_blk, nb, seq, solve_precision):
    c = DN_CHUNK
    nt = seq // c
    hw = heads * HEAD_DIM
    return _pcall(
        functools.partial(_dn_body, heads=heads, solve_precision=solve_precision),
        grid=(nb, nt),
        in_specs=[pl.BlockSpec((c, 3 * hw), lambda b, t: (b * nt + t, 0)),
                  pl.BlockSpec((c, 2 * LANES), lambda b, t: (b * nt + t, 0)),
                  pl.BlockSpec((c, hw), lambda b, t: (b * nt + t, z_blk)),
                  pl.BlockSpec((1, HEAD_DIM), lambda b, t: (0, 0))],
        out_specs=pl.BlockSpec((c, hw), lambda b, t: (b * nt + t, 0)),
        out_shape=jax.ShapeDtypeStruct((nb * seq, hw), BF16),
        scratch_shapes=[pltpu.VMEM((heads, HEAD_DIM, HEAD_DIM), F32)],
        compiler_params=_cparams("parallel", "arbitrary"),
    )(qkv, gb, proj, out_gain.reshape(1, HEAD_DIM))


def _merge_body(ya_ref, yb_ref, yc_ref, yd_ref, g0_ref, g1_ref, g2_ref, g3_ref, wb_ref, o_ref):
    acc = None
    for i, (y_ref, gl_ref) in enumerate(((ya_ref, g0_ref), (yb_ref, g1_ref), (yc_ref, g2_ref),
                                         (yd_ref, g3_ref))):
        t = _sigmoid(gl_ref[...].astype(F32)) * _dot(y_ref[...], wb_ref[i])
        acc = t if acc is None else acc + t
    o_ref[...] = acc.astype(o_ref.dtype)


def _gated_merge(ys, proj, w_branch, *, gate_off, tm, tn):
    m, bw = ys[0].shape
    nbr, _, n = w_branch.shape
    yspec = pl.BlockSpec((tm, bw), lambda i, j: (i, 0))
    gspecs = [pl.BlockSpec((tm, tn), functools.partial(lambda i, j, o: (i, o + j), o=(gate_off + r * n) // tn))
              for r in range(nbr)]
    return _pcall(
        _merge_body,
        grid=(m // tm, n // tn),
        in_specs=[yspec] * nbr + gspecs + [pl.BlockSpec((nbr, bw, tn), lambda i, j: (0, 0, j))],
        out_specs=pl.BlockSpec((tm, tn), lambda i, j: (i, j)),
        out_shape=jax.ShapeDtypeStruct((m, n), BF16),
        compiler_params=_cparams("parallel", "arbitrary"),
    )(*ys, *([proj] * nbr), w_branch)


def _mem_kv_body(m_ref, g_ref, w_ref, kg_ref, k_ref, v_ref, *, heads):
    xn = _rms(m_ref[...], g_ref[...]).astype(BF16)
    kv = _dot(xn, w_ref[...])
    hw = heads * HEAD_DIM
    for h in range(heads):
        k_ref[:, h * HEAD_DIM:(h + 1) * HEAD_DIM] = _rms(kv[:, h * HEAD_DIM:(h + 1) * HEAD_DIM],
                                                         kg_ref[...]).astype(k_ref.dtype)
    v_ref[...] = kv[:, hw:].astype(v_ref.dtype)


def _mem_kv(mem2, g, wkv, k_gain, *, nb, mlen, heads):
    d = mem2.shape[1]
    hw = heads * HEAD_DIM
    out = pl.BlockSpec((mlen, hw), lambda b: (b, 0))
    shp = jax.ShapeDtypeStruct((nb * mlen, hw), BF16)
    return _pcall(
        functools.partial(_mem_kv_body, heads=heads),
        grid=(nb,),
        in_specs=[pl.BlockSpec((mlen, d), lambda b: (b, 0)),
                  pl.BlockSpec((1, d), lambda b: (0, 0)),
                  pl.BlockSpec((d, 2 * hw), lambda b: (0, 0)),
                  pl.BlockSpec((1, HEAD_DIM), lambda b: (0, 0))],
        out_specs=[out, out],
        out_shape=[shp, shp],
        compiler_params=_cparams("parallel"),
    )(mem2, g.reshape(1, d), wkv, k_gain.reshape(1, HEAD_DIM))


def _xattn_body(x_ref, g_ref, wq_ref, k_ref, v_ref, qg_ref, wo_ref, o_ref, *, heads):
    x = x_ref[...]
    q = _dot(_rms(x, g_ref[...]).astype(BF16), wq_ref[...])
    scale = HEAD_DIM ** -0.5
    outs = []
    for h in range(heads):
        sl = slice(h * HEAD_DIM, (h + 1) * HEAD_DIM)
        qh = (_rms(q[:, sl], qg_ref[...]) * scale).astype(BF16)
        s = _dot_nt(qh, k_ref[:, sl])
        p = jnp.exp(s - jnp.max(s, axis=-1, keepdims=True))
        l = jnp.sum(p, axis=-1, keepdims=True)
        outs.append(_dot(p.astype(BF16), v_ref[:, sl]) / l)
    o = jnp.concatenate(outs, axis=-1).astype(BF16)
    o_ref[...] = x + _dot(o, wo_ref[...])


def _cross_attention(x2, g, wq, k, v, q_gain, wo, *, nb, seq, mlen, heads, tm):
    d = x2.shape[1]
    hw = heads * HEAD_DIM
    nt = seq // tm
    xspec = pl.BlockSpec((tm, d), lambda b, t: (b * nt + t, 0))
    return _pcall(
        functools.partial(_xattn_body, heads=heads),
        grid=(nb, nt),
        in_specs=[xspec,
                  pl.BlockSpec((1, d), lambda b, t: (0, 0)),
                  pl.BlockSpec((d, hw), lambda b, t: (0, 0)),
                  pl.BlockSpec((mlen, hw), lambda b, t: (b, 0)),
                  pl.BlockSpec((mlen, hw), lambda b, t: (b, 0)),
                  pl.BlockSpec((1, HEAD_DIM), lambda b, t: (0, 0)),
                  pl.BlockSpec((hw, d), lambda b, t: (0, 0))],
        out_specs=xspec,
        out_shape=jax.ShapeDtypeStruct(x2.shape, F32),
        compiler_params=_cparams("parallel", "arbitrary"),
    )(x2, g.reshape(1, d), wq, k, v, q_gain.reshape(1, HEAD_DIM), wo)


def _router_body(x_ref, g_ref, w_ref, o_ref, *, n_experts):
    logits = _dot(_rms(x_ref[...], g_ref[...]), w_ref[...], HIGHEST)
    lane = lax.broadcasted_iota(jnp.int32, logits.shape, 1)
    lg = jnp.where(lane < n_experts, logits, -jnp.inf)
    m1 = jnp.max(lg, axis=-1, keepdims=True)
    i1 = jnp.min(jnp.where(lg == m1, lane, LANES), axis=-1, keepdims=True)
    lg2 = jnp.where(lane == i1, -jnp.inf, lg)
    m2 = jnp.max(lg2, axis=-1, keepdims=True)
    i2 = jnp.min(jnp.where(lg2 == m2, lane, LANES), axis=-1, keepdims=True)
    e2 = jnp.exp(m2 - m1)
    den = 1.0 + e2
    o_ref[...] = jnp.where(lane == i1, 1.0 / den, 0.0) + jnp.where(lane == i2, e2 / den, 0.0)


def _router(x2, g, w_router, *, tm):
    m, d = x2.shape
    ne = w_router.shape[1]
    wpad = jnp.zeros((d, LANES), F32).at[:, :ne].set(w_router)
    return _pcall(
        functools.partial(_router_body, n_experts=ne),
        grid=(m // tm,),
        in_specs=[pl.BlockSpec((tm, d), lambda i: (i, 0)),
                  pl.BlockSpec((1, d), lambda i: (0, 0)),
                  pl.BlockSpec((d, LANES), lambda i: (0, 0))],
        out_specs=pl.BlockSpec((tm, LANES), lambda i: (i, 0)),
        out_shape=jax.ShapeDtypeStruct((m, LANES), F32),
        compiler_params=_cparams("parallel"),
    )(x2, g.reshape(1, d), wpad)


def _rope_tables(seq):
    half = MLA_ROPE // 2
    inv_freq = ROPE_BASE ** (-jnp.arange(half, dtype=F32) / half)
    ang = jnp.arange(seq, dtype=jnp.int32).astype(F32)[:, None] * inv_freq[None, :]
    cos, sin = jnp.cos(ang), jnp.sin(ang)
    z = jnp.zeros((seq, half), F32)
    cos_t = jnp.concatenate([cos, cos, z, z], axis=-1)
    sin_a = jnp.concatenate([-sin, z, z, z], axis=-1)
    sin_b = jnp.concatenate([z, sin, z, z], axis=-1)
    return cos_t, sin_a, sin_b


def kernel(x, mem, mix_norm, w_in, rg_conv_w, rg_conv_b, rg_w_a, rg_b_a, rg_w_x, rg_b_x, rg_lambda, sb_q_gain, sb_k_gain, mla_cq_gain, mla_w_uq, mla_ckv_gain, mla_w_ukv, mla_q_gain, mla_k_gain, dn_conv_w, dn_a_log, dn_dt_bias, dn_out_gain, w_branch, w_out, xa_norm, mem_norm, xa_wq, xa_wkv, xa_q_gain, xa_k_gain, xa_wo, ffn_norm, ffn_w_gu, ffn_w_down, moe_router, moe_w_gu, moe_w_down):
    nb, seq, d = x.shape
    depth = w_in.shape[0]
    mlen = mem.shape[1]
    bw = rg_conv_w.shape[2]
    heads = bw // HEAD_DIM
    q_rank = mla_cq_gain.shape[1]
    kv_rank = mla_ckv_gain.shape[1]
    xa_heads = xa_wq.shape[2] // HEAD_DIM
    t_tok = nb * seq
    assert bw % HEAD_DIM == 0 and seq % 512 == 0 and d % 512 == 0
    assert mla_q_gain.shape[1] == HEAD_DIM + MLA_ROPE

    groups = (("rg", 2 * bw, bw), ("sb", 3 * bw, HEAD_DIM), ("cq", q_rank, q_rank),
              ("ckv", kv_rank, kv_rank), ("kr", MLA_ROPE, LANES), ("dnqkv", 3 * bw, 3 * bw),
              ("dnz", bw, bw), ("beta", heads, None), ("dt", heads, None), ("gate", 4 * d, d))
    src, off, pieces = {}, {}, []
    s_pos = p_pos = 0
    for name, width, align in groups:
        src[name] = (s_pos, s_pos + width)
        s_pos += width
        if align is None:
            continue
        start = -(-p_pos // align) * align
        pieces.append((start - p_pos, src[name]))
        off[name] = start
        p_pos = start + width
        if name == "kr":
            pieces.append((LANES - width, None))
            p_pos += LANES - width
    n_proj = -(-p_pos // 512) * 512
    pieces.append((n_proj - p_pos, None))
    assert s_pos == w_in.shape[2]

    x2 = x.reshape(t_tok, d)
    mem2 = mem.reshape(nb * mlen, d)
    tabs = _rope_tables(seq)
    mla_scale = (HEAD_DIM + MLA_ROPE) ** -0.5

    tm_big = _pick(t_tok, (1024, 512, 256, 128))
    tm_mid = _pick(t_tok, (512, 256, 128))
    ts_seq = _pick(seq, (512, 256, 128))

    for layer in range(depth):
        wi = w_in[layer]
        zc = lambda n: jnp.zeros((d, n), wi.dtype)
        cols = []
        for gap, rng in pieces:
            if gap:
                cols.append(zc(gap))
            if rng is not None:
                cols.append(wi[:, rng[0]:rng[1]])
        w_proj = jnp.concatenate(cols, axis=1).astype(BF16)
        w_bd = jnp.concatenate([wi[:, src["beta"][0]:src["beta"][1]], zc(LANES - heads),
                                wi[:, src["dt"][0]:src["dt"][1]], zc(LANES - heads)], axis=1)
        wq3 = mla_w_uq[layer].reshape(q_rank, heads, HEAD_DIM + MLA_ROPE)
        w_q = jnp.concatenate([wq3, jnp.zeros((q_rank, heads, HEAD_DIM - MLA_ROPE), F32)],
                              axis=-1).reshape(q_rank, heads * 2 * HEAD_DIM).astype(BF16)
        qg = mla_q_gain[layer] * mla_scale
        q_gain2 = jnp.concatenate([qg, jnp.zeros((HEAD_DIM - MLA_ROPE,), F32)]).reshape(1, 2 * HEAD_DIM)
        kgl = mla_k_gain[layer]
        k_gain_n = kgl[:HEAD_DIM].reshape(1, HEAD_DIM)
        k_gain_r = jnp.concatenate([kgl[HEAD_DIM:], jnp.zeros((LANES - MLA_ROPE,), F32)]).reshape(1, LANES)

        proj = _norm_matmul(x2, mix_norm[layer], w_proj, out_dtype=BF16, tm=tm_big,
                            tn=_pick(n_proj, (512, 256, 128)))
        bd = _norm_matmul(x2, mix_norm[layer], w_bd, out_dtype=F32, tm=tm_big, tn=2 * LANES,
                          precision=HIGHEST)
        proj3 = proj.reshape(nb, seq, n_proj)

        y_a = _rglru(proj, nb, seq, rg_conv_w[layer], rg_conv_b[layer], rg_w_a[layer], rg_b_a[layer],
                     rg_w_x[layer], rg_b_x[layer], rg_lambda[layer], ts=_pick(seq, (256, 128)))

        y_b = _sb_attention(proj3, sb_q_gain[layer], sb_k_gain[layer], heads=heads,
                            qblk=off["sb"] // HEAD_DIM, kblk=(off["sb"] + bw) // HEAD_DIM,
                            vblk=(off["sb"] + 2 * bw) // HEAD_DIM, tq=_pick(seq, (256, 128)))

        qn, qr = _mla_q(proj, mla_cq_gain[layer], w_q, q_gain2, tabs, heads=heads,
                        cq_blk=off["cq"] // q_rank, nb=nb, seq=seq, tm=ts_seq)
        kn, vv, kr = _mla_kv(proj, mla_ckv_gain[layer], mla_w_ukv[layer].astype(BF16), k_gain_n,
                             k_gain_r, tabs, heads=heads, ckv_blk=off["ckv"] // kv_rank,
                             kr_blk=off["kr"] // LANES, nb=nb, seq=seq, tm=ts_seq)
        r3 = lambda a: a.reshape(nb, seq, a.shape[-1])
        y_c = _mla_attention(r3(qn), r3(qr), r3(kn), r3(kr), r3(vv), heads=heads, tq=ts_seq)

        dn_qkv, dn_gb = _dn_prep(proj, bd, dn_conv_w[layer], dn_a_log[layer], dn_dt_bias[layer],
                                 heads=heads, qkv_blk=off["dnqkv"] // (3 * bw), nb=nb, seq=seq, ts=ts_seq)
        y_d = _delta_rule(dn_qkv, dn_gb, proj, dn_out_gain[layer], heads=heads, z_blk=off["dnz"] // bw,
                          nb=nb, seq=seq, solve_precision=HIGHEST)

        merged = _gated_merge([y_a, y_b.reshape(t_tok, bw), y_c.reshape(t_tok, bw), y_d], proj,
                              w_branch[layer].astype(BF16), gate_off=off["gate"], tm=tm_mid,
                              tn=_pick(d, (512, 256, 128)))
        x2 = _matmul_residual(merged, w_out[layer].astype(BF16), x2, tm=tm_mid,
                              tn=_pick(d, (1024, 512, 256, 128)), tk=d)

        mk, mv = _mem_kv(mem2, mem_norm[layer], xa_wkv[layer].astype(BF16), xa_k_gain[layer],
                         nb=nb, mlen=mlen, heads=xa_heads)
        x2 = _cross_attention(x2, xa_norm[layer], xa_wq[layer].astype(BF16), mk, mv, xa_q_gain[layer],
                              xa_wo[layer].astype(BF16), nb=nb, seq=seq, mlen=mlen, heads=xa_heads,
                              tm=ts_seq)

        if layer % 2 == 0:
            wgu = ffn_w_gu[layer // 2].astype(BF16)
            f = wgu.shape[1] // 2
            hmid = _norm_swiglu(x2, ffn_norm[layer], wgu, tm=tm_mid, tn=_pick(f, (512, 256, 128)))
            x2 = _matmul_residual(hmid, ffn_w_down[layer // 2].astype(BF16), x2, tm=tm_mid,
                                  tn=_pick(d, (1024, 512, 256, 128)), tk=_pick(f, (1408, 512, 256, 128)))
        else:
            wgu = moe_w_gu[layer // 2].astype(BF16)
            ne, _, f2 = wgu.shape
            f = f2 // 2
            combine = _router(x2, ffn_norm[layer], moe_router[layer // 2], tm=tm_mid)
            hmid = _moe_swiglu(x2, ffn_norm[layer], combine, wgu, tm=tm_mid, tn=_pick(f, (1408, 512, 256, 128)))
            wdn = moe_w_down[layer // 2].astype(BF16).reshape(ne * f, d)
            x2 = _matmul_residual(hmid, wdn, x2, tm=tm_mid, tn=_pick(d, (1024, 512, 256, 128)),
                                  tk=_pick(f, (1408, 512, 256, 128)))
    return x2.reshape(nb, seq, d)
```

```python
import functools
import math

import jax
import jax.numpy as jnp
import numpy as np
from jax import lax
from jax.experimental import pallas as pl
from jax.experimental.pallas import tpu as pltpu

F32 = jnp.float32
BF16 = jnp.bfloat16
HIGHEST = lax.Precision.HIGHEST

HEAD_DIM = 128
LANES = 128
MLA_ROPE = 64
RG_C = 8.0
ROPE_BASE = 10000.0
NORM_EPS = 1e-6
NEG_INF = -1e30
EXP_UNDERFLOW = -104.0
DN_CHUNK = 128
TOP_K = 2
VMEM_LIMIT = 56 * 1024 * 1024


def _cparams(*sem):
    return pltpu.CompilerParams(dimension_semantics=sem, vmem_limit_bytes=VMEM_LIMIT)


def _pcall(body, **kwargs):
    fn = getattr(body, "func", body)
    return pl.pallas_call(body, name=fn.__name__.strip("_").replace("_body", ""), **kwargs)


def _pick(n, prefs):
    for p in prefs:
        if p <= n and n % p == 0:
            return p
    return n


def _rms(x, g):
    return x * lax.rsqrt(jnp.mean(x * x, axis=-1, keepdims=True) + NORM_EPS) * g


def _sigmoid(x):
    return 1.0 / (1.0 + jnp.exp(-x))


def _softplus(x):
    return jnp.maximum(x, 0.0) + jnp.log(1.0 + jnp.exp(-jnp.abs(x)))


def _dot(a, b, precision=None):
    return jnp.dot(a, b, preferred_element_type=F32, precision=precision)


def _dot_nt(a, b, precision=None):
    return lax.dot_general(a, b, (((1,), (1,)), ((), ())), preferred_element_type=F32,
                           precision=precision)


def _norm_mm_body(x_ref, g_ref, w_ref, o_ref, xn_ref, *, precision):
    @pl.when(pl.program_id(1) == 0)
    def _():
        xn_ref[...] = _rms(x_ref[...].astype(F32), g_ref[...]).astype(xn_ref.dtype)

    o_ref[...] = _dot(xn_ref[...], w_ref[...], precision).astype(o_ref.dtype)


def _norm_matmul(x, g, w, *, out_dtype, tm, tn, precision=None):
    m, k = x.shape
    n = w.shape[1]
    return _pcall(
        functools.partial(_norm_mm_body, precision=precision),
        grid=(m // tm, n // tn),
        in_specs=[pl.BlockSpec((tm, k), lambda i, j: (i, 0)),
                  pl.BlockSpec((1, k), lambda i, j: (0, 0)),
                  pl.BlockSpec((k, tn), lambda i, j: (0, j))],
        out_specs=pl.BlockSpec((tm, tn), lambda i, j: (i, j)),
        out_shape=jax.ShapeDtypeStruct((m, n), out_dtype),
        scratch_shapes=[pltpu.VMEM((tm, k), w.dtype)],
        compiler_params=_cparams("parallel", "arbitrary"),
    )(x, g.reshape(1, k), w)


def _swiglu_body(x_ref, g_ref, wg_ref, wu_ref, o_ref, xn_ref):
    @pl.when(pl.program_id(1) == 0)
    def _():
        xn_ref[...] = _rms(x_ref[...], g_ref[...]).astype(BF16)

    xn = xn_ref[...]
    a = _dot(xn, wg_ref[...])
    b = _dot(xn, wu_ref[...])
    o_ref[...] = (a * _sigmoid(a) * b).astype(o_ref.dtype)


def _norm_swiglu(x, g, w_gu, *, tm, tn):
    m, k = x.shape
    f = w_gu.shape[1] // 2
    nj = f // tn
    return _pcall(
        _swiglu_body,
        grid=(m // tm, nj),
        in_specs=[pl.BlockSpec((tm, k), lambda i, j: (i, 0)),
                  pl.BlockSpec((1, k), lambda i, j: (0, 0)),
                  pl.BlockSpec((k, tn), lambda i, j: (0, j)),
                  pl.BlockSpec((k, tn), lambda i, j: (0, j + nj))],
        out_specs=pl.BlockSpec((tm, tn), lambda i, j: (i, j)),
        out_shape=jax.ShapeDtypeStruct((m, f), BF16),
        scratch_shapes=[pltpu.VMEM((tm, k), BF16)],
        compiler_params=_cparams("parallel", "arbitrary"),
    )(x, g.reshape(1, k), w_gu, w_gu)


def _moe_swiglu_body(x_ref, g_ref, c_ref, wg_ref, wu_ref, o_ref, xn_ref):
    e = pl.program_id(1)

    @pl.when((e == 0) & (pl.program_id(2) == 0))
    def _():
        xn_ref[...] = _rms(x_ref[...], g_ref[...]).astype(BF16)

    xn = xn_ref[...]
    a = _dot(xn, wg_ref[...])
    b = _dot(xn, wu_ref[...])
    c = c_ref[...]
    lane = lax.broadcasted_iota(jnp.int32, c.shape, 1)
    ce = jnp.sum(jnp.where(lane == e, c, 0.0), axis=-1, keepdims=True)
    o_ref[...] = (a * _sigmoid(a) * b * ce).astype(o_ref.dtype)


def _moe_swiglu(x, g, combine, w_gu, *, tm, tn):
    m, k = x.shape
    ne = w_gu.shape[0]
    f = w_gu.shape[2] // 2
    nj = f // tn
    return _pcall(
        _moe_swiglu_body,
        grid=(m // tm, ne, nj),
        in_specs=[pl.BlockSpec((tm, k), lambda i, e, j: (i, 0)),
                  pl.BlockSpec((1, k), lambda i, e, j: (0, 0)),
                  pl.BlockSpec((tm, LANES), lambda i, e, j: (i, 0)),
                  pl.BlockSpec((None, k, tn), lambda i, e, j: (e, 0, j)),
                  pl.BlockSpec((None, k, tn), lambda i, e, j: (e, 0, j + nj))],
        out_specs=pl.BlockSpec((tm, tn), lambda i, e, j: (i, e * nj + j)),
        out_shape=jax.ShapeDtypeStruct((m, ne * f), BF16),
        scratch_shapes=[pltpu.VMEM((tm, k), BF16)],
        compiler_params=_cparams("parallel", "arbitrary", "arbitrary"),
    )(x, g.reshape(1, k), combine, w_gu, w_gu)


def _mm_res_body(a_ref, w_ref, r_ref, o_ref, acc_ref, *, nk):
    kk = pl.program_id(2)

    @pl.when(kk == 0)
    def _():
        acc_ref[...] = jnp.zeros_like(acc_ref)

    acc_ref[...] += _dot(a_ref[...], w_ref[...])

    @pl.when(kk == nk - 1)
    def _():
        o_ref[...] = r_ref[...] + acc_ref[...]


def _matmul_residual(a, w, r, *, tm, tn, tk):
    m, k = a.shape
    n = w.shape[1]
    nk = k // tk
    return _pcall(
        functools.partial(_mm_res_body, nk=nk),
        grid=(m // tm, n // tn, nk),
        in_specs=[pl.BlockSpec((tm, tk), lambda i, j, kk: (i, kk)),
                  pl.BlockSpec((tk, tn), lambda i, j, kk: (kk, j)),
                  pl.BlockSpec((tm, tn), lambda i, j, kk: (i, j))],
        out_specs=pl.BlockSpec((tm, tn), lambda i, j, kk: (i, j)),
        out_shape=jax.ShapeDtypeStruct((m, n), F32),
        scratch_shapes=[pltpu.VMEM((tm, tn), F32)],
        compiler_params=_cparams("parallel", "parallel", "arbitrary"),
    )(a, w, r)


def _causal_conv(xbuf, x, cw, ts):
    kk = cw.shape[0]
    xbuf[8:8 + ts, :] = x
    y = cw[kk - 1:kk, :] * x
    for d in range(1, kk):
        y = y + cw[kk - 1 - d:kk - d, :] * xbuf[8 - d:8 - d + ts, :]
    xbuf[0:8, :] = xbuf[ts:ts + 8, :]
    return y


def _rglru_body(x_ref, gate_ref, cw_ref, cb_ref, wa_ref, ba_ref, wx_ref, bx_ref, lam_ref, o_ref,
                xbuf, hcar, *, ts, width):
    @pl.when(pl.program_id(1) == 0)
    def _():
        xbuf[0:8, :] = jnp.zeros((8, width), F32)
        hcar[...] = jnp.zeros_like(hcar)

    u = _causal_conv(xbuf, x_ref[...].astype(F32), cw_ref[...], ts) + cb_ref[...]
    ub = u.astype(BF16)
    rs, is_ = [], []
    for g in range(width // HEAD_DIM):
        blk = ub[:, g * HEAD_DIM:(g + 1) * HEAD_DIM]
        rs.append(_dot(blk, wa_ref[g]))
        is_.append(_dot(blk, wx_ref[g]))
    r = _sigmoid(jnp.concatenate(rs, axis=-1) + ba_ref[...])
    ig = _sigmoid(jnp.concatenate(is_, axis=-1) + bx_ref[...])
    log_a = (-RG_C) * r * _softplus(-lam_ref[...])
    a = jnp.exp(log_a)
    b = jnp.sqrt(1.0 - jnp.exp(2.0 * log_a)) * (ig * u)
    rows = lax.broadcasted_iota(jnp.int32, (ts, 1), 0)
    d = 1
    while d < ts:
        keep = rows >= d
        b = jnp.where(keep, a * pltpu.roll(b, d, 0) + b, b)
        a = jnp.where(keep, a * pltpu.roll(a, d, 0), a)
        d *= 2
    h = a * hcar[0:1, :] + b
    hcar[0:1, :] = h[ts - 1:ts, :]
    gt = gate_ref[...].astype(F32)
    gelu = 0.5 * gt * (1.0 + jnp.tanh(math.sqrt(2.0 / math.pi) * (gt + 0.044715 * (gt * gt * gt))))
    o_ref[...] = (h * gelu).astype(o_ref.dtype)


def _rglru(proj, nb, seq, conv_w, conv_b, w_a, b_a, w_x, b_x, lam, *, ts):
    width = conv_w.shape[1]
    nt = seq // ts
    row = lambda v: v.reshape(1, width)
    full2 = lambda shp: pl.BlockSpec(shp, lambda b, t: (0,) * len(shp))
    return _pcall(
        functools.partial(_rglru_body, ts=ts, width=width),
        grid=(nb, nt),
        in_specs=[pl.BlockSpec((ts, width), lambda b, t: (b * nt + t, 0)),
                  pl.BlockSpec((ts, width), lambda b, t: (b * nt + t, 1)),
                  full2(conv_w.shape), full2((1, width)), full2(w_a.shape), full2((1, width)),
                  full2(w_x.shape), full2((1, width)), full2((1, width))],
        out_specs=pl.BlockSpec((ts, width), lambda b, t: (b * nt + t, 0)),
        out_shape=jax.ShapeDtypeStruct((nb * seq, width), BF16),
        scratch_shapes=[pltpu.VMEM((ts + 8, width), F32), pltpu.VMEM((8, width), F32)],
        compiler_params=_cparams("parallel", "arbitrary"),
    )(proj, proj, conv_w, row(conv_b), w_a.astype(BF16), row(b_a), w_x.astype(BF16), row(b_x), row(lam))


def _sb_body(q_ref, k_ref, v_ref, qg_ref, kg_ref, o_ref, kn_ref, *, tq, scale, hp):
    qi = pl.program_id(2)
    hs = range(hp)
    sl = lambda h: slice(h * HEAD_DIM, (h + 1) * HEAD_DIM)

    @pl.when(qi == 0)
    def _():
        for h in hs:
            kn_ref[:, sl(h)] = _rms(k_ref[:, sl(h)].astype(F32), kg_ref[...]).astype(BF16)

    qn = [(_rms(q_ref[:, sl(h)].astype(F32), qg_ref[...]) * scale).astype(BF16) for h in hs]
    rr = lax.broadcasted_iota(jnp.int32, (tq, tq), 0)
    cc = lax.broadcasted_iota(jnp.int32, (tq, tq), 1)
    upper = (rr > cc).astype(BF16)

    def block(j, accs, runs, diag):
        ks = pl.multiple_of(j * tq, tq)
        z = [_dot_nt(qn[h], kn_ref[pl.ds(ks, tq), sl(h)]) for h in hs]
        sp = [_softplus(z[h]) for h in hs]
        if diag:
            vis = cc < rr
            lk = [jnp.where(vis, -sp[h], 0.0) for h in hs]
        else:
            lk = [-sp[h] for h in hs]
        hi = [lk[h].astype(BF16) for h in hs]
        lo = [(lk[h] - hi[h].astype(F32)).astype(BF16) for h in hs]
        cs = [_dot(hi[h], upper) + _dot(lo[h], upper) for h in hs]
        w = [jnp.exp(z[h] - sp[h] + cs[h] + runs[h]) for h in hs]
        if diag:
            w = [jnp.where(vis, w[h], 0.0) for h in hs]
        accs = tuple(accs[h] + _dot(w[h].astype(BF16), v_ref[pl.ds(ks, tq), sl(h)]) for h in hs)
        runs = tuple(runs[h] + jnp.sum(lk[h], axis=-1, keepdims=True) for h in hs)
        return accs, runs

    accs, runs = block(qi, tuple(jnp.zeros((tq, HEAD_DIM), F32) for _ in hs),
                       tuple(jnp.zeros((tq, 1), F32) for _ in hs), True)

    def more(c):
        return (c[0] >= 0) & (jnp.max(functools.reduce(jnp.maximum, c[2])) > EXP_UNDERFLOW)

    def step(c):
        a, r = block(c[0], c[1], c[2], False)
        return c[0] - 1, a, r

    _, accs, _ = lax.while_loop(more, step, (qi - 1, accs, runs))
    for h in hs:
        o_ref[:, sl(h)] = accs[h].astype(o_ref.dtype)


def _sb_attention(proj3, q_gain, k_gain, *, heads, q_off, tq):
    nb, seq, _ = proj3.shape
    hp = _pick(heads, (4, 2, 1))
    wblk = hp * HEAD_DIM
    bw = heads * HEAD_DIM
    assert q_off % wblk == 0 and bw % wblk == 0
    qb, kb, vb = q_off // wblk, (q_off + bw) // wblk, (q_off + 2 * bw) // wblk
    return _pcall(
        functools.partial(_sb_body, tq=tq, scale=HEAD_DIM ** -0.5, hp=hp),
        grid=(nb, heads // hp, seq // tq),
        in_specs=[pl.BlockSpec((None, tq, wblk), lambda b, h, i: (b, i, qb + h)),
                  pl.BlockSpec((None, seq, wblk), lambda b, h, i: (b, 0, kb + h)),
                  pl.BlockSpec((None, seq, wblk), lambda b, h, i: (b, 0, vb + h)),
                  pl.BlockSpec((1, HEAD_DIM), lambda b, h, i: (0, 0)),
                  pl.BlockSpec((1, HEAD_DIM), lambda b, h, i: (0, 0))],
        out_specs=pl.BlockSpec((None, tq, wblk), lambda b, h, i: (b, i, h)),
        out_shape=jax.ShapeDtypeStruct((nb, seq, bw), BF16),
        scratch_shapes=[pltpu.VMEM((seq, wblk), BF16)],
        compiler_params=_cparams("parallel", "parallel", "arbitrary"),
    )(proj3, proj3, proj3, q_gain.reshape(1, HEAD_DIM), k_gain.reshape(1, HEAD_DIM))


def _rope_pad(r, gain, cos_t, sin_a, sin_b):
    ms = jnp.sum(r * r, axis=-1, keepdims=True) * (1.0 / MLA_ROPE)
    rn = r * lax.rsqrt(ms + NORM_EPS) * gain
    return rn * cos_t + pltpu.roll(rn, LANES - MLA_ROPE // 2, 1) * sin_a + pltpu.roll(rn, MLA_ROPE // 2, 1) * sin_b


def _mla_q_body(cq_ref, g_ref, w_ref, qg_ref, cos_ref, sa_ref, sb_ref, o_ref, xn_ref):
    @pl.when(pl.program_id(1) == 0)
    def _():
        xn_ref[...] = _rms(cq_ref[...].astype(F32), g_ref[...]).astype(BF16)

    q = _dot(xn_ref[...], w_ref[...])
    qg = qg_ref[...]
    o_ref[:, :HEAD_DIM] = _rms(q[:, :HEAD_DIM], qg[:, :HEAD_DIM]).astype(o_ref.dtype)
    o_ref[:, HEAD_DIM:] = _rope_pad(q[:, HEAD_DIM:], qg[:, HEAD_DIM:], cos_ref[...], sa_ref[...],
                                    sb_ref[...]).astype(o_ref.dtype)


def _mla_q(proj, cq_gain, w_q, q_gain2, tabs, *, heads, cq_blk, nb, seq, tm):
    rank = w_q.shape[0]
    nt = seq // tm
    tab = pl.BlockSpec((tm, LANES), lambda i, j: (i % nt, 0))
    return _pcall(
        _mla_q_body,
        grid=(nb * nt, heads),
        in_specs=[pl.BlockSpec((tm, rank), lambda i, j: (i, cq_blk)),
                  pl.BlockSpec((1, rank), lambda i, j: (0, 0)),
                  pl.BlockSpec((rank, 2 * HEAD_DIM), lambda i, j: (0, j)),
                  pl.BlockSpec((1, 2 * HEAD_DIM), lambda i, j: (0, 0)),
                  tab, tab, tab],
        out_specs=pl.BlockSpec((tm, 2 * HEAD_DIM), lambda i, j: (i, j)),
        out_shape=jax.ShapeDtypeStruct((nb * seq, heads * 2 * HEAD_DIM), BF16),
        scratch_shapes=[pltpu.VMEM((tm, rank), BF16)],
        compiler_params=_cparams("parallel", "arbitrary"),
    )(proj, cq_gain.reshape(1, rank), w_q, q_gain2, *tabs)


def _mla_kv_body(ckv_ref, g_ref, w_ref, kg_ref, kr_ref, krg_ref, cos_ref, sa_ref, sb_ref,
                 k_ref, v_ref, xn_ref, kro_ref):
    @pl.when(pl.program_id(1) == 0)
    def _():
        xn_ref[...] = _rms(ckv_ref[...].astype(F32), g_ref[...]).astype(BF16)
        kro_ref[...] = _rope_pad(kr_ref[...].astype(F32), krg_ref[...], cos_ref[...], sa_ref[...],
                                 sb_ref[...]).astype(kro_ref.dtype)

    kv = _dot(xn_ref[...], w_ref[...])
    k_ref[:, :HEAD_DIM] = _rms(kv[:, :HEAD_DIM], kg_ref[...]).astype(k_ref.dtype)
    k_ref[:, HEAD_DIM:] = kro_ref[...]
    v_ref[...] = kv[:, HEAD_DIM:].astype(v_ref.dtype)


def _mla_kv(proj, ckv_gain, w_kv, k_gain_n, k_gain_r, tabs, *, heads, ckv_blk, kr_blk, nb, seq, tm):
    rank = w_kv.shape[0]
    nt = seq // tm
    tab = pl.BlockSpec((tm, LANES), lambda i, j: (i % nt, 0))
    return _pcall(
        _mla_kv_body,
        grid=(nb * nt, heads),
        in_specs=[pl.BlockSpec((tm, rank), lambda i, j: (i, ckv_blk)),
                  pl.BlockSpec((1, rank), lambda i, j: (0, 0)),
                  pl.BlockSpec((rank, 2 * HEAD_DIM), lambda i, j: (0, j)),
                  pl.BlockSpec((1, HEAD_DIM), lambda i, j: (0, 0)),
                  pl.BlockSpec((tm, LANES), lambda i, j: (i, kr_blk)),
                  pl.BlockSpec((1, LANES), lambda i, j: (0, 0)),
                  tab, tab, tab],
        out_specs=[pl.BlockSpec((tm, 2 * HEAD_DIM), lambda i, j: (i, j)),
                   pl.BlockSpec((tm, HEAD_DIM), lambda i, j: (i, j))],
        out_shape=[jax.ShapeDtypeStruct((nb * seq, heads * 2 * HEAD_DIM), BF16),
                   jax.ShapeDtypeStruct((nb * seq, heads * HEAD_DIM), BF16)],
        scratch_shapes=[pltpu.VMEM((tm, rank), BF16), pltpu.VMEM((tm, LANES), BF16)],
        compiler_params=_cparams("parallel", "arbitrary"),
    )(proj, ckv_gain.reshape(1, rank), w_kv, k_gain_n, proj, k_gain_r, *tabs)


def _mla_attn_body(q_ref, k_ref, v_ref, o_ref, *, tq, tk, hp):
    qi = pl.program_id(2)
    hs = range(hp)
    qk = lambda h: slice(h * 2 * HEAD_DIM, (h + 1) * 2 * HEAD_DIM)
    vs = lambda h: slice(h * HEAD_DIM, (h + 1) * HEAD_DIM)
    q = [q_ref[:, qk(h)] for h in hs]
    rr = lax.broadcasted_iota(jnp.int32, (tq, tk), 0)
    cc = lax.broadcasted_iota(jnp.int32, (tq, tk), 1)
    nsub = tq // tk

    def block(j, carry, diag):
        m, l, acc = carry
        ks = pl.multiple_of(j * tk, tk)
        s = [_dot_nt(q[h], k_ref[pl.ds(ks, tk), qk(h)]) for h in hs]
        if diag:
            vis = cc + (j - qi * nsub) * tk <= rr
            s = [jnp.where(vis, s[h], NEG_INF) for h in hs]
        m_new = tuple(jnp.maximum(m[h], jnp.max(s[h], axis=-1, keepdims=True)) for h in hs)
        alpha = [jnp.exp(m[h] - m_new[h]) for h in hs]
        p = [jnp.exp(s[h] - m_new[h]) for h in hs]
        l = tuple(alpha[h] * l[h] + jnp.sum(p[h], axis=-1, keepdims=True) for h in hs)
        acc = tuple(alpha[h] * acc[h] + _dot(p[h].astype(BF16), v_ref[pl.ds(ks, tk), vs(h)]) for h in hs)
        return m_new, l, acc

    init = (tuple(jnp.full((tq, 1), NEG_INF, F32) for _ in hs),
            tuple(jnp.zeros((tq, 1), F32) for _ in hs),
            tuple(jnp.zeros((tq, HEAD_DIM), F32) for _ in hs))
    carry = lax.fori_loop(0, qi * nsub, lambda j, c: block(j, c, False), init)
    for d in range(nsub):
        carry = block(qi * nsub + d, carry, True)
    _, l, acc = carry
    for h in hs:
        o_ref[:, vs(h)] = (acc[h] / l[h]).astype(o_ref.dtype)


def _mla_attention(q, k, v, *, heads, tq, tk):
    nb, seq, _ = q.shape
    hp = _pick(heads, (2, 1))
    return _pcall(
        functools.partial(_mla_attn_body, tq=tq, tk=tk, hp=hp),
        grid=(nb, heads // hp, seq // tq),
        in_specs=[pl.BlockSpec((None, tq, hp * 2 * HEAD_DIM), lambda b, h, i: (b, i, h)),
                  pl.BlockSpec((None, seq, hp * 2 * HEAD_DIM), lambda b, h, i: (b, 0, h)),
                  pl.BlockSpec((None, seq, hp * HEAD_DIM), lambda b, h, i: (b, 0, h))],
        out_specs=pl.BlockSpec((None, tq, hp * HEAD_DIM), lambda b, h, i: (b, i, h)),
        out_shape=jax.ShapeDtypeStruct((nb, seq, heads * HEAD_DIM), BF16),
        compiler_params=_cparams("parallel", "parallel", "arbitrary"),
    )(q, k, v)


def _dn_prep_body(x_ref, bd_ref, cw_ref, alog_ref, dtb_ref, o_ref, gb_ref, xbuf, *, ts, heads):
    width = 3 * heads * HEAD_DIM

    @pl.when(pl.program_id(1) == 0)
    def _():
        xbuf[0:8, :] = jnp.zeros((8, width), F32)

    y = _causal_conv(xbuf, x_ref[...].astype(F32), cw_ref[...], ts)
    y = y * _sigmoid(y)
    qscale = HEAD_DIM ** -0.5
    for hh in range(3 * heads):
        blk = y[:, hh * HEAD_DIM:(hh + 1) * HEAD_DIM]
        if hh < 2 * heads:
            blk = blk * lax.rsqrt(jnp.sum(blk * blk, axis=-1, keepdims=True) + NORM_EPS)
            if hh < heads:
                blk = blk * qscale
        o_ref[:, hh * HEAD_DIM:(hh + 1) * HEAD_DIM] = blk
    bd = bd_ref[...]
    beta = _sigmoid(bd[:, :LANES])
    g = -jnp.exp(alog_ref[...]) * _softplus(bd[:, LANES:] + dtb_ref[...])
    rows = lax.broadcasted_iota(jnp.int32, (ts, 1), 0) % DN_CHUNK
    d = 1
    while d < DN_CHUNK:
        g = g + jnp.where(rows >= d, pltpu.roll(g, d, 0), 0.0)
        d *= 2
    gb_ref[:, :LANES] = g
    gb_ref[:, LANES:] = beta


def _dn_prep(proj, bd, conv_w, a_log, dt_bias, *, heads, qkv_blk, nb, seq, ts):
    width = 3 * heads * HEAD_DIM
    nt = seq // ts
    pad = lambda v: jnp.zeros((1, LANES), F32).at[0, :heads].set(v)
    return _pcall(
        functools.partial(_dn_prep_body, ts=ts, heads=heads),
        grid=(nb, nt),
        in_specs=[pl.BlockSpec((ts, width), lambda b, t: (b * nt + t, qkv_blk)),
                  pl.BlockSpec((ts, 2 * LANES), lambda b, t: (b * nt + t, 0)),
                  pl.BlockSpec(conv_w.shape, lambda b, t: (0, 0)),
                  pl.BlockSpec((1, LANES), lambda b, t: (0, 0)),
                  pl.BlockSpec((1, LANES), lambda b, t: (0, 0))],
        out_specs=[pl.BlockSpec((ts, width), lambda b, t: (b * nt + t, 0)),
                   pl.BlockSpec((ts, 2 * LANES), lambda b, t: (b * nt + t, 0))],
        out_shape=[jax.ShapeDtypeStruct((nb * seq, width), F32),
                   jax.ShapeDtypeStruct((nb * seq, 2 * LANES), F32)],
        scratch_shapes=[pltpu.VMEM((ts + 8, width), F32)],
        compiler_params=_cparams("parallel", "arbitrary"),
    )(proj, bd, conv_w, pad(a_log), pad(dt_bias))


def _bdot(a, b):
    return _dot(a.astype(BF16), b.astype(BF16))


def _dn_body(qkv_ref, gb_ref, z_ref, og_ref, o_ref, state, *, heads):
    c = DN_CHUNK
    hd = HEAD_DIM

    @pl.when(pl.program_id(1) == 0)
    def _():
        state[...] = jnp.zeros_like(state)

    rr = lax.broadcasted_iota(jnp.int32, (c, c), 0)
    cc = lax.broadcasted_iota(jnp.int32, (c, c), 1)
    lower = rr >= cc
    strict = rr > cc
    eye = (rr == cc).astype(F32)
    same = lambda sh: jnp.right_shift(rr, sh) == jnp.right_shift(cc, sh)
    base_sh = 4
    in_base = strict & same(base_sh)
    off_levels = [strict & same(sh + 1) & jnp.logical_not(same(sh))
                  for sh in range(base_sh, int(math.log2(c)))]
    gb = gb_ref[...]
    hs = range(heads)
    q = [qkv_ref[:, h * hd:(h + 1) * hd] for h in hs]
    k = [qkv_ref[:, (heads + h) * hd:(heads + h + 1) * hd] for h in hs]
    v = [qkv_ref[:, (2 * heads + h) * hd:(2 * heads + h + 1) * hd] for h in hs]
    gcol = [gb[:, h:h + 1] for h in hs]
    bcol = [gb[:, LANES + h:LANES + h + 1] for h in hs]
    gcb = [jnp.broadcast_to(gcol[h], (c, c)) for h in hs]
    dec = [jnp.where(lower, jnp.exp(jnp.where(lower, gcb[h] - gcb[h].T, 0.0)), 0.0) for h in hs]
    kb = [k[h] * bcol[h] for h in hs]
    kq = [_dot_nt(jnp.concatenate([kb[h], q[h]], axis=0).astype(BF16), k[h].astype(BF16)) for h in hs]
    a_mat = [jnp.where(strict, kq[h][:c] * dec[h], 0.0) for h in hs]
    intra = [kq[h][c:] * dec[h] for h in hs]
    pw = [-jnp.where(in_base, a_mat[h], 0.0) for h in hs]
    p_mat = [eye + pw[h] for h in hs]
    for _ in range(base_sh - 1):
        pw = [_bdot(pw[h], pw[h]) for h in hs]
        p_mat = [p_mat[h] + _bdot(p_mat[h], pw[h]) for h in hs]
    for off in off_levels:
        pc = [_bdot(p_mat[h], jnp.where(off, a_mat[h], 0.0)) for h in hs]
        p_mat = [p_mat[h] - _bdot(pc[h], p_mat[h]) for h in hs]
    eg = [jnp.exp(gcol[h]) for h in hs]
    uw = [_bdot(p_mat[h], jnp.concatenate([v[h] * bcol[h], kb[h] * eg[h]], axis=1)) for h in hs]
    glast = [gcol[h][c - 1:c, :] for h in hs]
    s_old = [state[h] for h in hs]
    ws = [_bdot(jnp.concatenate([uw[h][:, hd:], q[h] * eg[h]], axis=0), s_old[h]) for h in hs]
    v_new = [uw[h][:, :hd] - ws[h][:c] for h in hs]
    o = [ws[h][c:] + _bdot(intra[h], v_new[h]) for h in hs]
    k_dec = [k[h] * jnp.exp(glast[h] - gcol[h]) for h in hs]
    for h in hs:
        state[h] = s_old[h] * jnp.exp(glast[h]) + _bdot(k_dec[h].T, v_new[h])
    for h in hs:
        zz = z_ref[:, h * hd:(h + 1) * hd].astype(F32)
        o_ref[:, h * hd:(h + 1) * hd] = (_rms(o[h], og_ref[...]) * (zz * _sigmoid(zz))).astype(o_ref.dtype)


def _delta_rule(qkv, gb, proj, out_gain, *, heads, z_blk, nb, seq):
    c = DN_CHUNK
    nt = seq // c
    hw = heads * HEAD_DIM
    return _pcall(
        functools.partial(_dn_body, heads=heads),
        grid=(nb, nt),
        in_specs=[pl.BlockSpec((c, 3 * hw), lambda b, t: (b * nt + t, 0)),
                  pl.BlockSpec((c, 2 * LANES), lambda b, t: (b * nt + t, 0)),
                  pl.BlockSpec((c, hw), lambda b, t: (b * nt + t, z_blk)),
                  pl.BlockSpec((1, HEAD_DIM), lambda b, t: (0, 0))],
        out_specs=pl.BlockSpec((c, hw), lambda b, t: (b * nt + t, 0)),
        out_shape=jax.ShapeDtypeStruct((nb * seq, hw), BF16),
        scratch_shapes=[pltpu.VMEM((heads, HEAD_DIM, HEAD_DIM), F32)],
        compiler_params=_cparams("parallel", "arbitrary"),
    )(qkv, gb, proj, out_gain.reshape(1, HEAD_DIM))


def _merge_body(ya_ref, yb_ref, yc_ref, yd_ref, g0_ref, g1_ref, g2_ref, g3_ref, wb_ref, o_ref):
    acc = None
    for i, (y_ref, gl_ref) in enumerate(((ya_ref, g0_ref), (yb_ref, g1_ref), (yc_ref, g2_ref),
                                         (yd_ref, g3_ref))):
        t = _sigmoid(gl_ref[...].astype(F32)) * _dot(y_ref[...], wb_ref[i])
        acc = t if acc is None else acc + t
    o_ref[...] = acc.astype(o_ref.dtype)


def _gated_merge(ys, proj, w_branch, *, gate_off, tm, tn):
    m, bw = ys[0].shape
    nbr, _, n = w_branch.shape
    yspec = pl.BlockSpec((tm, bw), lambda i, j: (i, 0))
    gspecs = [pl.BlockSpec((tm, tn), functools.partial(lambda i, j, o: (i, o + j), o=(gate_off + r * n) // tn))
              for r in range(nbr)]
    return _pcall(
        _merge_body,
        grid=(m // tm, n // tn),
        in_specs=[yspec] * nbr + gspecs + [pl.BlockSpec((nbr, bw, tn), lambda i, j: (0, 0, j))],
        out_specs=pl.BlockSpec((tm, tn), lambda i, j: (i, j)),
        out_shape=jax.ShapeDtypeStruct((m, n), BF16),
        compiler_params=_cparams("parallel", "arbitrary"),
    )(*ys, *([proj] * nbr), w_branch)


def _mem_kv_body(m_ref, g_ref, w_ref, kg_ref, k_ref, v_ref, *, heads):
    xn = _rms(m_ref[...], g_ref[...]).astype(BF16)
    kv = _dot(xn, w_ref[...])
    hw = heads * HEAD_DIM
    for h in range(heads):
        k_ref[:, h * HEAD_DIM:(h + 1) * HEAD_DIM] = _rms(kv[:, h * HEAD_DIM:(h + 1) * HEAD_DIM],
                                                         kg_ref[...]).astype(k_ref.dtype)
    v_ref[...] = kv[:, hw:].astype(v_ref.dtype)


def _mem_kv(mem2, g, wkv, k_gain, *, nb, mlen, heads):
    d = mem2.shape[1]
    hw = heads * HEAD_DIM
    out = pl.BlockSpec((mlen, hw), lambda b: (b, 0))
    shp = jax.ShapeDtypeStruct((nb * mlen, hw), BF16)
    return _pcall(
        functools.partial(_mem_kv_body, heads=heads),
        grid=(nb,),
        in_specs=[pl.BlockSpec((mlen, d), lambda b: (b, 0)),
                  pl.BlockSpec((1, d), lambda b: (0, 0)),
                  pl.BlockSpec((d, 2 * hw), lambda b: (0, 0)),
                  pl.BlockSpec((1, HEAD_DIM), lambda b: (0, 0))],
        out_specs=[out, out],
        out_shape=[shp, shp],
        compiler_params=_cparams("parallel"),
    )(mem2, g.reshape(1, d), wkv, k_gain.reshape(1, HEAD_DIM))


def _xattn_body(x_ref, g_ref, wq_ref, k_ref, v_ref, qg_ref, wo_ref, o_ref, *, heads):
    x = x_ref[...]
    q = _dot(_rms(x, g_ref[...]).astype(BF16), wq_ref[...])
    scale = HEAD_DIM ** -0.5
    outs = []
    for h in range(heads):
        sl = slice(h * HEAD_DIM, (h + 1) * HEAD_DIM)
        qh = (_rms(q[:, sl], qg_ref[...]) * scale).astype(BF16)
        s = _dot_nt(qh, k_ref[:, sl])
        p = jnp.exp(s - jnp.max(s, axis=-1, keepdims=True))
        l = jnp.sum(p, axis=-1, keepdims=True)
        outs.append(_dot(p.astype(BF16), v_ref[:, sl]) / l)
    o = jnp.concatenate(outs, axis=-1).astype(BF16)
    o_ref[...] = x + _dot(o, wo_ref[...])


def _cross_attention(x2, g, wq, k, v, q_gain, wo, *, nb, seq, mlen, heads, tm):
    d = x2.shape[1]
    hw = heads * HEAD_DIM
    nt = seq // tm
    xspec = pl.BlockSpec((tm, d), lambda b, t: (b * nt + t, 0))
    return _pcall(
        functools.partial(_xattn_body, heads=heads),
        grid=(nb, nt),
        in_specs=[xspec,
                  pl.BlockSpec((1, d), lambda b, t: (0, 0)),
                  pl.BlockSpec((d, hw), lambda b, t: (0, 0)),
                  pl.BlockSpec((mlen, hw), lambda b, t: (b, 0)),
                  pl.BlockSpec((mlen, hw), lambda b, t: (b, 0)),
                  pl.BlockSpec((1, HEAD_DIM), lambda b, t: (0, 0)),
                  pl.BlockSpec((hw, d), lambda b, t: (0, 0))],
        out_specs=xspec,
        out_shape=jax.ShapeDtypeStruct(x2.shape, F32),
        compiler_params=_cparams("parallel", "arbitrary"),
    )(x2, g.reshape(1, d), wq, k, v, q_gain.reshape(1, HEAD_DIM), wo)


def _router_body(x_ref, g_ref, w_ref, o_ref, *, n_experts):
    logits = _dot(_rms(x_ref[...], g_ref[...]), w_ref[...], HIGHEST)
    lane = lax.broadcasted_iota(jnp.int32, logits.shape, 1)
    lg = jnp.where(lane < n_experts, logits, -jnp.inf)
    m1 = jnp.max(lg, axis=-1, keepdims=True)
    i1 = jnp.min(jnp.where(lg == m1, lane, LANES), axis=-1, keepdims=True)
    lg2 = jnp.where(lane == i1, -jnp.inf, lg)
    m2 = jnp.max(lg2, axis=-1, keepdims=True)
    i2 = jnp.min(jnp.where(lg2 == m2, lane, LANES), axis=-1, keepdims=True)
    e2 = jnp.exp(m2 - m1)
    den = 1.0 + e2
    o_ref[...] = jnp.where(lane == i1, 1.0 / den, 0.0) + jnp.where(lane == i2, e2 / den, 0.0)


def _router(x2, g, w_router, *, tm):
    m, d = x2.shape
    ne = w_router.shape[1]
    wpad = jnp.zeros((d, LANES), F32).at[:, :ne].set(w_router)
    return _pcall(
        functools.partial(_router_body, n_experts=ne),
        grid=(m // tm,),
        in_specs=[pl.BlockSpec((tm, d), lambda i: (i, 0)),
                  pl.BlockSpec((1, d), lambda i: (0, 0)),
                  pl.BlockSpec((d, LANES), lambda i: (0, 0))],
        out_specs=pl.BlockSpec((tm, LANES), lambda i: (i, 0)),
        out_shape=jax.ShapeDtypeStruct((m, LANES), F32),
        compiler_params=_cparams("parallel"),
    )(x2, g.reshape(1, d), wpad)


def _rope_tables(seq):
    half = MLA_ROPE // 2
    inv_freq = ROPE_BASE ** (-jnp.arange(half, dtype=F32) / half)
    ang = jnp.arange(seq, dtype=jnp.int32).astype(F32)[:, None] * inv_freq[None, :]
    cos, sin = jnp.cos(ang), jnp.sin(ang)
    z = jnp.zeros((seq, half), F32)
    cos_t = jnp.concatenate([cos, cos, z, z], axis=-1)
    sin_a = jnp.concatenate([-sin, z, z, z], axis=-1)
    sin_b = jnp.concatenate([z, sin, z, z], axis=-1)
    return cos_t, sin_a, sin_b


def kernel(x, mem, mix_norm, w_in, rg_conv_w, rg_conv_b, rg_w_a, rg_b_a, rg_w_x, rg_b_x, rg_lambda, sb_q_gain, sb_k_gain, mla_cq_gain, mla_w_uq, mla_ckv_gain, mla_w_ukv, mla_q_gain, mla_k_gain, dn_conv_w, dn_a_log, dn_dt_bias, dn_out_gain, w_branch, w_out, xa_norm, mem_norm, xa_wq, xa_wkv, xa_q_gain, xa_k_gain, xa_wo, ffn_norm, ffn_w_gu, ffn_w_down, moe_router, moe_w_gu, moe_w_down):
    nb, seq, d = x.shape
    depth = w_in.shape[0]
    mlen = mem.shape[1]
    bw = rg_conv_w.shape[2]
    heads = bw // HEAD_DIM
    q_rank = mla_cq_gain.shape[1]
    kv_rank = mla_ckv_gain.shape[1]
    xa_heads = xa_wq.shape[2] // HEAD_DIM
    t_tok = nb * seq
    assert bw % HEAD_DIM == 0 and seq % 512 == 0 and d % 512 == 0
    assert mla_q_gain.shape[1] == HEAD_DIM + MLA_ROPE

    groups = (("rg", 2 * bw, bw), ("sb", 3 * bw, _pick(heads, (4, 2, 1)) * HEAD_DIM), ("cq", q_rank, q_rank),
              ("ckv", kv_rank, kv_rank), ("kr", MLA_ROPE, LANES), ("dnqkv", 3 * bw, 3 * bw),
              ("dnz", bw, bw), ("beta", heads, None), ("dt", heads, None), ("gate", 4 * d, d))
    src, off, pieces = {}, {}, []
    s_pos = p_pos = 0
    for name, width, align in groups:
        src[name] = (s_pos, s_pos + width)
        s_pos += width
        if align is None:
            continue
        start = -(-p_pos // align) * align
        pieces.append((start - p_pos, src[name]))
        off[name] = start
        p_pos = start + width
        if name == "kr":
            pieces.append((LANES - width, None))
            p_pos += LANES - width
    n_proj = -(-p_pos // 512) * 512
    pieces.append((n_proj - p_pos, None))
    assert s_pos == w_in.shape[2]

    x2 = x.reshape(t_tok, d)
    mem2 = mem.reshape(nb * mlen, d)
    tabs = _rope_tables(seq)
    mla_scale = (HEAD_DIM + MLA_ROPE) ** -0.5

    tm_big = _pick(t_tok, (1024, 512, 256, 128))
    tm_mid = _pick(t_tok, (512, 256, 128))
    ts_seq = _pick(seq, (512, 256, 128))

    for layer in range(depth):
        wi = w_in[layer]
        zc = lambda n: jnp.zeros((d, n), wi.dtype)
        cols = []
        for gap, rng in pieces:
            if gap:
                cols.append(zc(gap))
            if rng is not None:
                cols.append(wi[:, rng[0]:rng[1]])
        w_proj = jnp.concatenate(cols, axis=1).astype(BF16)
        w_bd = jnp.concatenate([wi[:, src["beta"][0]:src["beta"][1]], zc(LANES - heads),
                                wi[:, src["dt"][0]:src["dt"][1]], zc(LANES - heads)], axis=1)
        wq3 = mla_w_uq[layer].reshape(q_rank, heads, HEAD_DIM + MLA_ROPE)
        w_q = jnp.concatenate([wq3, jnp.zeros((q_rank, heads, HEAD_DIM - MLA_ROPE), F32)],
                              axis=-1).reshape(q_rank, heads * 2 * HEAD_DIM).astype(BF16)
        qg = mla_q_gain[layer] * mla_scale
        q_gain2 = jnp.concatenate([qg, jnp.zeros((HEAD_DIM - MLA_ROPE,), F32)]).reshape(1, 2 * HEAD_DIM)
        kgl = mla_k_gain[layer]
        k_gain_n = kgl[:HEAD_DIM].reshape(1, HEAD_DIM)
        k_gain_r = jnp.concatenate([kgl[HEAD_DIM:], jnp.zeros((LANES - MLA_ROPE,), F32)]).reshape(1, LANES)

        proj = _norm_matmul(x2, mix_norm[layer], w_proj, out_dtype=BF16, tm=tm_big,
                            tn=_pick(n_proj, (512, 256, 128)))
        bd = _norm_matmul(x2, mix_norm[layer], w_bd, out_dtype=F32, tm=tm_big, tn=2 * LANES,
                          precision=HIGHEST)
        proj3 = proj.reshape(nb, seq, n_proj)

        y_a = _rglru(proj, nb, seq, rg_conv_w[layer], rg_conv_b[layer], rg_w_a[layer], rg_b_a[layer],
                     rg_w_x[layer], rg_b_x[layer], rg_lambda[layer], ts=_pick(seq, (256, 128)))

        y_b = _sb_attention(proj3, sb_q_gain[layer], sb_k_gain[layer], heads=heads,
                            q_off=off["sb"], tq=_pick(seq, (256, 128)))

        mq = _mla_q(proj, mla_cq_gain[layer], w_q, q_gain2, tabs, heads=heads,
                    cq_blk=off["cq"] // q_rank, nb=nb, seq=seq, tm=ts_seq)
        mk, mv = _mla_kv(proj, mla_ckv_gain[layer], mla_w_ukv[layer].astype(BF16), k_gain_n,
                         k_gain_r, tabs, heads=heads, ckv_blk=off["ckv"] // kv_rank,
                         kr_blk=off["kr"] // LANES, nb=nb, seq=seq, tm=ts_seq)
        r3 = lambda a: a.reshape(nb, seq, a.shape[-1])
        y_c = _mla_attention(r3(mq), r3(mk), r3(mv), heads=heads, tq=ts_seq, tk=ts_seq)

        dn_qkv, dn_gb = _dn_prep(proj, bd, dn_conv_w[layer], dn_a_log[layer], dn_dt_bias[layer],
                                 heads=heads, qkv_blk=off["dnqkv"] // (3 * bw), nb=nb, seq=seq, ts=ts_seq)
        y_d = _delta_rule(dn_qkv, dn_gb, proj, dn_out_gain[layer], heads=heads, z_blk=off["dnz"] // bw,
                          nb=nb, seq=seq)

        merged = _gated_merge([y_a, y_b.reshape(t_tok, bw), y_c.reshape(t_tok, bw), y_d], proj,
                              w_branch[layer].astype(BF16), gate_off=off["gate"], tm=tm_mid,
                              tn=_pick(d, (512, 256, 128)))
        x2 = _matmul_residual(merged, w_out[layer].astype(BF16), x2, tm=tm_mid,
                              tn=_pick(d, (1024, 512, 256, 128)), tk=d)

        mk, mv = _mem_kv(mem2, mem_norm[layer], xa_wkv[layer].astype(BF16), xa_k_gain[layer],
                         nb=nb, mlen=mlen, heads=xa_heads)
        x2 = _cross_attention(x2, xa_norm[layer], xa_wq[layer].astype(BF16), mk, mv, xa_q_gain[layer],
                              xa_wo[layer].astype(BF16), nb=nb, seq=seq, mlen=mlen, heads=xa_heads,
                              tm=ts_seq)

        if layer % 2 == 0:
            wgu = ffn_w_gu[layer // 2].astype(BF16)
            f = wgu.shape[1] // 2
            hmid = _norm_swiglu(x2, ffn_norm[layer], wgu, tm=tm_mid, tn=_pick(f, (512, 256, 128)))
            x2 = _matmul_residual(hmid, ffn_w_down[layer // 2].astype(BF16), x2, tm=tm_mid,
                                  tn=_pick(d, (1024, 512, 256, 128)), tk=_pick(f, (1408, 512, 256, 128)))
        else:
            wgu = moe_w_gu[layer // 2].astype(BF16)
            ne, _, f2 = wgu.shape
            f = f2 // 2
            combine = _router(x2, ffn_norm[layer], moe_router[layer // 2], tm=tm_mid)
            hmid = _moe_swiglu(x2, ffn_norm[layer], combine, wgu, tm=tm_mid, tn=_pick(f, (1408, 512, 256, 128)))
            wdn = moe_w_down[layer // 2].astype(BF16).reshape(ne * f, d)
            x2 = _matmul_residual(hmid, wdn, x2, tm=tm_mid, tn=_pick(d, (1024, 512, 256, 128)),
                                  tk=_pick(f, (1408, 512, 256, 128)))
    return x2.reshape(nb, seq, d)
```

```python
import functools
import math

import jax
import jax.numpy as jnp
import numpy as np
from jax import lax
from jax.experimental import pallas as pl
from jax.experimental.pallas import tpu as pltpu

F32 = jnp.float32
BF16 = jnp.bfloat16
HIGHEST = lax.Precision.HIGHEST

HEAD_DIM = 128
LANES = 128
MLA_ROPE = 64
RG_C = 8.0
ROPE_BASE = 10000.0
NORM_EPS = 1e-6
NEG_INF = -1e30
EXP_UNDERFLOW = -104.0
DN_CHUNK = 128
TOP_K = 2
VMEM_LIMIT = 56 * 1024 * 1024


def _cparams(*sem):
    return pltpu.CompilerParams(dimension_semantics=sem, vmem_limit_bytes=VMEM_LIMIT)


def _pcall(body, **kwargs):
    fn = getattr(body, "func", body)
    return pl.pallas_call(body, name=fn.__name__.strip("_").replace("_body", ""), **kwargs)


def _pick(n, prefs):
    for p in prefs:
        if p <= n and n % p == 0:
            return p
    return n


def _rms(x, g):
    return x * lax.rsqrt(jnp.mean(x * x, axis=-1, keepdims=True) + NORM_EPS) * g


def _sigmoid(x):
    return 1.0 / (1.0 + jnp.exp(-x))


def _softplus(x):
    return jnp.maximum(x, 0.0) + jnp.log(1.0 + jnp.exp(-jnp.abs(x)))


def _dot(a, b, precision=None):
    return jnp.dot(a, b, preferred_element_type=F32, precision=precision)


def _dot_nt(a, b, precision=None):
    return lax.dot_general(a, b, (((1,), (1,)), ((), ())), preferred_element_type=F32,
                           precision=precision)


def _norm_mm_body(x_ref, g_ref, w_ref, o_ref, xn_ref, *, precision):
    @pl.when(pl.program_id(1) == 0)
    def _():
        xn_ref[...] = _rms(x_ref[...].astype(F32), g_ref[...]).astype(xn_ref.dtype)

    o_ref[...] = _dot(xn_ref[...], w_ref[...], precision).astype(o_ref.dtype)


def _norm_matmul(x, g, w, *, out_dtype, tm, tn, precision=None):
    m, k = x.shape
    n = w.shape[1]
    return _pcall(
        functools.partial(_norm_mm_body, precision=precision),
        grid=(m // tm, n // tn),
        in_specs=[pl.BlockSpec((tm, k), lambda i, j: (i, 0)),
                  pl.BlockSpec((1, k), lambda i, j: (0, 0)),
                  pl.BlockSpec((k, tn), lambda i, j: (0, j))],
        out_specs=pl.BlockSpec((tm, tn), lambda i, j: (i, j)),
        out_shape=jax.ShapeDtypeStruct((m, n), out_dtype),
        scratch_shapes=[pltpu.VMEM((tm, k), w.dtype)],
        compiler_params=_cparams("parallel", "arbitrary"),
    )(x, g.reshape(1, k), w)


def _swiglu_body(x_ref, g_ref, wg_ref, wu_ref, o_ref, xn_ref):
    @pl.when(pl.program_id(1) == 0)
    def _():
        xn_ref[...] = _rms(x_ref[...], g_ref[...]).astype(BF16)

    xn = xn_ref[...]
    a = _dot(xn, wg_ref[...])
    b = _dot(xn, wu_ref[...])
    o_ref[...] = (a * _sigmoid(a) * b).astype(o_ref.dtype)


def _norm_swiglu(x, g, w_gu, *, tm, tn):
    m, k = x.shape
    f = w_gu.shape[1] // 2
    nj = f // tn
    return _pcall(
        _swiglu_body,
        grid=(m // tm, nj),
        in_specs=[pl.BlockSpec((tm, k), lambda i, j: (i, 0)),
                  pl.BlockSpec((1, k), lambda i, j: (0, 0)),
                  pl.BlockSpec((k, tn), lambda i, j: (0, j)),
                  pl.BlockSpec((k, tn), lambda i, j: (0, j + nj))],
        out_specs=pl.BlockSpec((tm, tn), lambda i, j: (i, j)),
        out_shape=jax.ShapeDtypeStruct((m, f), BF16),
        scratch_shapes=[pltpu.VMEM((tm, k), BF16)],
        compiler_params=_cparams("parallel", "arbitrary"),
    )(x, g.reshape(1, k), w_gu, w_gu)


def _row_copies(pos_ref, n_rows, make_copy):
    def issue(r, c):
        for k in range(TOP_K):
            make_copy(r, k, pos_ref[0, TOP_K * r + k]).start()
        return c

    lax.fori_loop(0, n_rows, issue, 0)

    def drain(r, c):
        make_copy(0, 0, 0).wait()
        return c

    lax.fori_loop(0, TOP_K * n_rows, drain, 0)


def _moe_scatter_body(pos_ref, x_hbm, xg_init_hbm, xg_hbm, sem, *, tm):
    del xg_init_hbm
    base = pl.program_id(0) * tm
    _row_copies(pos_ref, tm, lambda r, k, p: pltpu.make_async_copy(
        x_hbm.at[pl.ds(base + r, 1)], xg_hbm.at[pl.ds(p, 1)], sem))


def _moe_scatter(x2, pos3, n_rows, *, tm):
    t_tok, d = x2.shape
    return _pcall(
        functools.partial(_moe_scatter_body, tm=tm),
        grid=(t_tok // tm,),
        in_specs=[pl.BlockSpec((None, 1, TOP_K * tm), lambda i: (i, 0, 0), memory_space=pltpu.SMEM),
                  pl.BlockSpec(memory_space=pl.ANY),
                  pl.BlockSpec(memory_space=pl.ANY)],
        out_specs=pl.BlockSpec(memory_space=pl.ANY),
        out_shape=jax.ShapeDtypeStruct((n_rows, d), x2.dtype),
        scratch_shapes=[pltpu.SemaphoreType.DMA(())],
        input_output_aliases={2: 0},
        compiler_params=_cparams("arbitrary"),
    )(pos3, x2, jnp.zeros((n_rows, d), x2.dtype))


def _moe_up_body(te_ref, nu_ref, xg_ref, g_ref, wg_ref, wu_ref, o_ref):
    del te_ref
    used = pl.program_id(1) < nu_ref[0]

    @pl.when(used)
    def _():
        xn = _rms(xg_ref[...], g_ref[...]).astype(BF16)
        a = _dot(xn, wg_ref[...])
        b = _dot(xn, wu_ref[...])
        o_ref[...] = (a * _sigmoid(a) * b).astype(o_ref.dtype)

    @pl.when(jnp.logical_not(used))
    def _():
        o_ref[...] = jnp.zeros_like(o_ref)


def _moe_up(xg, g, w_gu, tile_expert, n_used, *, tm, tn):
    p_rows, d = xg.shape
    f = w_gu.shape[2] // 2
    nj = f // tn
    grid_spec = pltpu.PrefetchScalarGridSpec(
        num_scalar_prefetch=2, grid=(nj, p_rows // tm),
        in_specs=[pl.BlockSpec((tm, d), lambda j, i, te, nu: (i, 0)),
                  pl.BlockSpec((1, d), lambda j, i, te, nu: (0, 0)),
                  pl.BlockSpec((None, d, tn), lambda j, i, te, nu: (te[i], 0, j)),
                  pl.BlockSpec((None, d, tn), lambda j, i, te, nu: (te[i], 0, j + nj))],
        out_specs=pl.BlockSpec((tm, tn), lambda j, i, te, nu: (i, j)))
    return _pcall(
        _moe_up_body, grid_spec=grid_spec,
        out_shape=jax.ShapeDtypeStruct((p_rows, f), BF16),
        compiler_params=_cparams("arbitrary", "arbitrary"),
    )(tile_expert, n_used, xg, g.reshape(1, d), w_gu, w_gu)


def _moe_down_body(te_ref, nu_ref, h_ref, w_ref, o_ref):
    del te_ref
    used = pl.program_id(0) < nu_ref[0]

    @pl.when(used)
    def _():
        o_ref[...] = _dot(h_ref[...], w_ref[...])

    @pl.when(jnp.logical_not(used))
    def _():
        o_ref[...] = jnp.zeros_like(o_ref)


def _moe_down(hmid, w_down, tile_expert, n_used, *, tm):
    p_rows, f = hmid.shape
    d = w_down.shape[2]
    grid_spec = pltpu.PrefetchScalarGridSpec(
        num_scalar_prefetch=2, grid=(p_rows // tm,),
        in_specs=[pl.BlockSpec((tm, f), lambda i, te, nu: (i, 0)),
                  pl.BlockSpec((None, f, d), lambda i, te, nu: (te[i], 0, 0))],
        out_specs=pl.BlockSpec((tm, d), lambda i, te, nu: (i, 0)))
    return _pcall(
        _moe_down_body, grid_spec=grid_spec,
        out_shape=jax.ShapeDtypeStruct((p_rows, d), F32),
        compiler_params=_cparams("arbitrary"),
    )(tile_expert, n_used, hmid, w_down)


def _moe_combine_body(pos_ref, rt_ref, x_ref, y_hbm, o_ref, buf, sem, *, tm):
    _row_copies(pos_ref, tm, lambda r, k, p: pltpu.make_async_copy(
        y_hbm.at[pl.ds(p, 1)], buf.at[k, pl.ds(r, 1)], sem))
    w = rt_ref[...]
    o_ref[...] = x_ref[...] + w[:, 0:1] * buf[0] + w[:, 1:2] * buf[1]


def _moe_combine(x2, rt, pos3, y, *, tm):
    t_tok, d = x2.shape
    return _pcall(
        functools.partial(_moe_combine_body, tm=tm),
        grid=(t_tok // tm,),
        in_specs=[pl.BlockSpec((None, 1, TOP_K * tm), lambda i: (i, 0, 0), memory_space=pltpu.SMEM),
                  pl.BlockSpec((tm, LANES), lambda i: (i, 0)),
                  pl.BlockSpec((tm, d), lambda i: (i, 0)),
                  pl.BlockSpec(memory_space=pl.ANY)],
        out_specs=pl.BlockSpec((tm, d), lambda i: (i, 0)),
        out_shape=jax.ShapeDtypeStruct((t_tok, d), F32),
        scratch_shapes=[pltpu.VMEM((TOP_K, tm, d), F32), pltpu.SemaphoreType.DMA(())],
        compiler_params=_cparams("arbitrary"),
    )(pos3, rt, x2, y)


def _moe_plan(rt, n_experts, tm):
    t_tok = rt.shape[0]
    e_flat = rt[:, 2:2 + TOP_K].astype(jnp.int32).reshape(-1)
    onehot = (e_flat[:, None] == jnp.arange(n_experts, dtype=jnp.int32)[None, :]).astype(jnp.int32)
    csum = jnp.cumsum(onehot, axis=0)
    padded = -(-csum[-1] // tm) * tm
    gend = jnp.cumsum(padded)
    pos = jnp.sum(onehot * (csum - 1 + (gend - padded)[None, :]), axis=1)
    n_tiles = (TOP_K * t_tok) // tm + n_experts
    tile_start = jnp.arange(n_tiles, dtype=jnp.int32) * tm
    tile_expert = jnp.minimum(jnp.sum((tile_start[:, None] >= gend[None, :]).astype(jnp.int32), axis=1),
                              n_experts - 1)
    return pos, tile_expert, (gend[-1:] // tm).astype(jnp.int32), n_tiles * tm


def _mm_res_body(a_ref, w_ref, r_ref, o_ref, acc_ref, *, nk):
    kk = pl.program_id(2)

    @pl.when(kk == 0)
    def _():
        acc_ref[...] = jnp.zeros_like(acc_ref)

    acc_ref[...] += _dot(a_ref[...], w_ref[...])

    @pl.when(kk == nk - 1)
    def _():
        o_ref[...] = r_ref[...] + acc_ref[...]


def _matmul_residual(a, w, r, *, tm, tn, tk):
    m, k = a.shape
    n = w.shape[1]
    nk = k // tk
    return _pcall(
        functools.partial(_mm_res_body, nk=nk),
        grid=(m // tm, n // tn, nk),
        in_specs=[pl.BlockSpec((tm, tk), lambda i, j, kk: (i, kk)),
                  pl.BlockSpec((tk, tn), lambda i, j, kk: (kk, j)),
                  pl.BlockSpec((tm, tn), lambda i, j, kk: (i, j))],
        out_specs=pl.BlockSpec((tm, tn), lambda i, j, kk: (i, j)),
        out_shape=jax.ShapeDtypeStruct((m, n), F32),
        scratch_shapes=[pltpu.VMEM((tm, tn), F32)],
        compiler_params=_cparams("parallel", "parallel", "arbitrary"),
    )(a, w, r)


def _causal_conv(xbuf, x, cw, ts):
    kk = cw.shape[0]
    xbuf[8:8 + ts, :] = x
    y = cw[kk - 1:kk, :] * x
    for d in range(1, kk):
        y = y + cw[kk - 1 - d:kk - d, :] * xbuf[8 - d:8 - d + ts, :]
    xbuf[0:8, :] = xbuf[ts:ts + 8, :]
    return y


def _rglru_body(x_ref, gate_ref, cw_ref, cb_ref, wa_ref, ba_ref, wx_ref, bx_ref, lam_ref, o_ref,
                xbuf, hcar, *, ts, width):
    @pl.when(pl.program_id(1) == 0)
    def _():
        xbuf[0:8, :] = jnp.zeros((8, width), F32)
        hcar[...] = jnp.zeros_like(hcar)

    u = _causal_conv(xbuf, x_ref[...].astype(F32), cw_ref[...], ts) + cb_ref[...]
    ub = u.astype(BF16)
    rs, is_ = [], []
    for g in range(width // HEAD_DIM):
        blk = ub[:, g * HEAD_DIM:(g + 1) * HEAD_DIM]
        rs.append(_dot(blk, wa_ref[g]))
        is_.append(_dot(blk, wx_ref[g]))
    r = _sigmoid(jnp.concatenate(rs, axis=-1) + ba_ref[...])
    ig = _sigmoid(jnp.concatenate(is_, axis=-1) + bx_ref[...])
    log_a = (-RG_C) * r * _softplus(-lam_ref[...])
    a = jnp.exp(log_a)
    b = jnp.sqrt(1.0 - jnp.exp(2.0 * log_a)) * (ig * u)
    rows = lax.broadcasted_iota(jnp.int32, (ts, 1), 0)
    d = 1
    while d < ts:
        keep = rows >= d
        b = jnp.where(keep, a * pltpu.roll(b, d, 0) + b, b)
        a = jnp.where(keep, a * pltpu.roll(a, d, 0), a)
        d *= 2
    h = a * hcar[0:1, :] + b
    hcar[0:1, :] = h[ts - 1:ts, :]
    gt = gate_ref[...].astype(F32)
    gelu = 0.5 * gt * (1.0 + jnp.tanh(math.sqrt(2.0 / math.pi) * (gt + 0.044715 * (gt * gt * gt))))
    o_ref[...] = (h * gelu).astype(o_ref.dtype)


def _rglru(proj, nb, seq, conv_w, conv_b, w_a, b_a, w_x, b_x, lam, *, ts):
    width = conv_w.shape[1]
    nt = seq // ts
    row = lambda v: v.reshape(1, width)
    full2 = lambda shp: pl.BlockSpec(shp, lambda b, t: (0,) * len(shp))
    return _pcall(
        functools.partial(_rglru_body, ts=ts, width=width),
        grid=(nb, nt),
        in_specs=[pl.BlockSpec((ts, width), lambda b, t: (b * nt + t, 0)),
                  pl.BlockSpec((ts, width), lambda b, t: (b * nt + t, 1)),
                  full2(conv_w.shape), full2((1, width)), full2(w_a.shape), full2((1, width)),
                  full2(w_x.shape), full2((1, width)), full2((1, width))],
        out_specs=pl.BlockSpec((ts, width), lambda b, t: (b * nt + t, 0)),
        out_shape=jax.ShapeDtypeStruct((nb * seq, width), BF16),
        scratch_shapes=[pltpu.VMEM((ts + 8, width), F32), pltpu.VMEM((8, width), F32)],
        compiler_params=_cparams("parallel", "arbitrary"),
    )(proj, proj, conv_w, row(conv_b), w_a.astype(BF16), row(b_a), w_x.astype(BF16), row(b_x), row(lam))


def _sb_body(q_ref, k_ref, v_ref, qg_ref, kg_ref, o_ref, kn_ref, *, tq, scale, hp):
    qi = pl.program_id(2)
    hs = range(hp)
    sl = lambda h: slice(h * HEAD_DIM, (h + 1) * HEAD_DIM)

    @pl.when(qi == 0)
    def _():
        for h in hs:
            kn_ref[:, sl(h)] = _rms(k_ref[:, sl(h)].astype(F32), kg_ref[...]).astype(BF16)

    qn = [(_rms(q_ref[:, sl(h)].astype(F32), qg_ref[...]) * scale).astype(BF16) for h in hs]
    rr = lax.broadcasted_iota(jnp.int32, (tq, tq), 0)
    cc = lax.broadcasted_iota(jnp.int32, (tq, tq), 1)
    upper = (rr > cc).astype(BF16)

    def block(j, accs, runs, diag):
        ks = pl.multiple_of(j * tq, tq)
        z = [_dot_nt(qn[h], kn_ref[pl.ds(ks, tq), sl(h)]) for h in hs]
        sp = [_softplus(z[h]) for h in hs]
        if diag:
            vis = cc < rr
            lk = [jnp.where(vis, -sp[h], 0.0) for h in hs]
        else:
            lk = [-sp[h] for h in hs]
        hi = [lk[h].astype(BF16) for h in hs]
        lo = [(lk[h] - hi[h].astype(F32)).astype(BF16) for h in hs]
        cs = [_dot(hi[h], upper) + _dot(lo[h], upper) for h in hs]
        w = [jnp.exp(z[h] - sp[h] + cs[h] + runs[h]) for h in hs]
        if diag:
            w = [jnp.where(vis, w[h], 0.0) for h in hs]
        accs = tuple(accs[h] + _dot(w[h].astype(BF16), v_ref[pl.ds(ks, tq), sl(h)]) for h in hs)
        runs = tuple(runs[h] + jnp.sum(lk[h], axis=-1, keepdims=True) for h in hs)
        return accs, runs

    accs, runs = block(qi, tuple(jnp.zeros((tq, HEAD_DIM), F32) for _ in hs),
                       tuple(jnp.zeros((tq, 1), F32) for _ in hs), True)

    def more(c):
        return (c[0] >= 0) & (jnp.max(functools.reduce(jnp.maximum, c[2])) > EXP_UNDERFLOW)

    def step(c):
        a, r = block(c[0], c[1], c[2], False)
        return c[0] - 1, a, r

    _, accs, _ = lax.while_loop(more, step, (qi - 1, accs, runs))
    for h in hs:
        o_ref[:, sl(h)] = accs[h].astype(o_ref.dtype)


def _sb_attention(proj3, q_gain, k_gain, *, heads, q_off, tq):
    nb, seq, _ = proj3.shape
    hp = _pick(heads, (4, 2, 1))
    wblk = hp * HEAD_DIM
    bw = heads * HEAD_DIM
    assert q_off % wblk == 0 and bw % wblk == 0
    qb, kb, vb = q_off // wblk, (q_off + bw) // wblk, (q_off + 2 * bw) // wblk
    return _pcall(
        functools.partial(_sb_body, tq=tq, scale=HEAD_DIM ** -0.5, hp=hp),
        grid=(nb, heads // hp, seq // tq),
        in_specs=[pl.BlockSpec((None, tq, wblk), lambda b, h, i: (b, i, qb + h)),
                  pl.BlockSpec((None, seq, wblk), lambda b, h, i: (b, 0, kb + h)),
                  pl.BlockSpec((None, seq, wblk), lambda b, h, i: (b, 0, vb + h)),
                  pl.BlockSpec((1, HEAD_DIM), lambda b, h, i: (0, 0)),
                  pl.BlockSpec((1, HEAD_DIM), lambda b, h, i: (0, 0))],
        out_specs=pl.BlockSpec((None, tq, wblk), lambda b, h, i: (b, i, h)),
        out_shape=jax.ShapeDtypeStruct((nb, seq, bw), BF16),
        scratch_shapes=[pltpu.VMEM((seq, wblk), BF16)],
        compiler_params=_cparams("parallel", "parallel", "arbitrary"),
    )(proj3, proj3, proj3, q_gain.reshape(1, HEAD_DIM), k_gain.reshape(1, HEAD_DIM))


def _rope_pad(r, gain, cos_t, sin_a, sin_b):
    ms = jnp.sum(r * r, axis=-1, keepdims=True) * (1.0 / MLA_ROPE)
    rn = r * lax.rsqrt(ms + NORM_EPS) * gain
    return rn * cos_t + pltpu.roll(rn, LANES - MLA_ROPE // 2, 1) * sin_a + pltpu.roll(rn, MLA_ROPE // 2, 1) * sin_b


def _mla_q_body(cq_ref, g_ref, w_ref, qg_ref, cos_ref, sa_ref, sb_ref, o_ref, xn_ref):
    @pl.when(pl.program_id(1) == 0)
    def _():
        xn_ref[...] = _rms(cq_ref[...].astype(F32), g_ref[...]).astype(BF16)

    q = _dot(xn_ref[...], w_ref[...])
    qg = qg_ref[...]
    o_ref[:, :HEAD_DIM] = _rms(q[:, :HEAD_DIM], qg[:, :HEAD_DIM]).astype(o_ref.dtype)
    o_ref[:, HEAD_DIM:] = _rope_pad(q[:, HEAD_DIM:], qg[:, HEAD_DIM:], cos_ref[...], sa_ref[...],
                                    sb_ref[...]).astype(o_ref.dtype)


def _mla_q(proj, cq_gain, w_q, q_gain2, tabs, *, heads, cq_blk, nb, seq, tm):
    rank = w_q.shape[0]
    nt = seq // tm
    tab = pl.BlockSpec((tm, LANES), lambda i, j: (i % nt, 0))
    return _pcall(
        _mla_q_body,
        grid=(nb * nt, heads),
        in_specs=[pl.BlockSpec((tm, rank), lambda i, j: (i, cq_blk)),
                  pl.BlockSpec((1, rank), lambda i, j: (0, 0)),
                  pl.BlockSpec((rank, 2 * HEAD_DIM), lambda i, j: (0, j)),
                  pl.BlockSpec((1, 2 * HEAD_DIM), lambda i, j: (0, 0)),
                  tab, tab, tab],
        out_specs=pl.BlockSpec((tm, 2 * HEAD_DIM), lambda i, j: (i, j)),
        out_shape=jax.ShapeDtypeStruct((nb * seq, heads * 2 * HEAD_DIM), BF16),
        scratch_shapes=[pltpu.VMEM((tm, rank), BF16)],
        compiler_params=_cparams("parallel", "arbitrary"),
    )(proj, cq_gain.reshape(1, rank), w_q, q_gain2, *tabs)


def _mla_kv_body(ckv_ref, g_ref, w_ref, kg_ref, kr_ref, krg_ref, cos_ref, sa_ref, sb_ref,
                 k_ref, v_ref, xn_ref, kro_ref):
    @pl.when(pl.program_id(1) == 0)
    def _():
        xn_ref[...] = _rms(ckv_ref[...].astype(F32), g_ref[...]).astype(BF16)
        kro_ref[...] = _rope_pad(kr_ref[...].astype(F32), krg_ref[...], cos_ref[...], sa_ref[...],
                                 sb_ref[...]).astype(kro_ref.dtype)

    kv = _dot(xn_ref[...], w_ref[...])
    k_ref[:, :HEAD_DIM] = _rms(kv[:, :HEAD_DIM], kg_ref[...]).astype(k_ref.dtype)
    k_ref[:, HEAD_DIM:] = kro_ref[...]
    v_ref[...] = kv[:, HEAD_DIM:].astype(v_ref.dtype)


def _mla_kv(proj, ckv_gain, w_kv, k_gain_n, k_gain_r, tabs, *, heads, ckv_blk, kr_blk, nb, seq, tm):
    rank = w_kv.shape[0]
    nt = seq // tm
    tab = pl.BlockSpec((tm, LANES), lambda i, j: (i % nt, 0))
    return _pcall(
        _mla_kv_body,
        grid=(nb * nt, heads),
        in_specs=[pl.BlockSpec((tm, rank), lambda i, j: (i, ckv_blk)),
                  pl.BlockSpec((1, rank), lambda i, j: (0, 0)),
                  pl.BlockSpec((rank, 2 * HEAD_DIM), lambda i, j: (0, j)),
                  pl.BlockSpec((1, HEAD_DIM), lambda i, j: (0, 0)),
                  pl.BlockSpec((tm, LANES), lambda i, j: (i, kr_blk)),
                  pl.BlockSpec((1, LANES), lambda i, j: (0, 0)),
                  tab, tab, tab],
        out_specs=[pl.BlockSpec((tm, 2 * HEAD_DIM), lambda i, j: (i, j)),
                   pl.BlockSpec((tm, HEAD_DIM), lambda i, j: (i, j))],
        out_shape=[jax.ShapeDtypeStruct((nb * seq, heads * 2 * HEAD_DIM), BF16),
                   jax.ShapeDtypeStruct((nb * seq, heads * HEAD_DIM), BF16)],
        scratch_shapes=[pltpu.VMEM((tm, rank), BF16), pltpu.VMEM((tm, LANES), BF16)],
        compiler_params=_cparams("parallel", "arbitrary"),
    )(proj, ckv_gain.reshape(1, rank), w_kv, k_gain_n, proj, k_gain_r, *tabs)


def _mla_attn_body(q_ref, k_ref, v_ref, o_ref, *, tq, tk, hp):
    qi = pl.program_id(2)
    hs = range(hp)
    qk = lambda h: slice(h * 2 * HEAD_DIM, (h + 1) * 2 * HEAD_DIM)
    vs = lambda h: slice(h * HEAD_DIM, (h + 1) * HEAD_DIM)
    q = [q_ref[:, qk(h)] for h in hs]
    rr = lax.broadcasted_iota(jnp.int32, (tq, tk), 0)
    cc = lax.broadcasted_iota(jnp.int32, (tq, tk), 1)
    nsub = tq // tk

    def block(j, carry, diag):
        m, l, acc = carry
        ks = pl.multiple_of(j * tk, tk)
        s = [_dot_nt(q[h], k_ref[pl.ds(ks, tk), qk(h)]) for h in hs]
        if diag:
            vis = cc + (j - qi * nsub) * tk <= rr
            s = [jnp.where(vis, s[h], NEG_INF) for h in hs]
        m_new = tuple(jnp.maximum(m[h], jnp.max(s[h], axis=-1, keepdims=True)) for h in hs)
        alpha = [jnp.exp(m[h] - m_new[h]) for h in hs]
        p = [jnp.exp(s[h] - m_new[h]) for h in hs]
        l = tuple(alpha[h] * l[h] + jnp.sum(p[h], axis=-1, keepdims=True) for h in hs)
        acc = tuple(alpha[h] * acc[h] + _dot(p[h].astype(BF16), v_ref[pl.ds(ks, tk), vs(h)]) for h in hs)
        return m_new, l, acc

    init = (tuple(jnp.full((tq, 1), NEG_INF, F32) for _ in hs),
            tuple(jnp.zeros((tq, 1), F32) for _ in hs),
            tuple(jnp.zeros((tq, HEAD_DIM), F32) for _ in hs))
    carry = lax.fori_loop(0, qi * nsub, lambda j, c: block(j, c, False), init)
    for d in range(nsub):
        carry = block(qi * nsub + d, carry, True)
    _, l, acc = carry
    for h in hs:
        o_ref[:, vs(h)] = (acc[h] / l[h]).astype(o_ref.dtype)


def _mla_attention(q, k, v, *, heads, tq, tk):
    nb, seq, _ = q.shape
    hp = _pick(heads, (2, 1))
    return _pcall(
        functools.partial(_mla_attn_body, tq=tq, tk=tk, hp=hp),
        grid=(nb, heads // hp, seq // tq),
        in_specs=[pl.BlockSpec((None, tq, hp * 2 * HEAD_DIM), lambda b, h, i: (b, i, h)),
                  pl.BlockSpec((None, seq, hp * 2 * HEAD_DIM), lambda b, h, i: (b, 0, h)),
                  pl.BlockSpec((None, seq, hp * HEAD_DIM), lambda b, h, i: (b, 0, h))],
        out_specs=pl.BlockSpec((None, tq, hp * HEAD_DIM), lambda b, h, i: (b, i, h)),
        out_shape=jax.ShapeDtypeStruct((nb, seq, heads * HEAD_DIM), BF16),
        compiler_params=_cparams("parallel", "parallel", "arbitrary"),
    )(q, k, v)


def _dn_prep_body(x_ref, bd_ref, cw_ref, alog_ref, dtb_ref, o_ref, gb_ref, xbuf, *, ts, heads):
    width = 3 * heads * HEAD_DIM

    @pl.when(pl.program_id(1) == 0)
    def _():
        xbuf[0:8, :] = jnp.zeros((8, width), F32)

    y = _causal_conv(xbuf, x_ref[...].astype(F32), cw_ref[...], ts)
    y = y * _sigmoid(y)
    qscale = HEAD_DIM ** -0.5
    for hh in range(3 * heads):
        blk = y[:, hh * HEAD_DIM:(hh + 1) * HEAD_DIM]
        if hh < 2 * heads:
            blk = blk * lax.rsqrt(jnp.sum(blk * blk, axis=-1, keepdims=True) + NORM_EPS)
            if hh < heads:
                blk = blk * qscale
        o_ref[:, hh * HEAD_DIM:(hh + 1) * HEAD_DIM] = blk
    bd = bd_ref[...]
    beta = _sigmoid(bd[:, :LANES])
    g = -jnp.exp(alog_ref[...]) * _softplus(bd[:, LANES:] + dtb_ref[...])
    rows = lax.broadcasted_iota(jnp.int32, (ts, 1), 0) % DN_CHUNK
    d = 1
    while d < DN_CHUNK:
        g = g + jnp.where(rows >= d, pltpu.roll(g, d, 0), 0.0)
        d *= 2
    gb_ref[:, :LANES] = g
    gb_ref[:, LANES:] = beta


def _dn_prep(proj, bd, conv_w, a_log, dt_bias, *, heads, qkv_blk, nb, seq, ts):
    width = 3 * heads * HEAD_DIM
    nt = seq // ts
    pad = lambda v: jnp.zeros((1, LANES), F32).at[0, :heads].set(v)
    return _pcall(
        functools.partial(_dn_prep_body, ts=ts, heads=heads),
        grid=(nb, nt),
        in_specs=[pl.BlockSpec((ts, width), lambda b, t: (b * nt + t, qkv_blk)),
                  pl.BlockSpec((ts, 2 * LANES), lambda b, t: (b * nt + t, 0)),
                  pl.BlockSpec(conv_w.shape, lambda b, t: (0, 0)),
                  pl.BlockSpec((1, LANES), lambda b, t: (0, 0)),
                  pl.BlockSpec((1, LANES), lambda b, t: (0, 0))],
        out_specs=[pl.BlockSpec((ts, width), lambda b, t: (b * nt + t, 0)),
                   pl.BlockSpec((ts, 2 * LANES), lambda b, t: (b * nt + t, 0))],
        out_shape=[jax.ShapeDtypeStruct((nb * seq, width), F32),
                   jax.ShapeDtypeStruct((nb * seq, 2 * LANES), F32)],
        scratch_shapes=[pltpu.VMEM((ts + 8, width), F32)],
        compiler_params=_cparams("parallel", "arbitrary"),
    )(proj, bd, conv_w, pad(a_log), pad(dt_bias))


def _bdot(a, b):
    return _dot(a.astype(BF16), b.astype(BF16))


def _dn_body(qkv_ref, gb_ref, z_ref, og_ref, o_ref, state, *, heads):
    c = DN_CHUNK
    hd = HEAD_DIM

    @pl.when(pl.program_id(1) == 0)
    def _():
        state[...] = jnp.zeros_like(state)

    rr = lax.broadcasted_iota(jnp.int32, (c, c), 0)
    cc = lax.broadcasted_iota(jnp.int32, (c, c), 1)
    lower = rr >= cc
    strict = rr > cc
    eye = (rr == cc).astype(F32)
    same = lambda sh: jnp.right_shift(rr, sh) == jnp.right_shift(cc, sh)
    base_sh = 4
    in_base = strict & same(base_sh)
    off_levels = [strict & same(sh + 1) & jnp.logical_not(same(sh))
                  for sh in range(base_sh, int(math.log2(c)))]
    gb = gb_ref[...]
    hs = range(heads)
    q = [qkv_ref[:, h * hd:(h + 1) * hd] for h in hs]
    k = [qkv_ref[:, (heads + h) * hd:(heads + h + 1) * hd] for h in hs]
    v = [qkv_ref[:, (2 * heads + h) * hd:(2 * heads + h + 1) * hd] for h in hs]
    gcol = [gb[:, h:h + 1] for h in hs]
    bcol = [gb[:, LANES + h:LANES + h + 1] for h in hs]
    gcb = [jnp.broadcast_to(gcol[h], (c, c)) for h in hs]
    dec = [jnp.where(lower, jnp.exp(jnp.where(lower, gcb[h] - gcb[h].T, 0.0)), 0.0) for h in hs]
    kb = [k[h] * bcol[h] for h in hs]
    kq = [_dot_nt(jnp.concatenate([kb[h], q[h]], axis=0).astype(BF16), k[h].astype(BF16)) for h in hs]
    a_mat = [jnp.where(strict, kq[h][:c] * dec[h], 0.0) for h in hs]
    intra = [kq[h][c:] * dec[h] for h in hs]
    pw = [-jnp.where(in_base, a_mat[h], 0.0) for h in hs]
    p_mat = [eye + pw[h] for h in hs]
    for _ in range(base_sh - 1):
        pw = [_bdot(pw[h], pw[h]) for h in hs]
        p_mat = [p_mat[h] + _bdot(p_mat[h], pw[h]) for h in hs]
    for off in off_levels:
        pc = [_bdot(p_mat[h], jnp.where(off, a_mat[h], 0.0)) for h in hs]
        p_mat = [p_mat[h] - _bdot(pc[h], p_mat[h]) for h in hs]
    eg = [jnp.exp(gcol[h]) for h in hs]
    uw = [_bdot(p_mat[h], jnp.concatenate([v[h] * bcol[h], kb[h] * eg[h]], axis=1)) for h in hs]
    glast = [gcol[h][c - 1:c, :] for h in hs]
    s_old = [state[h] for h in hs]
    ws = [_bdot(jnp.concatenate([uw[h][:, hd:], q[h] * eg[h]], axis=0), s_old[h]) for h in hs]
    v_new = [uw[h][:, :hd] - ws[h][:c] for h in hs]
    o = [ws[h][c:] + _bdot(intra[h], v_new[h]) for h in hs]
    k_dec = [k[h] * jnp.exp(glast[h] - gcol[h]) for h in hs]
    for h in hs:
        state[h] = s_old[h] * jnp.exp(glast[h]) + _bdot(k_dec[h].T, v_new[h])
    for h in hs:
        zz = z_ref[:, h * hd:(h + 1) * hd].astype(F32)
        o_ref[:, h * hd:(h + 1) * hd] = (_rms(o[h], og_ref[...]) * (zz * _sigmoid(zz))).astype(o_ref.dtype)


def _delta_rule(qkv, gb, proj, out_gain, *, heads, z_blk, nb, seq):
    c = DN_CHUNK
    nt = seq // c
    hw = heads * HEAD_DIM
    return _pcall(
        functools.partial(_dn_body, heads=heads),
        grid=(nb, nt),
        in_specs=[pl.BlockSpec((c, 3 * hw), lambda b, t: (b * nt + t, 0)),
                  pl.BlockSpec((c, 2 * LANES), lambda b, t: (b * nt + t, 0)),
                  pl.BlockSpec((c, hw), lambda b, t: (b * nt + t, z_blk)),
                  pl.BlockSpec((1, HEAD_DIM), lambda b, t: (0, 0))],
        out_specs=pl.BlockSpec((c, hw), lambda b, t: (b * nt + t, 0)),
        out_shape=jax.ShapeDtypeStruct((nb * seq, hw), BF16),
        scratch_shapes=[pltpu.VMEM((heads, HEAD_DIM, HEAD_DIM), F32)],
        compiler_params=_cparams("parallel", "arbitrary"),
    )(qkv, gb, proj, out_gain.reshape(1, HEAD_DIM))


def _merge_body(ya_ref, yb_ref, yc_ref, yd_ref, g0_ref, g1_ref, g2_ref, g3_ref, wb_ref, o_ref):
    acc = None
    for i, (y_ref, gl_ref) in enumerate(((ya_ref, g0_ref), (yb_ref, g1_ref), (yc_ref, g2_ref),
                                         (yd_ref, g3_ref))):
        t = _sigmoid(gl_ref[...].astype(F32)) * _dot(y_ref[...], wb_ref[i])
        acc = t if acc is None else acc + t
    o_ref[...] = acc.astype(o_ref.dtype)


def _gated_merge(ys, proj, w_branch, *, gate_off, tm, tn):
    m, bw = ys[0].shape
    nbr, _, n = w_branch.shape
    yspec = pl.BlockSpec((tm, bw), lambda i, j: (i, 0))
    gspecs = [pl.BlockSpec((tm, tn), functools.partial(lambda i, j, o: (i, o + j), o=(gate_off + r * n) // tn))
              for r in range(nbr)]
    return _pcall(
        _merge_body,
        grid=(m // tm, n // tn),
        in_specs=[yspec] * nbr + gspecs + [pl.BlockSpec((nbr, bw, tn), lambda i, j: (0, 0, j))],
        out_specs=pl.BlockSpec((tm, tn), lambda i, j: (i, j)),
        out_shape=jax.ShapeDtypeStruct((m, n), BF16),
        compiler_params=_cparams("parallel", "arbitrary"),
    )(*ys, *([proj] * nbr), w_branch)


def _mem_kv_body(m_ref, g_ref, w_ref, kg_ref, k_ref, v_ref, *, heads):
    xn = _rms(m_ref[...], g_ref[...]).astype(BF16)
    kv = _dot(xn, w_ref[...])
    hw = heads * HEAD_DIM
    for h in range(heads):
        k_ref[:, h * HEAD_DIM:(h + 1) * HEAD_DIM] = _rms(kv[:, h * HEAD_DIM:(h + 1) * HEAD_DIM],
                                                         kg_ref[...]).astype(k_ref.dtype)
    v_ref[...] = kv[:, hw:].astype(v_ref.dtype)


def _mem_kv(mem2, g, wkv, k_gain, *, nb, mlen, heads):
    d = mem2.shape[1]
    hw = heads * HEAD_DIM
    out = pl.BlockSpec((mlen, hw), lambda b: (b, 0))
    shp = jax.ShapeDtypeStruct((nb * mlen, hw), BF16)
    return _pcall(
        functools.partial(_mem_kv_body, heads=heads),
        grid=(nb,),
        in_specs=[pl.BlockSpec((mlen, d), lambda b: (b, 0)),
                  pl.BlockSpec((1, d), lambda b: (0, 0)),
                  pl.BlockSpec((d, 2 * hw), lambda b: (0, 0)),
                  pl.BlockSpec((1, HEAD_DIM), lambda b: (0, 0))],
        out_specs=[out, out],
        out_shape=[shp, shp],
        compiler_params=_cparams("parallel"),
    )(mem2, g.reshape(1, d), wkv, k_gain.reshape(1, HEAD_DIM))


def _xattn_body(x_ref, g_ref, wq_ref, k_ref, v_ref, qg_ref, wo_ref, o_ref, *, heads):
    x = x_ref[...]
    q = _dot(_rms(x, g_ref[...]).astype(BF16), wq_ref[...])
    scale = HEAD_DIM ** -0.5
    outs = []
    for h in range(heads):
        sl = slice(h * HEAD_DIM, (h + 1) * HEAD_DIM)
        qh = (_rms(q[:, sl], qg_ref[...]) * scale).astype(BF16)
        s = _dot_nt(qh, k_ref[:, sl])
        p = jnp.exp(s - jnp.max(s, axis=-1, keepdims=True))
        l = jnp.sum(p, axis=-1, keepdims=True)
        outs.append(_dot(p.astype(BF16), v_ref[:, sl]) / l)
    o = jnp.concatenate(outs, axis=-1).astype(BF16)
    o_ref[...] = x + _dot(o, wo_ref[...])


def _cross_attention(x2, g, wq, k, v, q_gain, wo, *, nb, seq, mlen, heads, tm):
    d = x2.shape[1]
    hw = heads * HEAD_DIM
    nt = seq // tm
    xspec = pl.BlockSpec((tm, d), lambda b, t: (b * nt + t, 0))
    return _pcall(
        functools.partial(_xattn_body, heads=heads),
        grid=(nb, nt),
        in_specs=[xspec,
                  pl.BlockSpec((1, d), lambda b, t: (0, 0)),
                  pl.BlockSpec((d, hw), lambda b, t: (0, 0)),
                  pl.BlockSpec((mlen, hw), lambda b, t: (b, 0)),
                  pl.BlockSpec((mlen, hw), lambda b, t: (b, 0)),
                  pl.BlockSpec((1, HEAD_DIM), lambda b, t: (0, 0)),
                  pl.BlockSpec((hw, d), lambda b, t: (0, 0))],
        out_specs=xspec,
        out_shape=jax.ShapeDtypeStruct(x2.shape, F32),
        compiler_params=_cparams("parallel", "arbitrary"),
    )(x2, g.reshape(1, d), wq, k, v, q_gain.reshape(1, HEAD_DIM), wo)


def _router_body(x_ref, g_ref, w_ref, o_ref, *, n_experts):
    logits = _dot(_rms(x_ref[...], g_ref[...]), w_ref[...], HIGHEST)
    lane = lax.broadcasted_iota(jnp.int32, logits.shape, 1)
    lg = jnp.where(lane < n_experts, logits, -jnp.inf)
    m1 = jnp.max(lg, axis=-1, keepdims=True)
    i1 = jnp.min(jnp.where(lg == m1, lane, LANES), axis=-1, keepdims=True)
    lg2 = jnp.where(lane == i1, -jnp.inf, lg)
    m2 = jnp.max(lg2, axis=-1, keepdims=True)
    i2 = jnp.min(jnp.where(lg2 == m2, lane, LANES), axis=-1, keepdims=True)
    e2 = jnp.exp(m2 - m1)
    den = 1.0 + e2
    o_ref[...] = (jnp.where(lane == 0, 1.0 / den, 0.0) + jnp.where(lane == 1, e2 / den, 0.0)
                  + jnp.where(lane == 2, i1.astype(F32), 0.0) + jnp.where(lane == 3, i2.astype(F32), 0.0))


def _router(x2, g, w_router, *, tm):
    m, d = x2.shape
    ne = w_router.shape[1]
    wpad = jnp.zeros((d, LANES), F32).at[:, :ne].set(w_router)
    return _pcall(
        functools.partial(_router_body, n_experts=ne),
        grid=(m // tm,),
        in_specs=[pl.BlockSpec((tm, d), lambda i: (i, 0)),
                  pl.BlockSpec((1, d), lambda i: (0, 0)),
                  pl.BlockSpec((d, LANES), lambda i: (0, 0))],
        out_specs=pl.BlockSpec((tm, LANES), lambda i: (i, 0)),
        out_shape=jax.ShapeDtypeStruct((m, LANES), F32),
        compiler_params=_cparams("parallel"),
    )(x2, g.reshape(1, d), wpad)


def _rope_tables(seq):
    half = MLA_ROPE // 2
    inv_freq = ROPE_BASE ** (-jnp.arange(half, dtype=F32) / half)
    ang = jnp.arange(seq, dtype=jnp.int32).astype(F32)[:, None] * inv_freq[None, :]
    cos, sin = jnp.cos(ang), jnp.sin(ang)
    z = jnp.zeros((seq, half), F32)
    cos_t = jnp.concatenate([cos, cos, z, z], axis=-1)
    sin_a = jnp.concatenate([-sin, z, z, z], axis=-1)
    sin_b = jnp.concatenate([z, sin, z, z], axis=-1)
    return cos_t, sin_a, sin_b


def kernel(x, mem, mix_norm, w_in, rg_conv_w, rg_conv_b, rg_w_a, rg_b_a, rg_w_x, rg_b_x, rg_lambda, sb_q_gain, sb_k_gain, mla_cq_gain, mla_w_uq, mla_ckv_gain, mla_w_ukv, mla_q_gain, mla_k_gain, dn_conv_w, dn_a_log, dn_dt_bias, dn_out_gain, w_branch, w_out, xa_norm, mem_norm, xa_wq, xa_wkv, xa_q_gain, xa_k_gain, xa_wo, ffn_norm, ffn_w_gu, ffn_w_down, moe_router, moe_w_gu, moe_w_down):
    nb, seq, d = x.shape
    depth = w_in.shape[0]
    mlen = mem.shape[1]
    bw = rg_conv_w.shape[2]
    heads = bw // HEAD_DIM
    q_rank = mla_cq_gain.shape[1]
    kv_rank = mla_ckv_gain.shape[1]
    xa_heads = xa_wq.shape[2] // HEAD_DIM
    t_tok = nb * seq
    assert bw % HEAD_DIM == 0 and seq % 512 == 0 and d % 512 == 0
    assert mla_q_gain.shape[1] == HEAD_DIM + MLA_ROPE

    groups = (("rg", 2 * bw, bw), ("sb", 3 * bw, _pick(heads, (4, 2, 1)) * HEAD_DIM), ("cq", q_rank, q_rank),
              ("ckv", kv_rank, kv_rank), ("kr", MLA_ROPE, LANES), ("dnqkv", 3 * bw, 3 * bw),
              ("dnz", bw, bw), ("beta", heads, None), ("dt", heads, None), ("gate", 4 * d, d))
    src, off, pieces = {}, {}, []
    s_pos = p_pos = 0
    for name, width, align in groups:
        src[name] = (s_pos, s_pos + width)
        s_pos += width
        if align is None:
            continue
        start = -(-p_pos // align) * align
        pieces.append((start - p_pos, src[name]))
        off[name] = start
        p_pos = start + width
        if name == "kr":
            pieces.append((LANES - width, None))
            p_pos += LANES - width
    n_proj = -(-p_pos // 512) * 512
    pieces.append((n_proj - p_pos, None))
    assert s_pos == w_in.shape[2]

    x2 = x.reshape(t_tok, d)
    mem2 = mem.reshape(nb * mlen, d)
    tabs = _rope_tables(seq)
    mla_scale = (HEAD_DIM + MLA_ROPE) ** -0.5

    tm_big = _pick(t_tok, (1024, 512, 256, 128))
    tm_mid = _pick(t_tok, (512, 256, 128))
    ts_seq = _pick(seq, (512, 256, 128))

    for layer in range(depth):
        wi = w_in[layer]
        zc = lambda n: jnp.zeros((d, n), wi.dtype)
        cols = []
        for gap, rng in pieces:
            if gap:
                cols.append(zc(gap))
            if rng is not None:
                cols.append(wi[:, rng[0]:rng[1]])
        w_proj = jnp.concatenate(cols, axis=1).astype(BF16)
        w_bd = jnp.concatenate([wi[:, src["beta"][0]:src["beta"][1]], zc(LANES - heads),
                                wi[:, src["dt"][0]:src["dt"][1]], zc(LANES - heads)], axis=1)
        wq3 = mla_w_uq[layer].reshape(q_rank, heads, HEAD_DIM + MLA_ROPE)
        w_q = jnp.concatenate([wq3, jnp.zeros((q_rank, heads, HEAD_DIM - MLA_ROPE), F32)],
                              axis=-1).reshape(q_rank, heads * 2 * HEAD_DIM).astype(BF16)
        qg = mla_q_gain[layer] * mla_scale
        q_gain2 = jnp.concatenate([qg, jnp.zeros((HEAD_DIM - MLA_ROPE,), F32)]).reshape(1, 2 * HEAD_DIM)
        kgl = mla_k_gain[layer]
        k_gain_n = kgl[:HEAD_DIM].reshape(1, HEAD_DIM)
        k_gain_r = jnp.concatenate([kgl[HEAD_DIM:], jnp.zeros((LANES - MLA_ROPE,), F32)]).reshape(1, LANES)

        proj = _norm_matmul(x2, mix_norm[layer], w_proj, out_dtype=BF16, tm=tm_big,
                            tn=_pick(n_proj, (512, 256, 128)))
        bd = _norm_matmul(x2, mix_norm[layer], w_bd, out_dtype=F32, tm=tm_big, tn=2 * LANES,
                          precision=HIGHEST)
        proj3 = proj.reshape(nb, seq, n_proj)

        y_a = _rglru(proj, nb, seq, rg_conv_w[layer], rg_conv_b[layer], rg_w_a[layer], rg_b_a[layer],
                     rg_w_x[layer], rg_b_x[layer], rg_lambda[layer], ts=_pick(seq, (256, 128)))

        y_b = _sb_attention(proj3, sb_q_gain[layer], sb_k_gain[layer], heads=heads,
                            q_off=off["sb"], tq=_pick(seq, (256, 128)))

        mq = _mla_q(proj, mla_cq_gain[layer], w_q, q_gain2, tabs, heads=heads,
                    cq_blk=off["cq"] // q_rank, nb=nb, seq=seq, tm=ts_seq)
        mk, mv = _mla_kv(proj, mla_ckv_gain[layer], mla_w_ukv[layer].astype(BF16), k_gain_n,
                         k_gain_r, tabs, heads=heads, ckv_blk=off["ckv"] // kv_rank,
                         kr_blk=off["kr"] // LANES, nb=nb, seq=seq, tm=ts_seq)
        r3 = lambda a: a.reshape(nb, seq, a.shape[-1])
        y_c = _mla_attention(r3(mq), r3(mk), r3(mv), heads=heads, tq=ts_seq, tk=ts_seq)

        dn_qkv, dn_gb = _dn_prep(proj, bd, dn_conv_w[layer], dn_a_log[layer], dn_dt_bias[layer],
                                 heads=heads, qkv_blk=off["dnqkv"] // (3 * bw), nb=nb, seq=seq, ts=ts_seq)
        y_d = _delta_rule(dn_qkv, dn_gb, proj, dn_out_gain[layer], heads=heads, z_blk=off["dnz"] // bw,
                          nb=nb, seq=seq)

        merged = _gated_merge([y_a, y_b.reshape(t_tok, bw), y_c.reshape(t_tok, bw), y_d], proj,
                              w_branch[layer].astype(BF16), gate_off=off["gate"], tm=tm_mid,
                              tn=_pick(d, (512, 256, 128)))
        x2 = _matmul_residual(merged, w_out[layer].astype(BF16), x2, tm=tm_mid,
                              tn=_pick(d, (1024, 512, 256, 128)), tk=d)

        mk, mv = _mem_kv(mem2, mem_norm[layer], xa_wkv[layer].astype(BF16), xa_k_gain[layer],
                         nb=nb, mlen=mlen, heads=xa_heads)
        x2 = _cross_attention(x2, xa_norm[layer], xa_wq[layer].astype(BF16), mk, mv, xa_q_gain[layer],
                              xa_wo[layer].astype(BF16), nb=nb, seq=seq, mlen=mlen, heads=xa_heads,
                              tm=ts_seq)

        if layer % 2 == 0:
            wgu = ffn_w_gu[layer // 2].astype(BF16)
            f = wgu.shape[1] // 2
            hmid = _norm_swiglu(x2, ffn_norm[layer], wgu, tm=tm_mid, tn=_pick(f, (512, 256, 128)))
            x2 = _matmul_residual(hmid, ffn_w_down[layer // 2].astype(BF16), x2, tm=tm_mid,
                                  tn=_pick(d, (1024, 512, 256, 128)), tk=_pick(f, (1408, 512, 256, 128)))
        else:
            wgu = moe_w_gu[layer // 2].astype(BF16)
            ne, _, f2 = wgu.shape
            f = f2 // 2
            rt = _router(x2, ffn_norm[layer], moe_router[layer // 2], tm=tm_mid)
            pos, tile_expert, n_used, p_rows = _moe_plan(rt, ne, tm_mid)
            tm_c = _pick(t_tok, (256, 128))
            xg = _moe_scatter(x2, pos.reshape(t_tok // tm_mid, 1, TOP_K * tm_mid), p_rows, tm=tm_mid)
            hmid = _moe_up(xg, ffn_norm[layer], wgu, tile_expert, n_used, tm=tm_mid,
                           tn=_pick(f, (1408, 512, 256, 128)))
            yg = _moe_down(hmid, moe_w_down[layer // 2].astype(BF16), tile_expert, n_used, tm=tm_mid)
            x2 = _moe_combine(x2, rt, pos.reshape(t_tok // tm_c, 1, TOP_K * tm_c), yg, tm=tm_c)
    return x2.reshape(nb, seq, d)
```

```python
import functools
import math

import jax
import jax.numpy as jnp
import numpy as np
from jax import lax
from jax.experimental import pallas as pl
from jax.experimental.pallas import tpu as pltpu

F32 = jnp.float32
BF16 = jnp.bfloat16
HIGHEST = lax.Precision.HIGHEST

HEAD_DIM = 128
LANES = 128
MLA_ROPE = 64
RG_C = 8.0
ROPE_BASE = 10000.0
NORM_EPS = 1e-6
NEG_INF = -1e30
EXP_UNDERFLOW = -104.0
DN_CHUNK = 128
TOP_K = 2
VMEM_LIMIT = 56 * 1024 * 1024


def _cparams(*sem):
    return pltpu.CompilerParams(dimension_semantics=sem, vmem_limit_bytes=VMEM_LIMIT)


def _pcall(body, **kwargs):
    fn = getattr(body, "func", body)
    return pl.pallas_call(body, name=fn.__name__.strip("_").replace("_body", ""), **kwargs)


def _pick(n, prefs):
    for p in prefs:
        if p <= n and n % p == 0:
            return p
    return n


def _rms(x, g):
    return x * lax.rsqrt(jnp.mean(x * x, axis=-1, keepdims=True) + NORM_EPS) * g


def _sigmoid(x):
    return 1.0 / (1.0 + jnp.exp(-x))


def _softplus(x):
    return jnp.maximum(x, 0.0) + jnp.log(1.0 + jnp.exp(-jnp.abs(x)))


def _dot(a, b, precision=None):
    return jnp.dot(a, b, preferred_element_type=F32, precision=precision)


def _dot_nt(a, b, precision=None):
    return lax.dot_general(a, b, (((1,), (1,)), ((), ())), preferred_element_type=F32,
                           precision=precision)


def _norm_mm_body(x_ref, g_ref, w_ref, o_ref, xn_ref, *, precision):
    @pl.when(pl.program_id(1) == 0)
    def _():
        xn_ref[...] = _rms(x_ref[...].astype(F32), g_ref[...]).astype(xn_ref.dtype)

    o_ref[...] = _dot(xn_ref[...], w_ref[...], precision).astype(o_ref.dtype)


def _norm_matmul(x, g, w, *, out_dtype, tm, tn, precision=None):
    m, k = x.shape
    n = w.shape[1]
    return _pcall(
        functools.partial(_norm_mm_body, precision=precision),
        grid=(m // tm, n // tn),
        in_specs=[pl.BlockSpec((tm, k), lambda i, j: (i, 0)),
                  pl.BlockSpec((1, k), lambda i, j: (0, 0)),
                  pl.BlockSpec((k, tn), lambda i, j: (0, j))],
        out_specs=pl.BlockSpec((tm, tn), lambda i, j: (i, j)),
        out_shape=jax.ShapeDtypeStruct((m, n), out_dtype),
        scratch_shapes=[pltpu.VMEM((tm, k), w.dtype)],
        compiler_params=_cparams("parallel", "arbitrary"),
    )(x, g.reshape(1, k), w)


def _swiglu_body(x_ref, g_ref, wg_ref, wu_ref, o_ref, xn_ref):
    @pl.when(pl.program_id(1) == 0)
    def _():
        xn_ref[...] = _rms(x_ref[...], g_ref[...]).astype(BF16)

    xn = xn_ref[...]
    a = _dot(xn, wg_ref[...])
    b = _dot(xn, wu_ref[...])
    o_ref[...] = (a * _sigmoid(a) * b).astype(o_ref.dtype)


def _norm_swiglu(x, g, w_gu, *, tm, tn):
    m, k = x.shape
    f = w_gu.shape[1] // 2
    nj = f // tn
    return _pcall(
        _swiglu_body,
        grid=(m // tm, nj),
        in_specs=[pl.BlockSpec((tm, k), lambda i, j: (i, 0)),
                  pl.BlockSpec((1, k), lambda i, j: (0, 0)),
                  pl.BlockSpec((k, tn), lambda i, j: (0, j)),
                  pl.BlockSpec((k, tn), lambda i, j: (0, j + nj))],
        out_specs=pl.BlockSpec((tm, tn), lambda i, j: (i, j)),
        out_shape=jax.ShapeDtypeStruct((m, f), BF16),
        scratch_shapes=[pltpu.VMEM((tm, k), BF16)],
        compiler_params=_cparams("parallel", "arbitrary"),
    )(x, g.reshape(1, k), w_gu, w_gu)


def _row_copies(pos_ref, n_rows, make_copy):
    def issue(r, c):
        for k in range(TOP_K):
            make_copy(r, k, pos_ref[0, TOP_K * r + k]).start()
        return c

    lax.fori_loop(0, n_rows, issue, 0)

    def drain(r, c):
        make_copy(0, 0, 0).wait()
        return c

    lax.fori_loop(0, TOP_K * n_rows, drain, 0)


def _moe_scatter_body(pos_ref, x_ref, xg_init_hbm, xg_hbm, sem, *, tm):
    del xg_init_hbm
    _row_copies(pos_ref, tm, lambda r, k, p: pltpu.make_async_copy(
        x_ref.at[pl.ds(r, 1)], xg_hbm.at[pl.ds(p, 1)], sem))


def _moe_scatter(x2, pos3, n_rows, *, tm):
    t_tok, d = x2.shape
    return _pcall(
        functools.partial(_moe_scatter_body, tm=tm),
        grid=(t_tok // tm,),
        in_specs=[pl.BlockSpec((None, 1, TOP_K * tm), lambda i: (i, 0, 0), memory_space=pltpu.SMEM),
                  pl.BlockSpec((tm, d), lambda i: (i, 0)),
                  pl.BlockSpec(memory_space=pl.ANY)],
        out_specs=pl.BlockSpec(memory_space=pl.ANY),
        out_shape=jax.ShapeDtypeStruct((n_rows, d), x2.dtype),
        scratch_shapes=[pltpu.SemaphoreType.DMA(())],
        input_output_aliases={2: 0},
        compiler_params=_cparams("arbitrary"),
    )(pos3, x2, jnp.zeros((n_rows, d), x2.dtype))


def _moe_up_body(te_ref, nu_ref, xg_ref, g_ref, wg_ref, wu_ref, o_ref):
    del te_ref
    used = pl.program_id(1) < nu_ref[0]

    @pl.when(used)
    def _():
        xn = _rms(xg_ref[...], g_ref[...]).astype(BF16)
        a = _dot(xn, wg_ref[...])
        b = _dot(xn, wu_ref[...])
        o_ref[...] = (a * _sigmoid(a) * b).astype(o_ref.dtype)

    @pl.when(jnp.logical_not(used))
    def _():
        o_ref[...] = jnp.zeros_like(o_ref)


def _moe_up(xg, g, w_gu, tile_expert, n_used, *, tm, tn):
    p_rows, d = xg.shape
    f = w_gu.shape[2] // 2
    nj = f // tn
    grid_spec = pltpu.PrefetchScalarGridSpec(
        num_scalar_prefetch=2, grid=(nj, p_rows // tm),
        in_specs=[pl.BlockSpec((tm, d), lambda j, i, te, nu: (i, 0)),
                  pl.BlockSpec((1, d), lambda j, i, te, nu: (0, 0)),
                  pl.BlockSpec((None, d, tn), lambda j, i, te, nu: (te[i], 0, j)),
                  pl.BlockSpec((None, d, tn), lambda j, i, te, nu: (te[i], 0, j + nj))],
        out_specs=pl.BlockSpec((tm, tn), lambda j, i, te, nu: (i, j)))
    return _pcall(
        _moe_up_body, grid_spec=grid_spec,
        out_shape=jax.ShapeDtypeStruct((p_rows, f), BF16),
        compiler_params=_cparams("arbitrary", "arbitrary"),
    )(tile_expert, n_used, xg, g.reshape(1, d), w_gu, w_gu)


def _moe_down_body(te_ref, nu_ref, h_ref, w_ref, o_ref):
    del te_ref
    used = pl.program_id(0) < nu_ref[0]

    @pl.when(used)
    def _():
        o_ref[...] = _dot(h_ref[...], w_ref[...])

    @pl.when(jnp.logical_not(used))
    def _():
        o_ref[...] = jnp.zeros_like(o_ref)


def _moe_down(hmid, w_down, tile_expert, n_used, *, tm):
    p_rows, f = hmid.shape
    d = w_down.shape[2]
    grid_spec = pltpu.PrefetchScalarGridSpec(
        num_scalar_prefetch=2, grid=(p_rows // tm,),
        in_specs=[pl.BlockSpec((tm, f), lambda i, te, nu: (i, 0)),
                  pl.BlockSpec((None, f, d), lambda i, te, nu: (te[i], 0, 0))],
        out_specs=pl.BlockSpec((tm, d), lambda i, te, nu: (i, 0)))
    return _pcall(
        _moe_down_body, grid_spec=grid_spec,
        out_shape=jax.ShapeDtypeStruct((p_rows, d), F32),
        compiler_params=_cparams("arbitrary"),
    )(tile_expert, n_used, hmid, w_down)


def _moe_combine_body(pos_ref, rt_ref, x_ref, y_hbm, o_ref, buf, sem, *, tm):
    _row_copies(pos_ref, tm, lambda r, k, p: pltpu.make_async_copy(
        y_hbm.at[pl.ds(p, 1)], buf.at[k, pl.ds(r, 1)], sem))
    w = rt_ref[...]
    o_ref[...] = x_ref[...] + w[:, 0:1] * buf[0] + w[:, 1:2] * buf[1]


def _moe_combine(x2, rt, pos3, y, *, tm):
    t_tok, d = x2.shape
    return _pcall(
        functools.partial(_moe_combine_body, tm=tm),
        grid=(t_tok // tm,),
        in_specs=[pl.BlockSpec((None, 1, TOP_K * tm), lambda i: (i, 0, 0), memory_space=pltpu.SMEM),
                  pl.BlockSpec((tm, LANES), lambda i: (i, 0)),
                  pl.BlockSpec((tm, d), lambda i: (i, 0)),
                  pl.BlockSpec(memory_space=pl.ANY)],
        out_specs=pl.BlockSpec((tm, d), lambda i: (i, 0)),
        out_shape=jax.ShapeDtypeStruct((t_tok, d), F32),
        scratch_shapes=[pltpu.VMEM((TOP_K, tm, d), F32), pltpu.SemaphoreType.DMA(())],
        compiler_params=_cparams("arbitrary"),
    )(pos3, rt, x2, y)


def _moe_plan(rt, n_experts, tm):
    t_tok = rt.shape[0]
    e_flat = rt[:, 2:2 + TOP_K].astype(jnp.int32).reshape(-1)
    onehot = (e_flat[:, None] == jnp.arange(n_experts, dtype=jnp.int32)[None, :]).astype(jnp.int32)
    csum = jnp.cumsum(onehot, axis=0)
    padded = -(-csum[-1] // tm) * tm
    gend = jnp.cumsum(padded)
    pos = jnp.sum(onehot * (csum - 1 + (gend - padded)[None, :]), axis=1)
    n_tiles = (TOP_K * t_tok) // tm + n_experts
    tile_start = jnp.arange(n_tiles, dtype=jnp.int32) * tm
    tile_expert = jnp.minimum(jnp.sum((tile_start[:, None] >= gend[None, :]).astype(jnp.int32), axis=1),
                              n_experts - 1)
    return pos, tile_expert, (gend[-1:] // tm).astype(jnp.int32), n_tiles * tm


def _mm_res_body(a_ref, w_ref, r_ref, o_ref, *scratch, nk):
    if nk == 1:
        o_ref[...] = r_ref[...] + _dot(a_ref[...], w_ref[...])
        return
    acc_ref, = scratch
    kk = pl.program_id(2)

    @pl.when(kk == 0)
    def _():
        acc_ref[...] = jnp.zeros_like(acc_ref)

    acc_ref[...] += _dot(a_ref[...], w_ref[...])

    @pl.when(kk == nk - 1)
    def _():
        o_ref[...] = r_ref[...] + acc_ref[...]


def _matmul_residual(a, w, r, *, tm, tn, tk):
    m, k = a.shape
    n = w.shape[1]
    nk = k // tk
    return _pcall(
        functools.partial(_mm_res_body, nk=nk),
        grid=(m // tm, n // tn, nk),
        in_specs=[pl.BlockSpec((tm, tk), lambda i, j, kk: (i, kk)),
                  pl.BlockSpec((tk, tn), lambda i, j, kk: (kk, j)),
                  pl.BlockSpec((tm, tn), lambda i, j, kk: (i, j))],
        out_specs=pl.BlockSpec((tm, tn), lambda i, j, kk: (i, j)),
        out_shape=jax.ShapeDtypeStruct((m, n), F32),
        scratch_shapes=[pltpu.VMEM((tm, tn), F32)] if nk > 1 else [],
        compiler_params=_cparams("parallel", "parallel", "arbitrary"),
    )(a, w, r)


def _causal_conv(xbuf, x, cw, ts):
    kk = cw.shape[0]
    xbuf[8:8 + ts, :] = x
    y = cw[kk - 1:kk, :] * x
    for d in range(1, kk):
        y = y + cw[kk - 1 - d:kk - d, :] * xbuf[8 - d:8 - d + ts, :]
    xbuf[0:8, :] = xbuf[ts:ts + 8, :]
    return y


def _rglru_body(x_ref, gate_ref, cw_ref, cb_ref, wa_ref, ba_ref, wx_ref, bx_ref, lam_ref, o_ref,
                xbuf, hcar, *, ts, width):
    @pl.when(pl.program_id(1) == 0)
    def _():
        xbuf[0:8, :] = jnp.zeros((8, width), F32)
        hcar[...] = jnp.zeros_like(hcar)

    u = _causal_conv(xbuf, x_ref[...].astype(F32), cw_ref[...], ts) + cb_ref[...]
    ub = u.astype(BF16)
    rs, is_ = [], []
    for g in range(width // HEAD_DIM):
        blk = ub[:, g * HEAD_DIM:(g + 1) * HEAD_DIM]
        rs.append(_dot(blk, wa_ref[g]))
        is_.append(_dot(blk, wx_ref[g]))
    r = _sigmoid(jnp.concatenate(rs, axis=-1) + ba_ref[...])
    ig = _sigmoid(jnp.concatenate(is_, axis=-1) + bx_ref[...])
    log_a = (-RG_C) * r * _softplus(-lam_ref[...])
    a = jnp.exp(log_a)
    b = jnp.sqrt(1.0 - jnp.exp(2.0 * log_a)) * (ig * u)
    rows = lax.broadcasted_iota(jnp.int32, (ts, 1), 0)
    d = 1
    while d < ts:
        keep = rows >= d
        b = jnp.where(keep, a * pltpu.roll(b, d, 0) + b, b)
        a = jnp.where(keep, a * pltpu.roll(a, d, 0), a)
        d *= 2
    h = a * hcar[0:1, :] + b
    hcar[0:1, :] = h[ts - 1:ts, :]
    gt = gate_ref[...].astype(F32)
    gelu = 0.5 * gt * (1.0 + jnp.tanh(math.sqrt(2.0 / math.pi) * (gt + 0.044715 * (gt * gt * gt))))
    o_ref[...] = (h * gelu).astype(o_ref.dtype)


def _rglru(proj, nb, seq, conv_w, conv_b, w_a, b_a, w_x, b_x, lam, *, ts):
    width = conv_w.shape[1]
    nt = seq // ts
    row = lambda v: v.reshape(1, width)
    full2 = lambda shp: pl.BlockSpec(shp, lambda b, t: (0,) * len(shp))
    return _pcall(
        functools.partial(_rglru_body, ts=ts, width=width),
        grid=(nb, nt),
        in_specs=[pl.BlockSpec((ts, width), lambda b, t: (b * nt + t, 0)),
                  pl.BlockSpec((ts, width), lambda b, t: (b * nt + t, 1)),
                  full2(conv_w.shape), full2((1, width)), full2(w_a.shape), full2((1, width)),
                  full2(w_x.shape), full2((1, width)), full2((1, width))],
        out_specs=pl.BlockSpec((ts, width), lambda b, t: (b * nt + t, 0)),
        out_shape=jax.ShapeDtypeStruct((nb * seq, width), BF16),
        scratch_shapes=[pltpu.VMEM((ts + 8, width), F32), pltpu.VMEM((8, width), F32)],
        compiler_params=_cparams("parallel", "arbitrary"),
    )(proj, proj, conv_w, row(conv_b), w_a.astype(BF16), row(b_a), w_x.astype(BF16), row(b_x), row(lam))


def _sb_body(q_ref, k_ref, v_ref, qg_ref, kg_ref, o_ref, kn_ref, *, tq, scale, hp):
    qi = pl.program_id(2)
    hs = range(hp)
    sl = lambda h: slice(h * HEAD_DIM, (h + 1) * HEAD_DIM)

    @pl.when(qi == 0)
    def _():
        for h in hs:
            kn_ref[:, sl(h)] = _rms(k_ref[:, sl(h)].astype(F32), kg_ref[...]).astype(BF16)

    qn = [(_rms(q_ref[:, sl(h)].astype(F32), qg_ref[...]) * scale).astype(BF16) for h in hs]
    rr = lax.broadcasted_iota(jnp.int32, (tq, tq), 0)
    cc = lax.broadcasted_iota(jnp.int32, (tq, tq), 1)
    upper = (rr > cc).astype(BF16)

    def block(j, accs, runs, diag):
        ks = pl.multiple_of(j * tq, tq)
        z = [_dot_nt(qn[h], kn_ref[pl.ds(ks, tq), sl(h)]) for h in hs]
        sp = [_softplus(z[h]) for h in hs]
        if diag:
            vis = cc < rr
            lk = [jnp.where(vis, -sp[h], 0.0) for h in hs]
        else:
            lk = [-sp[h] for h in hs]
        hi = [lk[h].astype(BF16) for h in hs]
        lo = [(lk[h] - hi[h].astype(F32)).astype(BF16) for h in hs]
        cs = [_dot(hi[h], upper) + _dot(lo[h], upper) for h in hs]
        w = [jnp.exp(z[h] - sp[h] + cs[h] + runs[h]) for h in hs]
        if diag:
            w = [jnp.where(vis, w[h], 0.0) for h in hs]
        accs = tuple(accs[h] + _dot(w[h].astype(BF16), v_ref[pl.ds(ks, tq), sl(h)]) for h in hs)
        runs = tuple(runs[h] + jnp.sum(lk[h], axis=-1, keepdims=True) for h in hs)
        return accs, runs

    accs, runs = block(qi, tuple(jnp.zeros((tq, HEAD_DIM), F32) for _ in hs),
                       tuple(jnp.zeros((tq, 1), F32) for _ in hs), True)

    def more(c):
        return (c[0] >= 0) & (jnp.max(functools.reduce(jnp.maximum, c[2])) > EXP_UNDERFLOW)

    def step(c):
        a, r = block(c[0], c[1], c[2], False)
        return c[0] - 1, a, r

    _, accs, _ = lax.while_loop(more, step, (qi - 1, accs, runs))
    for h in hs:
        o_ref[:, sl(h)] = accs[h].astype(o_ref.dtype)


def _sb_attention(proj3, q_gain, k_gain, *, heads, q_off, tq):
    nb, seq, _ = proj3.shape
    hp = _pick(heads, (4, 2, 1))
    wblk = hp * HEAD_DIM
    bw = heads * HEAD_DIM
    assert q_off % wblk == 0 and bw % wblk == 0
    qb, kb, vb = q_off // wblk, (q_off + bw) // wblk, (q_off + 2 * bw) // wblk
    return _pcall(
        functools.partial(_sb_body, tq=tq, scale=HEAD_DIM ** -0.5, hp=hp),
        grid=(nb, heads // hp, seq // tq),
        in_specs=[pl.BlockSpec((None, tq, wblk), lambda b, h, i: (b, i, qb + h)),
                  pl.BlockSpec((None, seq, wblk), lambda b, h, i: (b, 0, kb + h)),
                  pl.BlockSpec((None, seq, wblk), lambda b, h, i: (b, 0, vb + h)),
                  pl.BlockSpec((1, HEAD_DIM), lambda b, h, i: (0, 0)),
                  pl.BlockSpec((1, HEAD_DIM), lambda b, h, i: (0, 0))],
        out_specs=pl.BlockSpec((None, tq, wblk), lambda b, h, i: (b, i, h)),
        out_shape=jax.ShapeDtypeStruct((nb, seq, bw), BF16),
        scratch_shapes=[pltpu.VMEM((seq, wblk), BF16)],
        compiler_params=_cparams("parallel", "parallel", "arbitrary"),
    )(proj3, proj3, proj3, q_gain.reshape(1, HEAD_DIM), k_gain.reshape(1, HEAD_DIM))


def _rope_pad(r, gain, cos_t, sin_a, sin_b):
    ms = jnp.sum(r * r, axis=-1, keepdims=True) * (1.0 / MLA_ROPE)
    rn = r * lax.rsqrt(ms + NORM_EPS) * gain
    return rn * cos_t + pltpu.roll(rn, LANES - MLA_ROPE // 2, 1) * sin_a + pltpu.roll(rn, MLA_ROPE // 2, 1) * sin_b


def _mla_q_body(cq_ref, g_ref, w_ref, qg_ref, cos_ref, sa_ref, sb_ref, o_ref, *, heads):
    q = _dot(_rms(cq_ref[...].astype(F32), g_ref[...]).astype(BF16), w_ref[...])
    qg = qg_ref[...]
    for h in range(heads):
        c0 = h * 2 * HEAD_DIM
        o_ref[:, c0:c0 + HEAD_DIM] = _rms(q[:, c0:c0 + HEAD_DIM], qg[:, :HEAD_DIM]).astype(o_ref.dtype)
        o_ref[:, c0 + HEAD_DIM:c0 + 2 * HEAD_DIM] = _rope_pad(
            q[:, c0 + HEAD_DIM:c0 + 2 * HEAD_DIM], qg[:, HEAD_DIM:], cos_ref[...], sa_ref[...],
            sb_ref[...]).astype(o_ref.dtype)


def _mla_q(proj, cq_gain, w_q, q_gain2, tabs, *, heads, cq_blk, nb, seq, tm):
    rank = w_q.shape[0]
    nt = seq // tm
    wide = heads * 2 * HEAD_DIM
    tab = pl.BlockSpec((tm, LANES), lambda i: (i % nt, 0))
    return _pcall(
        functools.partial(_mla_q_body, heads=heads),
        grid=(nb * nt,),
        in_specs=[pl.BlockSpec((tm, rank), lambda i: (i, cq_blk)),
                  pl.BlockSpec((1, rank), lambda i: (0, 0)),
                  pl.BlockSpec((rank, wide), lambda i: (0, 0)),
                  pl.BlockSpec((1, 2 * HEAD_DIM), lambda i: (0, 0)),
                  tab, tab, tab],
        out_specs=pl.BlockSpec((tm, wide), lambda i: (i, 0)),
        out_shape=jax.ShapeDtypeStruct((nb * seq, wide), BF16),
        compiler_params=_cparams("parallel"),
    )(proj, cq_gain.reshape(1, rank), w_q, q_gain2, *tabs)


def _mla_kv_body(ckv_ref, g_ref, w_ref, kg_ref, kr_ref, krg_ref, cos_ref, sa_ref, sb_ref,
                 k_ref, v_ref, *, heads):
    kv = _dot(_rms(ckv_ref[...].astype(F32), g_ref[...]).astype(BF16), w_ref[...])
    kro = _rope_pad(kr_ref[...].astype(F32), krg_ref[...], cos_ref[...], sa_ref[...],
                    sb_ref[...]).astype(k_ref.dtype)
    for h in range(heads):
        c0 = h * 2 * HEAD_DIM
        k_ref[:, c0:c0 + HEAD_DIM] = _rms(kv[:, c0:c0 + HEAD_DIM], kg_ref[...]).astype(k_ref.dtype)
        k_ref[:, c0 + HEAD_DIM:c0 + 2 * HEAD_DIM] = kro
        v_ref[:, h * HEAD_DIM:(h + 1) * HEAD_DIM] = kv[:, c0 + HEAD_DIM:c0 + 2 * HEAD_DIM].astype(v_ref.dtype)


def _mla_kv(proj, ckv_gain, w_kv, k_gain_n, k_gain_r, tabs, *, heads, ckv_blk, kr_blk, nb, seq, tm):
    rank = w_kv.shape[0]
    nt = seq // tm
    wide = heads * 2 * HEAD_DIM
    tab = pl.BlockSpec((tm, LANES), lambda i: (i % nt, 0))
    return _pcall(
        functools.partial(_mla_kv_body, heads=heads),
        grid=(nb * nt,),
        in_specs=[pl.BlockSpec((tm, rank), lambda i: (i, ckv_blk)),
                  pl.BlockSpec((1, rank), lambda i: (0, 0)),
                  pl.BlockSpec((rank, wide), lambda i: (0, 0)),
                  pl.BlockSpec((1, HEAD_DIM), lambda i: (0, 0)),
                  pl.BlockSpec((tm, LANES), lambda i: (i, kr_blk)),
                  pl.BlockSpec((1, LANES), lambda i: (0, 0)),
                  tab, tab, tab],
        out_specs=[pl.BlockSpec((tm, wide), lambda i: (i, 0)),
                   pl.BlockSpec((tm, heads * HEAD_DIM), lambda i: (i, 0))],
        out_shape=[jax.ShapeDtypeStruct((nb * seq, wide), BF16),
                   jax.ShapeDtypeStruct((nb * seq, heads * HEAD_DIM), BF16)],
        compiler_params=_cparams("parallel"),
    )(proj, ckv_gain.reshape(1, rank), w_kv, k_gain_n, proj, k_gain_r, *tabs)


def _mla_attn_body(q_ref, k_ref, v_ref, o_ref, *, tq, tk, hp):
    qi = pl.program_id(2)
    hs = range(hp)
    qk = lambda h: slice(h * 2 * HEAD_DIM, (h + 1) * 2 * HEAD_DIM)
    vs = lambda h: slice(h * HEAD_DIM, (h + 1) * HEAD_DIM)
    q = [q_ref[:, qk(h)] for h in hs]
    rr = lax.broadcasted_iota(jnp.int32, (tq, tk), 0)
    cc = lax.broadcasted_iota(jnp.int32, (tq, tk), 1)
    nsub = tq // tk

    def block(j, carry, diag):
        m, l, acc = carry
        ks = pl.multiple_of(j * tk, tk)
        s = [_dot_nt(q[h], k_ref[pl.ds(ks, tk), qk(h)]) for h in hs]
        if diag:
            vis = cc + (j - qi * nsub) * tk <= rr
            s = [jnp.where(vis, s[h], NEG_INF) for h in hs]
        m_new = tuple(jnp.maximum(m[h], jnp.max(s[h], axis=-1, keepdims=True)) for h in hs)
        alpha = [jnp.exp(m[h] - m_new[h]) for h in hs]
        p = [jnp.exp(s[h] - m_new[h]) for h in hs]
        l = tuple(alpha[h] * l[h] + jnp.sum(p[h], axis=-1, keepdims=True) for h in hs)
        acc = tuple(alpha[h] * acc[h] + _dot(p[h].astype(BF16), v_ref[pl.ds(ks, tk), vs(h)]) for h in hs)
        return m_new, l, acc

    init = (tuple(jnp.full((tq, 1), NEG_INF, F32) for _ in hs),
            tuple(jnp.zeros((tq, 1), F32) for _ in hs),
            tuple(jnp.zeros((tq, HEAD_DIM), F32) for _ in hs))
    carry = lax.fori_loop(0, qi * nsub, lambda j, c: block(j, c, False), init)
    for d in range(nsub):
        carry = block(qi * nsub + d, carry, True)
    _, l, acc = carry
    for h in hs:
        o_ref[:, vs(h)] = (acc[h] / l[h]).astype(o_ref.dtype)


def _mla_attention(q, k, v, *, heads, tq, tk):
    nb, seq, _ = q.shape
    hp = _pick(heads, (2, 1))
    return _pcall(
        functools.partial(_mla_attn_body, tq=tq, tk=tk, hp=hp),
        grid=(nb, heads // hp, seq // tq),
        in_specs=[pl.BlockSpec((None, tq, hp * 2 * HEAD_DIM), lambda b, h, i: (b, i, h)),
                  pl.BlockSpec((None, seq, hp * 2 * HEAD_DIM), lambda b, h, i: (b, 0, h)),
                  pl.BlockSpec((None, seq, hp * HEAD_DIM), lambda b, h, i: (b, 0, h))],
        out_specs=pl.BlockSpec((None, tq, hp * HEAD_DIM), lambda b, h, i: (b, i, h)),
        out_shape=jax.ShapeDtypeStruct((nb, seq, heads * HEAD_DIM), BF16),
        compiler_params=_cparams("parallel", "parallel", "arbitrary"),
    )(q, k, v)


def _dn_prep_body(x_ref, bd_ref, cw_ref, alog_ref, dtb_ref, o_ref, gb_ref, xbuf, *, ts, heads):
    width = 3 * heads * HEAD_DIM

    @pl.when(pl.program_id(1) == 0)
    def _():
        xbuf[0:8, :] = jnp.zeros((8, width), F32)

    y = _causal_conv(xbuf, x_ref[...].astype(F32), cw_ref[...], ts)
    y = y * _sigmoid(y)
    qscale = HEAD_DIM ** -0.5
    for hh in range(3 * heads):
        blk = y[:, hh * HEAD_DIM:(hh + 1) * HEAD_DIM]
        if hh < 2 * heads:
            blk = blk * lax.rsqrt(jnp.sum(blk * blk, axis=-1, keepdims=True) + NORM_EPS)
            if hh < heads:
                blk = blk * qscale
        o_ref[:, hh * HEAD_DIM:(hh + 1) * HEAD_DIM] = blk
    bd = bd_ref[...]
    beta = _sigmoid(bd[:, :LANES])
    g = -jnp.exp(alog_ref[...]) * _softplus(bd[:, LANES:] + dtb_ref[...])
    rows = lax.broadcasted_iota(jnp.int32, (ts, 1), 0) % DN_CHUNK
    d = 1
    while d < DN_CHUNK:
        g = g + jnp.where(rows >= d, pltpu.roll(g, d, 0), 0.0)
        d *= 2
    gb_ref[:, :LANES] = g
    gb_ref[:, LANES:] = beta


def _dn_prep(proj, bd, conv_w, a_log, dt_bias, *, heads, qkv_blk, nb, seq, ts):
    width = 3 * heads * HEAD_DIM
    nt = seq // ts
    pad = lambda v: jnp.zeros((1, LANES), F32).at[0, :heads].set(v)
    return _pcall(
        functools.partial(_dn_prep_body, ts=ts, heads=heads),
        grid=(nb, nt),
        in_specs=[pl.BlockSpec((ts, width), lambda b, t: (b * nt + t, qkv_blk)),
                  pl.BlockSpec((ts, 2 * LANES), lambda b, t: (b * nt + t, 0)),
                  pl.BlockSpec(conv_w.shape, lambda b, t: (0, 0)),
                  pl.BlockSpec((1, LANES), lambda b, t: (0, 0)),
                  pl.BlockSpec((1, LANES), lambda b, t: (0, 0))],
        out_specs=[pl.BlockSpec((ts, width), lambda b, t: (b * nt + t, 0)),
                   pl.BlockSpec((ts, 2 * LANES), lambda b, t: (b * nt + t, 0))],
        out_shape=[jax.ShapeDtypeStruct((nb * seq, width), F32),
                   jax.ShapeDtypeStruct((nb * seq, 2 * LANES), F32)],
        scratch_shapes=[pltpu.VMEM((ts + 8, width), F32)],
        compiler_params=_cparams("parallel", "arbitrary"),
    )(proj, bd, conv_w, pad(a_log), pad(dt_bias))


def _bdot(a, b):
    return _dot(a.astype(BF16), b.astype(BF16))


def _dn_body(qkv_ref, gb_ref, z_ref, og_ref, o_ref, state, *, heads):
    c = DN_CHUNK
    hd = HEAD_DIM

    @pl.when(pl.program_id(1) == 0)
    def _():
        state[...] = jnp.zeros_like(state)

    rr = lax.broadcasted_iota(jnp.int32, (c, c), 0)
    cc = lax.broadcasted_iota(jnp.int32, (c, c), 1)
    lower = rr >= cc
    strict = rr > cc
    eye = (rr == cc).astype(F32)
    same = lambda sh: jnp.right_shift(rr, sh) == jnp.right_shift(cc, sh)
    base_sh = 4
    in_base = strict & same(base_sh)
    off_levels = [strict & same(sh + 1) & jnp.logical_not(same(sh))
                  for sh in range(base_sh, int(math.log2(c)))]
    gb = gb_ref[...]
    hs = range(heads)
    q = [qkv_ref[:, h * hd:(h + 1) * hd] for h in hs]
    k = [qkv_ref[:, (heads + h) * hd:(heads + h + 1) * hd] for h in hs]
    v = [qkv_ref[:, (2 * heads + h) * hd:(2 * heads + h + 1) * hd] for h in hs]
    gcol = [gb[:, h:h + 1] for h in hs]
    bcol = [gb[:, LANES + h:LANES + h + 1] for h in hs]
    gcb = [jnp.broadcast_to(gcol[h], (c, c)) for h in hs]
    dec = [jnp.where(lower, jnp.exp(jnp.where(lower, gcb[h] - gcb[h].T, 0.0)), 0.0) for h in hs]
    kb = [k[h] * bcol[h] for h in hs]
    kq = [_dot_nt(jnp.concatenate([kb[h], q[h]], axis=0).astype(BF16), k[h].astype(BF16)) for h in hs]
    a_mat = [jnp.where(strict, kq[h][:c] * dec[h], 0.0) for h in hs]
    intra = [kq[h][c:] * dec[h] for h in hs]
    pw = [-jnp.where(in_base, a_mat[h], 0.0) for h in hs]
    p_mat = [eye + pw[h] for h in hs]
    for _ in range(base_sh - 1):
        pw = [_bdot(pw[h], pw[h]) for h in hs]
        p_mat = [p_mat[h] + _bdot(p_mat[h], pw[h]) for h in hs]
    for off in off_levels:
        pc = [_bdot(p_mat[h], jnp.where(off, a_mat[h], 0.0)) for h in hs]
        p_mat = [p_mat[h] - _bdot(pc[h], p_mat[h]) for h in hs]
    eg = [jnp.exp(gcol[h]) for h in hs]
    uw = [_bdot(p_mat[h], jnp.concatenate([v[h] * bcol[h], kb[h] * eg[h]], axis=1)) for h in hs]
    glast = [gcol[h][c - 1:c, :] for h in hs]
    s_old = [state[h] for h in hs]
    ws = [_bdot(jnp.concatenate([uw[h][:, hd:], q[h] * eg[h]], axis=0), s_old[h]) for h in hs]
    v_new = [uw[h][:, :hd] - ws[h][:c] for h in hs]
    o = [ws[h][c:] + _bdot(intra[h], v_new[h]) for h in hs]
    k_dec = [k[h] * jnp.exp(glast[h] - gcol[h]) for h in hs]
    for h in hs:
        state[h] = s_old[h] * jnp.exp(glast[h]) + _bdot(k_dec[h].T, v_new[h])
    for h in hs:
        zz = z_ref[:, h * hd:(h + 1) * hd].astype(F32)
        o_ref[:, h * hd:(h + 1) * hd] = (_rms(o[h], og_ref[...]) * (zz * _sigmoid(zz))).astype(o_ref.dtype)


def _delta_rule(qkv, gb, proj, out_gain, *, heads, z_blk, nb, seq):
    c = DN_CHUNK
    nt = seq // c
    hw = heads * HEAD_DIM
    return _pcall(
        functools.partial(_dn_body, heads=heads),
        grid=(nb, nt),
        in_specs=[pl.BlockSpec((c, 3 * hw), lambda b, t: (b * nt + t, 0)),
                  pl.BlockSpec((c, 2 * LANES), lambda b, t: (b * nt + t, 0)),
                  pl.BlockSpec((c, hw), lambda b, t: (b * nt + t, z_blk)),
                  pl.BlockSpec((1, HEAD_DIM), lambda b, t: (0, 0))],
        out_specs=pl.BlockSpec((c, hw), lambda b, t: (b * nt + t, 0)),
        out_shape=jax.ShapeDtypeStruct((nb * seq, hw), BF16),
        scratch_shapes=[pltpu.VMEM((heads, HEAD_DIM, HEAD_DIM), F32)],
        compiler_params=_cparams("parallel", "arbitrary"),
    )(qkv, gb, proj, out_gain.reshape(1, HEAD_DIM))


def _merge_body(ya_ref, yb_ref, yc_ref, yd_ref, g0_ref, g1_ref, g2_ref, g3_ref, wb_ref, o_ref):
    acc = None
    for i, (y_ref, gl_ref) in enumerate(((ya_ref, g0_ref), (yb_ref, g1_ref), (yc_ref, g2_ref),
                                         (yd_ref, g3_ref))):
        t = _sigmoid(gl_ref[...].astype(F32)) * _dot(y_ref[...], wb_ref[i])
        acc = t if acc is None else acc + t
    o_ref[...] = acc.astype(o_ref.dtype)


def _gated_merge(ys, proj, w_branch, *, gate_off, tm, tn):
    m, bw = ys[0].shape
    nbr, _, n = w_branch.shape
    yspec = pl.BlockSpec((tm, bw), lambda i, j: (i, 0))
    gspecs = [pl.BlockSpec((tm, tn), functools.partial(lambda i, j, o: (i, o + j), o=(gate_off + r * n) // tn))
              for r in range(nbr)]
    return _pcall(
        _merge_body,
        grid=(m // tm, n // tn),
        in_specs=[yspec] * nbr + gspecs + [pl.BlockSpec((nbr, bw, tn), lambda i, j: (0, 0, j))],
        out_specs=pl.BlockSpec((tm, tn), lambda i, j: (i, j)),
        out_shape=jax.ShapeDtypeStruct((m, n), BF16),
        compiler_params=_cparams("parallel", "arbitrary"),
    )(*ys, *([proj] * nbr), w_branch)


def _mem_kv_body(m_ref, g_ref, w_ref, kg_ref, k_ref, v_ref, *, heads):
    xn = _rms(m_ref[...], g_ref[...]).astype(BF16)
    kv = _dot(xn, w_ref[...])
    hw = heads * HEAD_DIM
    for h in range(heads):
        k_ref[:, h * HEAD_DIM:(h + 1) * HEAD_DIM] = _rms(kv[:, h * HEAD_DIM:(h + 1) * HEAD_DIM],
                                                         kg_ref[...]).astype(k_ref.dtype)
    v_ref[...] = kv[:, hw:].astype(v_ref.dtype)


def _mem_kv(mem2, g, wkv, k_gain, *, nb, mlen, heads):
    d = mem2.shape[1]
    hw = heads * HEAD_DIM
    out = pl.BlockSpec((mlen, hw), lambda b: (b, 0))
    shp = jax.ShapeDtypeStruct((nb * mlen, hw), BF16)
    return _pcall(
        functools.partial(_mem_kv_body, heads=heads),
        grid=(nb,),
        in_specs=[pl.BlockSpec((mlen, d), lambda b: (b, 0)),
                  pl.BlockSpec((1, d), lambda b: (0, 0)),
                  pl.BlockSpec((d, 2 * hw), lambda b: (0, 0)),
                  pl.BlockSpec((1, HEAD_DIM), lambda b: (0, 0))],
        out_specs=[out, out],
        out_shape=[shp, shp],
        compiler_params=_cparams("parallel"),
    )(mem2, g.reshape(1, d), wkv, k_gain.reshape(1, HEAD_DIM))


def _xattn_body(x_ref, g_ref, wq_ref, k_ref, v_ref, qg_ref, wo_ref, o_ref, *, heads):
    x = x_ref[...]
    q = _dot(_rms(x, g_ref[...]).astype(BF16), wq_ref[...])
    scale = HEAD_DIM ** -0.5
    outs = []
    for h in range(heads):
        sl = slice(h * HEAD_DIM, (h + 1) * HEAD_DIM)
        qh = (_rms(q[:, sl], qg_ref[...]) * scale).astype(BF16)
        s = _dot_nt(qh, k_ref[:, sl])
        p = jnp.exp(s - jnp.max(s, axis=-1, keepdims=True))
        l = jnp.sum(p, axis=-1, keepdims=True)
        outs.append(_dot(p.astype(BF16), v_ref[:, sl]) / l)
    o = jnp.concatenate(outs, axis=-1).astype(BF16)
    o_ref[...] = x + _dot(o, wo_ref[...])


def _cross_attention(x2, g, wq, k, v, q_gain, wo, *, nb, seq, mlen, heads, tm):
    d = x2.shape[1]
    hw = heads * HEAD_DIM
    nt = seq // tm
    xspec = pl.BlockSpec((tm, d), lambda b, t: (b * nt + t, 0))
    return _pcall(
        functools.partial(_xattn_body, heads=heads),
        grid=(nb, nt),
        in_specs=[xspec,
                  pl.BlockSpec((1, d), lambda b, t: (0, 0)),
                  pl.BlockSpec((d, hw), lambda b, t: (0, 0)),
                  pl.BlockSpec((mlen, hw), lambda b, t: (b, 0)),
                  pl.BlockSpec((mlen, hw), lambda b, t: (b, 0)),
                  pl.BlockSpec((1, HEAD_DIM), lambda b, t: (0, 0)),
                  pl.BlockSpec((hw, d), lambda b, t: (0, 0))],
        out_specs=xspec,
        out_shape=jax.ShapeDtypeStruct(x2.shape, F32),
        compiler_params=_cparams("parallel", "arbitrary"),
    )(x2, g.reshape(1, d), wq, k, v, q_gain.reshape(1, HEAD_DIM), wo)


def _router_body(x_ref, g_ref, w_ref, o_ref, *, n_experts):
    logits = _dot(_rms(x_ref[...], g_ref[...]), w_ref[...], HIGHEST)
    lane = lax.broadcasted_iota(jnp.int32, logits.shape, 1)
    lg = jnp.where(lane < n_experts, logits, -jnp.inf)
    m1 = jnp.max(lg, axis=-1, keepdims=True)
    i1 = jnp.min(jnp.where(lg == m1, lane, LANES), axis=-1, keepdims=True)
    lg2 = jnp.where(lane == i1, -jnp.inf, lg)
    m2 = jnp.max(lg2, axis=-1, keepdims=True)
    i2 = jnp.min(jnp.where(lg2 == m2, lane, LANES), axis=-1, keepdims=True)
    e2 = jnp.exp(m2 - m1)
    den = 1.0 + e2
    o_ref[...] = (jnp.where(lane == 0, 1.0 / den, 0.0) + jnp.where(lane == 1, e2 / den, 0.0)
                  + jnp.where(lane == 2, i1.astype(F32), 0.0) + jnp.where(lane == 3, i2.astype(F32), 0.0))


def _router(x2, g, w_router, *, tm):
    m, d = x2.shape
    ne = w_router.shape[1]
    wpad = jnp.zeros((d, LANES), F32).at[:, :ne].set(w_router)
    return _pcall(
        functools.partial(_router_body, n_experts=ne),
        grid=(m // tm,),
        in_specs=[pl.BlockSpec((tm, d), lambda i: (i, 0)),
                  pl.BlockSpec((1, d), lambda i: (0, 0)),
                  pl.BlockSpec((d, LANES), lambda i: (0, 0))],
        out_specs=pl.BlockSpec((tm, LANES), lambda i: (i, 0)),
        out_shape=jax.ShapeDtypeStruct((m, LANES), F32),
        compiler_params=_cparams("parallel"),
    )(x2, g.reshape(1, d), wpad)


def _rope_tables(seq):
    half = MLA_ROPE // 2
    inv_freq = ROPE_BASE ** (-jnp.arange(half, dtype=F32) / half)
    ang = jnp.arange(seq, dtype=jnp.int32).astype(F32)[:, None] * inv_freq[None, :]
    cos, sin = jnp.cos(ang), jnp.sin(ang)
    z = jnp.zeros((seq, half), F32)
    cos_t = jnp.concatenate([cos, cos, z, z], axis=-1)
    sin_a = jnp.concatenate([-sin, z, z, z], axis=-1)
    sin_b = jnp.concatenate([z, sin, z, z], axis=-1)
    return cos_t, sin_a, sin_b


def kernel(x, mem, mix_norm, w_in, rg_conv_w, rg_conv_b, rg_w_a, rg_b_a, rg_w_x, rg_b_x, rg_lambda, sb_q_gain, sb_k_gain, mla_cq_gain, mla_w_uq, mla_ckv_gain, mla_w_ukv, mla_q_gain, mla_k_gain, dn_conv_w, dn_a_log, dn_dt_bias, dn_out_gain, w_branch, w_out, xa_norm, mem_norm, xa_wq, xa_wkv, xa_q_gain, xa_k_gain, xa_wo, ffn_norm, ffn_w_gu, ffn_w_down, moe_router, moe_w_gu, moe_w_down):
    nb, seq, d = x.shape
    depth = w_in.shape[0]
    mlen = mem.shape[1]
    bw = rg_conv_w.shape[2]
    heads = bw // HEAD_DIM
    q_rank = mla_cq_gain.shape[1]
    kv_rank = mla_ckv_gain.shape[1]
    xa_heads = xa_wq.shape[2] // HEAD_DIM
    t_tok = nb * seq
    assert bw % HEAD_DIM == 0 and seq % 512 == 0 and d % 512 == 0
    assert mla_q_gain.shape[1] == HEAD_DIM + MLA_ROPE

    groups = (("rg", 2 * bw, bw), ("sb", 3 * bw, _pick(heads, (4, 2, 1)) * HEAD_DIM), ("cq", q_rank, q_rank),
              ("ckv", kv_rank, kv_rank), ("kr", MLA_ROPE, LANES), ("dnqkv", 3 * bw, 3 * bw),
              ("dnz", bw, bw), ("beta", heads, None), ("dt", heads, None), ("gate", 4 * d, d))
    src, off, pieces = {}, {}, []
    s_pos = p_pos = 0
    for name, width, align in groups:
        src[name] = (s_pos, s_pos + width)
        s_pos += width
        if align is None:
            continue
        start = -(-p_pos // align) * align
        pieces.append((start - p_pos, src[name]))
        off[name] = start
        p_pos = start + width
        if name == "kr":
            pieces.append((LANES - width, None))
            p_pos += LANES - width
    n_proj = -(-p_pos // 512) * 512
    pieces.append((n_proj - p_pos, None))
    assert s_pos == w_in.shape[2]

    x2 = x.reshape(t_tok, d)
    mem2 = mem.reshape(nb * mlen, d)
    tabs = _rope_tables(seq)
    mla_scale = (HEAD_DIM + MLA_ROPE) ** -0.5

    tm_big = _pick(t_tok, (1024, 512, 256, 128))
    tm_mid = _pick(t_tok, (512, 256, 128))
    ts_seq = _pick(seq, (512, 256, 128))

    for layer in range(depth):
        wi = w_in[layer]
        zc = lambda n: jnp.zeros((d, n), wi.dtype)
        cols = []
        for gap, rng in pieces:
            if gap:
                cols.append(zc(gap))
            if rng is not None:
                cols.append(wi[:, rng[0]:rng[1]])
        w_proj = jnp.concatenate(cols, axis=1).astype(BF16)
        w_bd = jnp.concatenate([wi[:, src["beta"][0]:src["beta"][1]], zc(LANES - heads),
                                wi[:, src["dt"][0]:src["dt"][1]], zc(LANES - heads)], axis=1)
        wq3 = mla_w_uq[layer].reshape(q_rank, heads, HEAD_DIM + MLA_ROPE)
        w_q = jnp.concatenate([wq3, jnp.zeros((q_rank, heads, HEAD_DIM - MLA_ROPE), F32)],
                              axis=-1).reshape(q_rank, heads * 2 * HEAD_DIM).astype(BF16)
        qg = mla_q_gain[layer] * mla_scale
        q_gain2 = jnp.concatenate([qg, jnp.zeros((HEAD_DIM - MLA_ROPE,), F32)]).reshape(1, 2 * HEAD_DIM)
        kgl = mla_k_gain[layer]
        k_gain_n = kgl[:HEAD_DIM].reshape(1, HEAD_DIM)
        k_gain_r = jnp.concatenate([kgl[HEAD_DIM:], jnp.zeros((LANES - MLA_ROPE,), F32)]).reshape(1, LANES)

        proj = _norm_matmul(x2, mix_norm[layer], w_proj, out_dtype=BF16, tm=tm_big,
                            tn=_pick(n_proj, (512, 256, 128)))
        bd = _norm_matmul(x2, mix_norm[layer], w_bd, out_dtype=F32, tm=tm_big, tn=2 * LANES,
                          precision=HIGHEST)
        proj3 = proj.reshape(nb, seq, n_proj)

        y_a = _rglru(proj, nb, seq, rg_conv_w[layer], rg_conv_b[layer], rg_w_a[layer], rg_b_a[layer],
                     rg_w_x[layer], rg_b_x[layer], rg_lambda[layer], ts=_pick(seq, (256, 128)))

        y_b = _sb_attention(proj3, sb_q_gain[layer], sb_k_gain[layer], heads=heads,
                            q_off=off["sb"], tq=_pick(seq, (256, 128)))

        mq = _mla_q(proj, mla_cq_gain[layer], w_q, q_gain2, tabs, heads=heads,
                    cq_blk=off["cq"] // q_rank, nb=nb, seq=seq, tm=ts_seq)
        mk, mv = _mla_kv(proj, mla_ckv_gain[layer], mla_w_ukv[layer].astype(BF16), k_gain_n,
                         k_gain_r, tabs, heads=heads, ckv_blk=off["ckv"] // kv_rank,
                         kr_blk=off["kr"] // LANES, nb=nb, seq=seq, tm=ts_seq)
        r3 = lambda a: a.reshape(nb, seq, a.shape[-1])
        y_c = _mla_attention(r3(mq), r3(mk), r3(mv), heads=heads, tq=ts_seq, tk=ts_seq)

        dn_qkv, dn_gb = _dn_prep(proj, bd, dn_conv_w[layer], dn_a_log[layer], dn_dt_bias[layer],
                                 heads=heads, qkv_blk=off["dnqkv"] // (3 * bw), nb=nb, seq=seq, ts=ts_seq)
        y_d = _delta_rule(dn_qkv, dn_gb, proj, dn_out_gain[layer], heads=heads, z_blk=off["dnz"] // bw,
                          nb=nb, seq=seq)

        merged = _gated_merge([y_a, y_b.reshape(t_tok, bw), y_c.reshape(t_tok, bw), y_d], proj,
                              w_branch[layer].astype(BF16), gate_off=off["gate"], tm=tm_mid,
                              tn=_pick(d, (512, 256, 128)))
        x2 = _matmul_residual(merged, w_out[layer].astype(BF16), x2, tm=tm_mid,
                              tn=_pick(d, (1024, 512, 256, 128)), tk=d)

        mk, mv = _mem_kv(mem2, mem_norm[layer], xa_wkv[layer].astype(BF16), xa_k_gain[layer],
                         nb=nb, mlen=mlen, heads=xa_heads)
        x2 = _cross_attention(x2, xa_norm[layer], xa_wq[layer].astype(BF16), mk, mv, xa_q_gain[layer],
                              xa_wo[layer].astype(BF16), nb=nb, seq=seq, mlen=mlen, heads=xa_heads,
                              tm=ts_seq)

        if layer % 2 == 0:
            wgu = ffn_w_gu[layer // 2].astype(BF16)
            f = wgu.shape[1] // 2
            hmid = _norm_swiglu(x2, ffn_norm[layer], wgu, tm=tm_mid, tn=_pick(f, (512, 256, 128)))
            x2 = _matmul_residual(hmid, ffn_w_down[layer // 2].astype(BF16), x2, tm=tm_mid,
                                  tn=_pick(d, (512, 256, 128)), tk=f)
        else:
            wgu = moe_w_gu[layer // 2].astype(BF16)
            ne, _, f2 = wgu.shape
            f = f2 // 2
            rt = _router(x2, ffn_norm[layer], moe_router[layer // 2], tm=tm_mid)
            pos, tile_expert, n_used, p_rows = _moe_plan(rt, ne, tm_mid)
            tm_c = _pick(t_tok, (256, 128))
            xg = _moe_scatter(x2, pos.reshape(t_tok // tm_mid, 1, TOP_K * tm_mid), p_rows, tm=tm_mid)
            hmid = _moe_up(xg, ffn_norm[layer], wgu, tile_expert, n_used, tm=tm_mid,
                           tn=_pick(f, (1408, 512, 256, 128)))
            yg = _moe_down(hmid, moe_w_down[layer // 2].astype(BF16), tile_expert, n_used, tm=tm_mid)
            x2 = _moe_combine(x2, rt, pos.reshape(t_tok // tm_c, 1, TOP_K * tm_c), yg, tm=tm_c)
    return x2.reshape(nb, seq, d)
```

```python
import functools
import math

import jax
import jax.numpy as jnp
import numpy as np
from jax import lax
from jax.experimental import pallas as pl
from jax.experimental.pallas import tpu as pltpu

F32 = jnp.float32
BF16 = jnp.bfloat16
HIGHEST = lax.Precision.HIGHEST

HEAD_DIM = 128
LANES = 128
MLA_ROPE = 64
RG_C = 8.0
ROPE_BASE = 10000.0
NORM_EPS = 1e-6
NEG_INF = -1e30
EXP_UNDERFLOW = -104.0
DN_CHUNK = 128
TOP_K = 2
VMEM_LIMIT = 56 * 1024 * 1024


def _cparams(*sem):
    return pltpu.CompilerParams(dimension_semantics=sem, vmem_limit_bytes=VMEM_LIMIT)


def _pcall(body, **kwargs):
    fn = getattr(body, "func", body)
    return pl.pallas_call(body, name=fn.__name__.strip("_").replace("_body", ""), **kwargs)


def _pick(n, prefs):
    for p in prefs:
        if p <= n and n % p == 0:
            return p
    return n


def _rms(x, g):
    return x * lax.rsqrt(jnp.mean(x * x, axis=-1, keepdims=True) + NORM_EPS) * g


def _sigmoid(x):
    return 1.0 / (1.0 + jnp.exp(-x))


def _softplus(x):
    return jnp.maximum(x, 0.0) + jnp.log(1.0 + jnp.exp(-jnp.abs(x)))


def _dot(a, b, precision=None):
    return jnp.dot(a, b, preferred_element_type=F32, precision=precision)


def _dot_nt(a, b, precision=None):
    return lax.dot_general(a, b, (((1,), (1,)), ((), ())), preferred_element_type=F32,
                           precision=precision)


def _norm_mm_body(x_ref, g_ref, w_ref, o_ref, xn_ref, *, precision):
    @pl.when(pl.program_id(1) == 0)
    def _():
        xn_ref[...] = _rms(x_ref[...].astype(F32), g_ref[...]).astype(xn_ref.dtype)

    o_ref[...] = _dot(xn_ref[...], w_ref[...], precision).astype(o_ref.dtype)


def _norm_matmul(x, g, w, *, out_dtype, tm, tn, precision=None):
    m, k = x.shape
    n = w.shape[1]
    return _pcall(
        functools.partial(_norm_mm_body, precision=precision),
        grid=(m // tm, n // tn),
        in_specs=[pl.BlockSpec((tm, k), lambda i, j: (i, 0)),
                  pl.BlockSpec((1, k), lambda i, j: (0, 0)),
                  pl.BlockSpec((k, tn), lambda i, j: (0, j))],
        out_specs=pl.BlockSpec((tm, tn), lambda i, j: (i, j)),
        out_shape=jax.ShapeDtypeStruct((m, n), out_dtype),
        scratch_shapes=[pltpu.VMEM((tm, k), w.dtype)],
        compiler_params=_cparams("parallel", "arbitrary"),
    )(x, g.reshape(1, k), w)


def _swiglu_body(x_ref, g_ref, wg_ref, wu_ref, o_ref, xn_ref):
    @pl.when(pl.program_id(1) == 0)
    def _():
        xn_ref[...] = _rms(x_ref[...], g_ref[...]).astype(BF16)

    xn = xn_ref[...]
    a = _dot(xn, wg_ref[...])
    b = _dot(xn, wu_ref[...])
    o_ref[...] = (a * _sigmoid(a) * b).astype(o_ref.dtype)


def _norm_swiglu(x, g, w_gu, *, tm, tn):
    m, k = x.shape
    f = w_gu.shape[1] // 2
    nj = f // tn
    return _pcall(
        _swiglu_body,
        grid=(m // tm, nj),
        in_specs=[pl.BlockSpec((tm, k), lambda i, j: (i, 0)),
                  pl.BlockSpec((1, k), lambda i, j: (0, 0)),
                  pl.BlockSpec((k, tn), lambda i, j: (0, j)),
                  pl.BlockSpec((k, tn), lambda i, j: (0, j + nj))],
        out_specs=pl.BlockSpec((tm, tn), lambda i, j: (i, j)),
        out_shape=jax.ShapeDtypeStruct((m, f), BF16),
        scratch_shapes=[pltpu.VMEM((tm, k), BF16)],
        compiler_params=_cparams("parallel", "arbitrary"),
    )(x, g.reshape(1, k), w_gu, w_gu)


def _row_copies(pos_ref, n_rows, make_copy, make_bulk):
    def issue(r, c):
        for k in range(TOP_K):
            make_copy(r, k, pos_ref[0, TOP_K * r + k]).start()
        return c

    lax.fori_loop(0, n_rows, issue, 0, unroll=4)
    for _ in range(TOP_K):
        make_bulk().wait()


def _moe_scatter_body(pos_ref, x_ref, xg_init_hbm, xg_hbm, sem, *, tm):
    del xg_init_hbm
    _row_copies(pos_ref, tm,
                lambda r, k, p: pltpu.make_async_copy(x_ref.at[pl.ds(r, 1)], xg_hbm.at[pl.ds(p, 1)], sem),
                lambda: pltpu.make_async_copy(x_ref, xg_hbm.at[pl.ds(0, tm)], sem))


def _moe_scatter(x2, pos3, n_rows, *, tm):
    t_tok, d = x2.shape
    return _pcall(
        functools.partial(_moe_scatter_body, tm=tm),
        grid=(t_tok // tm,),
        in_specs=[pl.BlockSpec((None, 1, TOP_K * tm), lambda i: (i, 0, 0), memory_space=pltpu.SMEM),
                  pl.BlockSpec((tm, d), lambda i: (i, 0)),
                  pl.BlockSpec(memory_space=pl.ANY)],
        out_specs=pl.BlockSpec(memory_space=pl.ANY),
        out_shape=jax.ShapeDtypeStruct((n_rows, d), x2.dtype),
        scratch_shapes=[pltpu.SemaphoreType.DMA(())],
        input_output_aliases={2: 0},
        compiler_params=_cparams("arbitrary"),
    )(pos3, x2, jnp.zeros((n_rows, d), x2.dtype))


def _moe_up_body(te_ref, nu_ref, xg_ref, g_ref, wg_ref, wu_ref, o_ref):
    del te_ref
    used = pl.program_id(1) < nu_ref[0]

    @pl.when(used)
    def _():
        xn = _rms(xg_ref[...], g_ref[...]).astype(BF16)
        a = _dot(xn, wg_ref[...])
        b = _dot(xn, wu_ref[...])
        o_ref[...] = (a * _sigmoid(a) * b).astype(o_ref.dtype)

    @pl.when(jnp.logical_not(used))
    def _():
        o_ref[...] = jnp.zeros_like(o_ref)


def _moe_up(xg, g, w_gu, tile_expert, n_used, *, tm, tn):
    p_rows, d = xg.shape
    f = w_gu.shape[2] // 2
    nj = f // tn
    grid_spec = pltpu.PrefetchScalarGridSpec(
        num_scalar_prefetch=2, grid=(nj, p_rows // tm),
        in_specs=[pl.BlockSpec((tm, d), lambda j, i, te, nu: (i, 0)),
                  pl.BlockSpec((1, d), lambda j, i, te, nu: (0, 0)),
                  pl.BlockSpec((None, d, tn), lambda j, i, te, nu: (te[i], 0, j)),
                  pl.BlockSpec((None, d, tn), lambda j, i, te, nu: (te[i], 0, j + nj))],
        out_specs=pl.BlockSpec((tm, tn), lambda j, i, te, nu: (i, j)))
    return _pcall(
        _moe_up_body, grid_spec=grid_spec,
        out_shape=jax.ShapeDtypeStruct((p_rows, f), BF16),
        compiler_params=_cparams("arbitrary", "arbitrary"),
    )(tile_expert, n_used, xg, g.reshape(1, d), w_gu, w_gu)


def _moe_down_body(te_ref, nu_ref, h_ref, w_ref, o_ref):
    del te_ref
    used = pl.program_id(0) < nu_ref[0]

    @pl.when(used)
    def _():
        o_ref[...] = _dot(h_ref[...], w_ref[...])

    @pl.when(jnp.logical_not(used))
    def _():
        o_ref[...] = jnp.zeros_like(o_ref)


def _moe_down(hmid, w_down, tile_expert, n_used, *, tm):
    p_rows, f = hmid.shape
    d = w_down.shape[2]
    grid_spec = pltpu.PrefetchScalarGridSpec(
        num_scalar_prefetch=2, grid=(p_rows // tm,),
        in_specs=[pl.BlockSpec((tm, f), lambda i, te, nu: (i, 0)),
                  pl.BlockSpec((None, f, d), lambda i, te, nu: (te[i], 0, 0))],
        out_specs=pl.BlockSpec((tm, d), lambda i, te, nu: (i, 0)))
    return _pcall(
        _moe_down_body, grid_spec=grid_spec,
        out_shape=jax.ShapeDtypeStruct((p_rows, d), F32),
        compiler_params=_cparams("arbitrary"),
    )(tile_expert, n_used, hmid, w_down)


def _moe_combine_body(pos_ref, rt_ref, x_ref, y_hbm, o_ref, buf, sem, *, tm):
    _row_copies(pos_ref, tm,
                lambda r, k, p: pltpu.make_async_copy(y_hbm.at[pl.ds(p, 1)], buf.at[k, pl.ds(r, 1)], sem),
                lambda: pltpu.make_async_copy(y_hbm.at[pl.ds(0, tm)], buf.at[0], sem))
    w = rt_ref[...]
    o_ref[...] = x_ref[...] + w[:, 0:1] * buf[0] + w[:, 1:2] * buf[1]


def _moe_combine(x2, rt, pos3, y, *, tm):
    t_tok, d = x2.shape
    return _pcall(
        functools.partial(_moe_combine_body, tm=tm),
        grid=(t_tok // tm,),
        in_specs=[pl.BlockSpec((None, 1, TOP_K * tm), lambda i: (i, 0, 0), memory_space=pltpu.SMEM),
                  pl.BlockSpec((tm, LANES), lambda i: (i, 0)),
                  pl.BlockSpec((tm, d), lambda i: (i, 0)),
                  pl.BlockSpec(memory_space=pl.ANY)],
        out_specs=pl.BlockSpec((tm, d), lambda i: (i, 0)),
        out_shape=jax.ShapeDtypeStruct((t_tok, d), F32),
        scratch_shapes=[pltpu.VMEM((TOP_K, tm, d), F32), pltpu.SemaphoreType.DMA(())],
        compiler_params=_cparams("arbitrary"),
    )(pos3, rt, x2, y)


def _moe_plan(rt, n_experts, tm):
    t_tok = rt.shape[0]
    e_flat = rt[:, 2:2 + TOP_K].astype(jnp.int32).reshape(-1)
    onehot = (e_flat[:, None] == jnp.arange(n_experts, dtype=jnp.int32)[None, :]).astype(jnp.int32)
    csum = jnp.cumsum(onehot, axis=0)
    padded = -(-csum[-1] // tm) * tm
    gend = jnp.cumsum(padded)
    pos = jnp.sum(onehot * (csum - 1 + (gend - padded)[None, :]), axis=1)
    n_tiles = (TOP_K * t_tok) // tm + n_experts
    tile_start = jnp.arange(n_tiles, dtype=jnp.int32) * tm
    tile_expert = jnp.minimum(jnp.sum((tile_start[:, None] >= gend[None, :]).astype(jnp.int32), axis=1),
                              n_experts - 1)
    return pos, tile_expert, (gend[-1:] // tm).astype(jnp.int32), n_tiles * tm


def _mm_res_body(a_ref, w_ref, r_ref, o_ref, *scratch, nk):
    if nk == 1:
        o_ref[...] = r_ref[...] + _dot(a_ref[...], w_ref[...])
        return
    acc_ref, = scratch
    kk = pl.program_id(2)

    @pl.when(kk == 0)
    def _():
        acc_ref[...] = jnp.zeros_like(acc_ref)

    acc_ref[...] += _dot(a_ref[...], w_ref[...])

    @pl.when(kk == nk - 1)
    def _():
        o_ref[...] = r_ref[...] + acc_ref[...]


def _matmul_residual(a, w, r, *, tm, tn, tk):
    m, k = a.shape
    n = w.shape[1]
    nk = k // tk
    return _pcall(
        functools.partial(_mm_res_body, nk=nk),
        grid=(m // tm, n // tn, nk),
        in_specs=[pl.BlockSpec((tm, tk), lambda i, j, kk: (i, kk)),
                  pl.BlockSpec((tk, tn), lambda i, j, kk: (kk, j)),
                  pl.BlockSpec((tm, tn), lambda i, j, kk: (i, j))],
        out_specs=pl.BlockSpec((tm, tn), lambda i, j, kk: (i, j)),
        out_shape=jax.ShapeDtypeStruct((m, n), F32),
        scratch_shapes=[pltpu.VMEM((tm, tn), F32)] if nk > 1 else [],
        compiler_params=_cparams("parallel", "parallel", "arbitrary"),
    )(a, w, r)


def _causal_conv(xbuf, x, cw, ts):
    kk = cw.shape[0]
    xbuf[8:8 + ts, :] = x
    y = cw[kk - 1:kk, :] * x
    for d in range(1, kk):
        y = y + cw[kk - 1 - d:kk - d, :] * xbuf[8 - d:8 - d + ts, :]
    xbuf[0:8, :] = xbuf[ts:ts + 8, :]
    return y


def _rglru_body(x_ref, gate_ref, cw_ref, cb_ref, wa_ref, ba_ref, wx_ref, bx_ref, lam_ref, o_ref,
                xbuf, hcar, *, ts, width):
    @pl.when(pl.program_id(1) == 0)
    def _():
        xbuf[0:8, :] = jnp.zeros((8, width), F32)
        hcar[...] = jnp.zeros_like(hcar)

    u = _causal_conv(xbuf, x_ref[...].astype(F32), cw_ref[...], ts) + cb_ref[...]
    ub = u.astype(BF16)
    rs, is_ = [], []
    for g in range(width // HEAD_DIM):
        blk = ub[:, g * HEAD_DIM:(g + 1) * HEAD_DIM]
        rs.append(_dot(blk, wa_ref[g]))
        is_.append(_dot(blk, wx_ref[g]))
    r = _sigmoid(jnp.concatenate(rs, axis=-1) + ba_ref[...])
    ig = _sigmoid(jnp.concatenate(is_, axis=-1) + bx_ref[...])
    log_a = (-RG_C) * r * _softplus(-lam_ref[...])
    a = jnp.exp(log_a)
    b = jnp.sqrt(1.0 - jnp.exp(2.0 * log_a)) * (ig * u)
    rows = lax.broadcasted_iota(jnp.int32, (ts, 1), 0)
    d = 1
    while d < ts:
        keep = rows >= d
        b = jnp.where(keep, a * pltpu.roll(b, d, 0) + b, b)
        a = jnp.where(keep, a * pltpu.roll(a, d, 0), a)
        d *= 2
    h = a * hcar[0:1, :] + b
    hcar[0:1, :] = h[ts - 1:ts, :]
    gt = gate_ref[...].astype(F32)
    gelu = 0.5 * gt * (1.0 + jnp.tanh(math.sqrt(2.0 / math.pi) * (gt + 0.044715 * (gt * gt * gt))))
    o_ref[...] = (h * gelu).astype(o_ref.dtype)


def _rglru(proj, nb, seq, conv_w, conv_b, w_a, b_a, w_x, b_x, lam, *, ts):
    width = conv_w.shape[1]
    nt = seq // ts
    row = lambda v: v.reshape(1, width)
    full2 = lambda shp: pl.BlockSpec(shp, lambda b, t: (0,) * len(shp))
    return _pcall(
        functools.partial(_rglru_body, ts=ts, width=width),
        grid=(nb, nt),
        in_specs=[pl.BlockSpec((ts, width), lambda b, t: (b * nt + t, 0)),
                  pl.BlockSpec((ts, width), lambda b, t: (b * nt + t, 1)),
                  full2(conv_w.shape), full2((1, width)), full2(w_a.shape), full2((1, width)),
                  full2(w_x.shape), full2((1, width)), full2((1, width))],
        out_specs=pl.BlockSpec((ts, width), lambda b, t: (b * nt + t, 0)),
        out_shape=jax.ShapeDtypeStruct((nb * seq, width), BF16),
        scratch_shapes=[pltpu.VMEM((ts + 8, width), F32), pltpu.VMEM((8, width), F32)],
        compiler_params=_cparams("parallel", "arbitrary"),
    )(proj, proj, conv_w, row(conv_b), w_a.astype(BF16), row(b_a), w_x.astype(BF16), row(b_x), row(lam))


def _sb_body(q_ref, k_ref, v_ref, qg_ref, kg_ref, o_ref, kn_ref, *, tq, scale, hp):
    qi = pl.program_id(2)
    hs = range(hp)
    sl = lambda h: slice(h * HEAD_DIM, (h + 1) * HEAD_DIM)

    @pl.when(qi == 0)
    def _():
        for h in hs:
            kn_ref[:, sl(h)] = _rms(k_ref[:, sl(h)].astype(F32), kg_ref[...]).astype(BF16)

    qn = [(_rms(q_ref[:, sl(h)].astype(F32), qg_ref[...]) * scale).astype(BF16) for h in hs]
    rr = lax.broadcasted_iota(jnp.int32, (tq, tq), 0)
    cc = lax.broadcasted_iota(jnp.int32, (tq, tq), 1)
    upper = (rr > cc).astype(BF16)

    def block(j, accs, runs, diag):
        ks = pl.multiple_of(j * tq, tq)
        z = [_dot_nt(qn[h], kn_ref[pl.ds(ks, tq), sl(h)]) for h in hs]
        sp = [_softplus(z[h]) for h in hs]
        if diag:
            vis = cc < rr
            lk = [jnp.where(vis, -sp[h], 0.0) for h in hs]
        else:
            lk = [-sp[h] for h in hs]
        hi = [lk[h].astype(BF16) for h in hs]
        lo = [(lk[h] - hi[h].astype(F32)).astype(BF16) for h in hs]
        cs = [_dot(hi[h], upper) + _dot(lo[h], upper) for h in hs]
        w = [jnp.exp(z[h] - sp[h] + cs[h] + runs[h]) for h in hs]
        if diag:
            w = [jnp.where(vis, w[h], 0.0) for h in hs]
        accs = tuple(accs[h] + _dot(w[h].astype(BF16), v_ref[pl.ds(ks, tq), sl(h)]) for h in hs)
        runs = tuple(runs[h] + jnp.sum(lk[h], axis=-1, keepdims=True) for h in hs)
        return accs, runs

    accs, runs = block(qi, tuple(jnp.zeros((tq, HEAD_DIM), F32) for _ in hs),
                       tuple(jnp.zeros((tq, 1), F32) for _ in hs), True)

    def more(c):
        return (c[0] >= 0) & (jnp.max(functools.reduce(jnp.maximum, c[2])) > EXP_UNDERFLOW)

    def step(c):
        a, r = block(c[0], c[1], c[2], False)
        return c[0] - 1, a, r

    _, accs, _ = lax.while_loop(more, step, (qi - 1, accs, runs))
    for h in hs:
        o_ref[:, sl(h)] = accs[h].astype(o_ref.dtype)


def _sb_attention(proj3, q_gain, k_gain, *, heads, q_off, tq):
    nb, seq, _ = proj3.shape
    hp = _pick(heads, (4, 2, 1))
    wblk = hp * HEAD_DIM
    bw = heads * HEAD_DIM
    assert q_off % wblk == 0 and bw % wblk == 0
    qb, kb, vb = q_off // wblk, (q_off + bw) // wblk, (q_off + 2 * bw) // wblk
    return _pcall(
        functools.partial(_sb_body, tq=tq, scale=HEAD_DIM ** -0.5, hp=hp),
        grid=(nb, heads // hp, seq // tq),
        in_specs=[pl.BlockSpec((None, tq, wblk), lambda b, h, i: (b, i, qb + h)),
                  pl.BlockSpec((None, seq, wblk), lambda b, h, i: (b, 0, kb + h)),
                  pl.BlockSpec((None, seq, wblk), lambda b, h, i: (b, 0, vb + h)),
                  pl.BlockSpec((1, HEAD_DIM), lambda b, h, i: (0, 0)),
                  pl.BlockSpec((1, HEAD_DIM), lambda b, h, i: (0, 0))],
        out_specs=pl.BlockSpec((None, tq, wblk), lambda b, h, i: (b, i, h)),
        out_shape=jax.ShapeDtypeStruct((nb, seq, bw), BF16),
        scratch_shapes=[pltpu.VMEM((seq, wblk), BF16)],
        compiler_params=_cparams("parallel", "parallel", "arbitrary"),
    )(proj3, proj3, proj3, q_gain.reshape(1, HEAD_DIM), k_gain.reshape(1, HEAD_DIM))


def _rope_pad(r, gain, cos_t, sin_a, sin_b):
    ms = jnp.sum(r * r, axis=-1, keepdims=True) * (1.0 / MLA_ROPE)
    rn = r * lax.rsqrt(ms + NORM_EPS) * gain
    return rn * cos_t + pltpu.roll(rn, LANES - MLA_ROPE // 2, 1) * sin_a + pltpu.roll(rn, MLA_ROPE // 2, 1) * sin_b


def _mla_q_body(cq_ref, g_ref, w_ref, qg_ref, cos_ref, sa_ref, sb_ref, o_ref, *, heads):
    q = _dot(_rms(cq_ref[...].astype(F32), g_ref[...]).astype(BF16), w_ref[...])
    qg = qg_ref[...]
    for h in range(heads):
        c0 = h * 2 * HEAD_DIM
        o_ref[:, c0:c0 + HEAD_DIM] = _rms(q[:, c0:c0 + HEAD_DIM], qg[:, :HEAD_DIM]).astype(o_ref.dtype)
        o_ref[:, c0 + HEAD_DIM:c0 + 2 * HEAD_DIM] = _rope_pad(
            q[:, c0 + HEAD_DIM:c0 + 2 * HEAD_DIM], qg[:, HEAD_DIM:], cos_ref[...], sa_ref[...],
            sb_ref[...]).astype(o_ref.dtype)


def _mla_q(proj, cq_gain, w_q, q_gain2, tabs, *, heads, cq_blk, nb, seq, tm):
    rank = w_q.shape[0]
    nt = seq // tm
    wide = heads * 2 * HEAD_DIM
    tab = pl.BlockSpec((tm, LANES), lambda i: (i % nt, 0))
    return _pcall(
        functools.partial(_mla_q_body, heads=heads),
        grid=(nb * nt,),
        in_specs=[pl.BlockSpec((tm, rank), lambda i: (i, cq_blk)),
                  pl.BlockSpec((1, rank), lambda i: (0, 0)),
                  pl.BlockSpec((rank, wide), lambda i: (0, 0)),
                  pl.BlockSpec((1, 2 * HEAD_DIM), lambda i: (0, 0)),
                  tab, tab, tab],
        out_specs=pl.BlockSpec((tm, wide), lambda i: (i, 0)),
        out_shape=jax.ShapeDtypeStruct((nb * seq, wide), BF16),
        compiler_params=_cparams("parallel"),
    )(proj, cq_gain.reshape(1, rank), w_q, q_gain2, *tabs)


def _mla_kv_body(ckv_ref, g_ref, w_ref, kg_ref, kr_ref, krg_ref, cos_ref, sa_ref, sb_ref,
                 k_ref, v_ref, *, heads):
    kv = _dot(_rms(ckv_ref[...].astype(F32), g_ref[...]).astype(BF16), w_ref[...])
    kro = _rope_pad(kr_ref[...].astype(F32), krg_ref[...], cos_ref[...], sa_ref[...],
                    sb_ref[...]).astype(k_ref.dtype)
    for h in range(heads):
        c0 = h * 2 * HEAD_DIM
        k_ref[:, c0:c0 + HEAD_DIM] = _rms(kv[:, c0:c0 + HEAD_DIM], kg_ref[...]).astype(k_ref.dtype)
        k_ref[:, c0 + HEAD_DIM:c0 + 2 * HEAD_DIM] = kro
        v_ref[:, h * HEAD_DIM:(h + 1) * HEAD_DIM] = kv[:, c0 + HEAD_DIM:c0 + 2 * HEAD_DIM].astype(v_ref.dtype)


def _mla_kv(proj, ckv_gain, w_kv, k_gain_n, k_gain_r, tabs, *, heads, ckv_blk, kr_blk, nb, seq, tm):
    rank = w_kv.shape[0]
    nt = seq // tm
    wide = heads * 2 * HEAD_DIM
    tab = pl.BlockSpec((tm, LANES), lambda i: (i % nt, 0))
    return _pcall(
        functools.partial(_mla_kv_body, heads=heads),
        grid=(nb * nt,),
        in_specs=[pl.BlockSpec((tm, rank), lambda i: (i, ckv_blk)),
                  pl.BlockSpec((1, rank), lambda i: (0, 0)),
                  pl.BlockSpec((rank, wide), lambda i: (0, 0)),
                  pl.BlockSpec((1, HEAD_DIM), lambda i: (0, 0)),
                  pl.BlockSpec((tm, LANES), lambda i: (i, kr_blk)),
                  pl.BlockSpec((1, LANES), lambda i: (0, 0)),
                  tab, tab, tab],
        out_specs=[pl.BlockSpec((tm, wide), lambda i: (i, 0)),
                   pl.BlockSpec((tm, heads * HEAD_DIM), lambda i: (i, 0))],
        out_shape=[jax.ShapeDtypeStruct((nb * seq, wide), BF16),
                   jax.ShapeDtypeStruct((nb * seq, heads * HEAD_DIM), BF16)],
        compiler_params=_cparams("parallel"),
    )(proj, ckv_gain.reshape(1, rank), w_kv, k_gain_n, proj, k_gain_r, *tabs)


def _mla_attn_body(q_ref, k_ref, v_ref, o_ref, *, tq, tk, hp):
    qi = pl.program_id(2)
    hs = range(hp)
    qk = lambda h: slice(h * 2 * HEAD_DIM, (h + 1) * 2 * HEAD_DIM)
    vs = lambda h: slice(h * HEAD_DIM, (h + 1) * HEAD_DIM)
    q = [q_ref[:, qk(h)] for h in hs]
    rr = lax.broadcasted_iota(jnp.int32, (tq, tk), 0)
    cc = lax.broadcasted_iota(jnp.int32, (tq, tk), 1)
    nsub = tq // tk

    ones = jnp.ones((tk, HEAD_DIM), BF16)

    def block(j, carry, diag):
        m, acc = carry
        ks = pl.multiple_of(j * tk, tk)
        s = [_dot_nt(q[h], k_ref[pl.ds(ks, tk), qk(h)]) for h in hs]
        if diag:
            vis = cc + (j - qi * nsub) * tk <= rr
            s = [jnp.where(vis, s[h], NEG_INF) for h in hs]
        m_new = tuple(jnp.maximum(m[h], jnp.max(s[h], axis=-1, keepdims=True)) for h in hs)
        alpha = [jnp.exp(m[h] - m_new[h]) for h in hs]
        p = [jnp.exp((s[h] - m_new[h]).astype(BF16)) for h in hs]
        v1 = [jnp.concatenate([v_ref[pl.ds(ks, tk), vs(h)], ones], axis=1) for h in hs]
        acc = tuple(alpha[h] * acc[h] + _dot(p[h], v1[h]) for h in hs)
        return m_new, acc

    init = (tuple(jnp.full((tq, 1), NEG_INF, F32) for _ in hs),
            tuple(jnp.zeros((tq, 2 * HEAD_DIM), F32) for _ in hs))
    carry = lax.fori_loop(0, qi * nsub, lambda j, c: block(j, c, False), init)
    for d in range(nsub):
        carry = block(qi * nsub + d, carry, True)
    _, acc = carry
    for h in hs:
        o_ref[:, vs(h)] = (acc[h][:, :HEAD_DIM] / acc[h][:, HEAD_DIM:HEAD_DIM + 1]).astype(o_ref.dtype)


def _mla_attention(q, k, v, *, heads, tq, tk):
    nb, seq, _ = q.shape
    hp = _pick(heads, (4, 2, 1))
    return _pcall(
        functools.partial(_mla_attn_body, tq=tq, tk=tk, hp=hp),
        grid=(nb, heads // hp, seq // tq),
        in_specs=[pl.BlockSpec((None, tq, hp * 2 * HEAD_DIM), lambda b, h, i: (b, i, h)),
                  pl.BlockSpec((None, seq, hp * 2 * HEAD_DIM), lambda b, h, i: (b, 0, h)),
                  pl.BlockSpec((None, seq, hp * HEAD_DIM), lambda b, h, i: (b, 0, h))],
        out_specs=pl.BlockSpec((None, tq, hp * HEAD_DIM), lambda b, h, i: (b, i, h)),
        out_shape=jax.ShapeDtypeStruct((nb, seq, heads * HEAD_DIM), BF16),
        compiler_params=_cparams("parallel", "parallel", "arbitrary"),
    )(q, k, v)


def _dn_prep_body(x_ref, bd_ref, cw_ref, alog_ref, dtb_ref, o_ref, gb_ref, xbuf, *, ts, heads):
    width = 3 * heads * HEAD_DIM

    @pl.when(pl.program_id(1) == 0)
    def _():
        xbuf[0:8, :] = jnp.zeros((8, width), F32)

    y = _causal_conv(xbuf, x_ref[...].astype(F32), cw_ref[...], ts)
    y = y * _sigmoid(y)
    qscale = HEAD_DIM ** -0.5
    for hh in range(3 * heads):
        blk = y[:, hh * HEAD_DIM:(hh + 1) * HEAD_DIM]
        if hh < 2 * heads:
            blk = blk * lax.rsqrt(jnp.sum(blk * blk, axis=-1, keepdims=True) + NORM_EPS)
            if hh < heads:
                blk = blk * qscale
        o_ref[:, hh * HEAD_DIM:(hh + 1) * HEAD_DIM] = blk
    bd = bd_ref[...].astype(F32)
    beta = _sigmoid(bd)
    g = -jnp.exp(alog_ref[...]) * _softplus(bd + dtb_ref[...])
    rows = lax.broadcasted_iota(jnp.int32, (ts, 1), 0) % DN_CHUNK
    d = 1
    while d < DN_CHUNK:
        g = g + jnp.where(rows >= d, pltpu.roll(g, d, 0), 0.0)
        d *= 2
    gb_ref[:, :LANES] = g
    gb_ref[:, LANES:] = beta


def _dn_prep(proj, conv_w, a_log, dt_bias, *, heads, qkv_blk, bd_blk, nb, seq, ts):
    width = 3 * heads * HEAD_DIM
    nt = seq // ts
    pad = lambda v: jnp.zeros((1, LANES), F32).at[0, heads:2 * heads].set(v)
    return _pcall(
        functools.partial(_dn_prep_body, ts=ts, heads=heads),
        grid=(nb, nt),
        in_specs=[pl.BlockSpec((ts, width), lambda b, t: (b * nt + t, qkv_blk)),
                  pl.BlockSpec((ts, LANES), lambda b, t: (b * nt + t, bd_blk)),
                  pl.BlockSpec(conv_w.shape, lambda b, t: (0, 0)),
                  pl.BlockSpec((1, LANES), lambda b, t: (0, 0)),
                  pl.BlockSpec((1, LANES), lambda b, t: (0, 0))],
        out_specs=[pl.BlockSpec((ts, width), lambda b, t: (b * nt + t, 0)),
                   pl.BlockSpec((ts, 2 * LANES), lambda b, t: (b * nt + t, 0))],
        out_shape=[jax.ShapeDtypeStruct((nb * seq, width), F32),
                   jax.ShapeDtypeStruct((nb * seq, 2 * LANES), F32)],
        scratch_shapes=[pltpu.VMEM((ts + 8, width), F32)],
        compiler_params=_cparams("parallel", "arbitrary"),
    )(proj, proj, conv_w, pad(a_log), pad(dt_bias))


def _bdot(a, b):
    return _dot(a.astype(BF16), b.astype(BF16))


def _dn_body(qkv_ref, gb_ref, z_ref, og_ref, o_ref, state, *, heads):
    c = DN_CHUNK
    hd = HEAD_DIM

    @pl.when(pl.program_id(1) == 0)
    def _():
        state[...] = jnp.zeros_like(state)

    rr = lax.broadcasted_iota(jnp.int32, (c, c), 0)
    cc = lax.broadcasted_iota(jnp.int32, (c, c), 1)
    lower = rr >= cc
    strict = rr > cc
    eye = (rr == cc).astype(F32)
    same = lambda sh: jnp.right_shift(rr, sh) == jnp.right_shift(cc, sh)
    base_sh = 4
    in_base = strict & same(base_sh)
    off_levels = [strict & same(sh + 1) & jnp.logical_not(same(sh))
                  for sh in range(base_sh, int(math.log2(c)))]
    gb = gb_ref[...]
    hs = range(heads)
    q = [qkv_ref[:, h * hd:(h + 1) * hd] for h in hs]
    k = [qkv_ref[:, (heads + h) * hd:(heads + h + 1) * hd] for h in hs]
    v = [qkv_ref[:, (2 * heads + h) * hd:(2 * heads + h + 1) * hd] for h in hs]
    gcol = [gb[:, heads + h:heads + h + 1] for h in hs]
    bcol = [gb[:, LANES + h:LANES + h + 1] for h in hs]
    gcb = [jnp.broadcast_to(gcol[h], (c, c)) for h in hs]
    dec = [jnp.where(lower, jnp.exp(jnp.where(lower, gcb[h] - gcb[h].T, 0.0)), 0.0) for h in hs]
    kb = [k[h] * bcol[h] for h in hs]
    kq = [_dot_nt(jnp.concatenate([kb[h], q[h]], axis=0).astype(BF16), k[h].astype(BF16)) for h in hs]
    a_mat = [jnp.where(strict, kq[h][:c] * dec[h], 0.0) for h in hs]
    intra = [kq[h][c:] * dec[h] for h in hs]
    pw = [-jnp.where(in_base, a_mat[h], 0.0) for h in hs]
    p_mat = [eye + pw[h] for h in hs]
    for _ in range(base_sh - 1):
        pw = [_bdot(pw[h], pw[h]) for h in hs]
        p_mat = [p_mat[h] + _bdot(p_mat[h], pw[h]) for h in hs]
    for off in off_levels:
        pc = [_bdot(p_mat[h], jnp.where(off, a_mat[h], 0.0)) for h in hs]
        p_mat = [p_mat[h] - _bdot(pc[h], p_mat[h]) for h in hs]
    eg = [jnp.exp(gcol[h]) for h in hs]
    uw = [_bdot(p_mat[h], jnp.concatenate([v[h] * bcol[h], kb[h] * eg[h]], axis=1)) for h in hs]
    glast = [gcol[h][c - 1:c, :] for h in hs]
    s_old = [state[h] for h in hs]
    ws = [_bdot(jnp.concatenate([uw[h][:, hd:], q[h] * eg[h]], axis=0), s_old[h]) for h in hs]
    v_new = [uw[h][:, :hd] - ws[h][:c] for h in hs]
    o = [ws[h][c:] + _bdot(intra[h], v_new[h]) for h in hs]
    k_dec = [k[h] * jnp.exp(glast[h] - gcol[h]) for h in hs]
    for h in hs:
        state[h] = s_old[h] * jnp.exp(glast[h]) + _bdot(k_dec[h].T, v_new[h])
    for h in hs:
        zz = z_ref[:, h * hd:(h + 1) * hd].astype(F32)
        o_ref[:, h * hd:(h + 1) * hd] = (_rms(o[h], og_ref[...]) * (zz * _sigmoid(zz))).astype(o_ref.dtype)


def _delta_rule(qkv, gb, proj, out_gain, *, heads, z_blk, nb, seq):
    c = DN_CHUNK
    nt = seq // c
    hw = heads * HEAD_DIM
    return _pcall(
        functools.partial(_dn_body, heads=heads),
        grid=(nb, nt),
        in_specs=[pl.BlockSpec((c, 3 * hw), lambda b, t: (b * nt + t, 0)),
                  pl.BlockSpec((c, 2 * LANES), lambda b, t: (b * nt + t, 0)),
                  pl.BlockSpec((c, hw), lambda b, t: (b * nt + t, z_blk)),
                  pl.BlockSpec((1, HEAD_DIM), lambda b, t: (0, 0))],
        out_specs=pl.BlockSpec((c, hw), lambda b, t: (b * nt + t, 0)),
        out_shape=jax.ShapeDtypeStruct((nb * seq, hw), BF16),
        scratch_shapes=[pltpu.VMEM((heads, HEAD_DIM, HEAD_DIM), F32)],
        compiler_params=_cparams("parallel", "arbitrary"),
    )(qkv, gb, proj, out_gain.reshape(1, HEAD_DIM))


def _merge_body(ya_ref, yb_ref, yc_ref, yd_ref, g0_ref, g1_ref, g2_ref, g3_ref, wb_ref, o_ref):
    acc = None
    for i, (y_ref, gl_ref) in enumerate(((ya_ref, g0_ref), (yb_ref, g1_ref), (yc_ref, g2_ref),
                                         (yd_ref, g3_ref))):
        t = _sigmoid(gl_ref[...].astype(F32)) * _dot(y_ref[...], wb_ref[i])
        acc = t if acc is None else acc + t
    o_ref[...] = acc.astype(o_ref.dtype)


def _gated_merge(ys, proj, w_branch, *, gate_off, tm, tn):
    m, bw = ys[0].shape
    nbr, _, n = w_branch.shape
    yspec = pl.BlockSpec((tm, bw), lambda i, j: (i, 0))
    gspecs = [pl.BlockSpec((tm, tn), functools.partial(lambda i, j, o: (i, o + j), o=(gate_off + r * n) // tn))
              for r in range(nbr)]
    return _pcall(
        _merge_body,
        grid=(m // tm, n // tn),
        in_specs=[yspec] * nbr + gspecs + [pl.BlockSpec((nbr, bw, tn), lambda i, j: (0, 0, j))],
        out_specs=pl.BlockSpec((tm, tn), lambda i, j: (i, j)),
        out_shape=jax.ShapeDtypeStruct((m, n), BF16),
        compiler_params=_cparams("parallel", "arbitrary"),
    )(*ys, *([proj] * nbr), w_branch)


def _mem_kv_body(m_ref, g_ref, w_ref, kg_ref, k_ref, v_ref, *, heads):
    xn = _rms(m_ref[...], g_ref[...]).astype(BF16)
    kv = _dot(xn, w_ref[...])
    hw = heads * HEAD_DIM
    for h in range(heads):
        k_ref[:, h * HEAD_DIM:(h + 1) * HEAD_DIM] = _rms(kv[:, h * HEAD_DIM:(h + 1) * HEAD_DIM],
                                                         kg_ref[...]).astype(k_ref.dtype)
    v_ref[...] = kv[:, hw:].astype(v_ref.dtype)


def _mem_kv(mem2, g, wkv, k_gain, *, nb, mlen, heads):
    d = mem2.shape[1]
    hw = heads * HEAD_DIM
    out = pl.BlockSpec((mlen, hw), lambda b: (b, 0))
    shp = jax.ShapeDtypeStruct((nb * mlen, hw), BF16)
    return _pcall(
        functools.partial(_mem_kv_body, heads=heads),
        grid=(nb,),
        in_specs=[pl.BlockSpec((mlen, d), lambda b: (b, 0)),
                  pl.BlockSpec((1, d), lambda b: (0, 0)),
                  pl.BlockSpec((d, 2 * hw), lambda b: (0, 0)),
                  pl.BlockSpec((1, HEAD_DIM), lambda b: (0, 0))],
        out_specs=[out, out],
        out_shape=[shp, shp],
        compiler_params=_cparams("parallel"),
    )(mem2, g.reshape(1, d), wkv, k_gain.reshape(1, HEAD_DIM))


def _xattn_body(x_ref, g_ref, wq_ref, k_ref, v_ref, qg_ref, wo_ref, o_ref, *, heads):
    x = x_ref[...]
    q = _dot(_rms(x, g_ref[...]).astype(BF16), wq_ref[...])
    scale = HEAD_DIM ** -0.5
    outs = []
    for h in range(heads):
        sl = slice(h * HEAD_DIM, (h + 1) * HEAD_DIM)
        qh = (_rms(q[:, sl], qg_ref[...]) * scale).astype(BF16)
        s = _dot_nt(qh, k_ref[:, sl])
        p = jnp.exp(s - jnp.max(s, axis=-1, keepdims=True))
        l = jnp.sum(p, axis=-1, keepdims=True)
        outs.append(_dot(p.astype(BF16), v_ref[:, sl]) / l)
    o = jnp.concatenate(outs, axis=-1).astype(BF16)
    o_ref[...] = x + _dot(o, wo_ref[...])


def _cross_attention(x2, g, wq, k, v, q_gain, wo, *, nb, seq, mlen, heads, tm):
    d = x2.shape[1]
    hw = heads * HEAD_DIM
    nt = seq // tm
    xspec = pl.BlockSpec((tm, d), lambda b, t: (b * nt + t, 0))
    return _pcall(
        functools.partial(_xattn_body, heads=heads),
        grid=(nb, nt),
        in_specs=[xspec,
                  pl.BlockSpec((1, d), lambda b, t: (0, 0)),
                  pl.BlockSpec((d, hw), lambda b, t: (0, 0)),
                  pl.BlockSpec((mlen, hw), lambda b, t: (b, 0)),
                  pl.BlockSpec((mlen, hw), lambda b, t: (b, 0)),
                  pl.BlockSpec((1, HEAD_DIM), lambda b, t: (0, 0)),
                  pl.BlockSpec((hw, d), lambda b, t: (0, 0))],
        out_specs=xspec,
        out_shape=jax.ShapeDtypeStruct(x2.shape, F32),
        compiler_params=_cparams("parallel", "arbitrary"),
    )(x2, g.reshape(1, d), wq, k, v, q_gain.reshape(1, HEAD_DIM), wo)


def _router_body(x_ref, g_ref, w_ref, o_ref, *, n_experts):
    logits = _dot(_rms(x_ref[...], g_ref[...]), w_ref[...], HIGHEST)
    lane = lax.broadcasted_iota(jnp.int32, logits.shape, 1)
    lg = jnp.where(lane < n_experts, logits, -jnp.inf)
    m1 = jnp.max(lg, axis=-1, keepdims=True)
    i1 = jnp.min(jnp.where(lg == m1, lane, LANES), axis=-1, keepdims=True)
    lg2 = jnp.where(lane == i1, -jnp.inf, lg)
    m2 = jnp.max(lg2, axis=-1, keepdims=True)
    i2 = jnp.min(jnp.where(lg2 == m2, lane, LANES), axis=-1, keepdims=True)
    e2 = jnp.exp(m2 - m1)
    den = 1.0 + e2
    o_ref[...] = (jnp.where(lane == 0, 1.0 / den, 0.0) + jnp.where(lane == 1, e2 / den, 0.0)
                  + jnp.where(lane == 2, i1.astype(F32), 0.0) + jnp.where(lane == 3, i2.astype(F32), 0.0))


def _router(x2, g, w_router, *, tm):
    m, d = x2.shape
    ne = w_router.shape[1]
    wpad = jnp.zeros((d, LANES), F32).at[:, :ne].set(w_router)
    return _pcall(
        functools.partial(_router_body, n_experts=ne),
        grid=(m // tm,),
        in_specs=[pl.BlockSpec((tm, d), lambda i: (i, 0)),
                  pl.BlockSpec((1, d), lambda i: (0, 0)),
                  pl.BlockSpec((d, LANES), lambda i: (0, 0))],
        out_specs=pl.BlockSpec((tm, LANES), lambda i: (i, 0)),
        out_shape=jax.ShapeDtypeStruct((m, LANES), F32),
        compiler_params=_cparams("parallel"),
    )(x2, g.reshape(1, d), wpad)


def _rope_tables(seq):
    half = MLA_ROPE // 2
    inv_freq = ROPE_BASE ** (-jnp.arange(half, dtype=F32) / half)
    ang = jnp.arange(seq, dtype=jnp.int32).astype(F32)[:, None] * inv_freq[None, :]
    cos, sin = jnp.cos(ang), jnp.sin(ang)
    z = jnp.zeros((seq, half), F32)
    cos_t = jnp.concatenate([cos, cos, z, z], axis=-1)
    sin_a = jnp.concatenate([-sin, z, z, z], axis=-1)
    sin_b = jnp.concatenate([z, sin, z, z], axis=-1)
    return cos_t, sin_a, sin_b


def kernel(x, mem, mix_norm, w_in, rg_conv_w, rg_conv_b, rg_w_a, rg_b_a, rg_w_x, rg_b_x, rg_lambda, sb_q_gain, sb_k_gain, mla_cq_gain, mla_w_uq, mla_ckv_gain, mla_w_ukv, mla_q_gain, mla_k_gain, dn_conv_w, dn_a_log, dn_dt_bias, dn_out_gain, w_branch, w_out, xa_norm, mem_norm, xa_wq, xa_wkv, xa_q_gain, xa_k_gain, xa_wo, ffn_norm, ffn_w_gu, ffn_w_down, moe_router, moe_w_gu, moe_w_down):
    nb, seq, d = x.shape
    depth = w_in.shape[0]
    mlen = mem.shape[1]
    bw = rg_conv_w.shape[2]
    heads = bw // HEAD_DIM
    q_rank = mla_cq_gain.shape[1]
    kv_rank = mla_ckv_gain.shape[1]
    xa_heads = xa_wq.shape[2] // HEAD_DIM
    t_tok = nb * seq
    assert bw % HEAD_DIM == 0 and seq % 512 == 0 and d % 512 == 0
    assert mla_q_gain.shape[1] == HEAD_DIM + MLA_ROPE

    src_widths = (("rg", 2 * bw), ("sb", 3 * bw), ("cq", q_rank), ("ckv", kv_rank), ("kr", MLA_ROPE),
                  ("dnqkv", 3 * bw), ("dnz", bw), ("bd", 2 * heads), ("gate", 4 * d))
    src, s_pos = {}, 0
    for name, width in src_widths:
        src[name] = (s_pos, s_pos + width)
        s_pos += width
    assert s_pos == w_in.shape[2] and 2 * heads <= LANES
    packed = (("rg", bw), ("sb", _pick(heads, (4, 2, 1)) * HEAD_DIM), ("cq", q_rank), ("ckv", kv_rank),
              ("kr", LANES), ("bd", LANES), ("dnqkv", 3 * bw), ("dnz", bw), ("gate", d))
    off, pieces, p_pos = {}, [], 0
    for name, align in packed:
        width = src[name][1] - src[name][0]
        start = -(-p_pos // align) * align
        pieces.append((start - p_pos, src[name]))
        off[name] = start
        p_pos = start + width
        if name in ("kr", "bd"):
            pieces.append((LANES - width, None))
            p_pos += LANES - width
    n_proj = -(-p_pos // 512) * 512
    pieces.append((n_proj - p_pos, None))

    x2 = x.reshape(t_tok, d)
    mem2 = mem.reshape(nb * mlen, d)
    tabs = _rope_tables(seq)
    mla_scale = (HEAD_DIM + MLA_ROPE) ** -0.5

    tm_big = _pick(t_tok, (1024, 512, 256, 128))
    tm_mid = _pick(t_tok, (512, 256, 128))
    ts_seq = _pick(seq, (512, 256, 128))

    for layer in range(depth):
        wi = w_in[layer]
        zc = lambda n: jnp.zeros((d, n), wi.dtype)
        cols = []
        for gap, rng in pieces:
            if gap:
                cols.append(zc(gap))
            if rng is not None:
                cols.append(wi[:, rng[0]:rng[1]])
        w_proj = jnp.concatenate(cols, axis=1).astype(BF16)
        wq3 = mla_w_uq[layer].reshape(q_rank, heads, HEAD_DIM + MLA_ROPE)
        w_q = jnp.concatenate([wq3, jnp.zeros((q_rank, heads, HEAD_DIM - MLA_ROPE), F32)],
                              axis=-1).reshape(q_rank, heads * 2 * HEAD_DIM).astype(BF16)
        qg = mla_q_gain[layer] * mla_scale
        q_gain2 = jnp.concatenate([qg, jnp.zeros((HEAD_DIM - MLA_ROPE,), F32)]).reshape(1, 2 * HEAD_DIM)
        kgl = mla_k_gain[layer]
        k_gain_n = kgl[:HEAD_DIM].reshape(1, HEAD_DIM)
        k_gain_r = jnp.concatenate([kgl[HEAD_DIM:], jnp.zeros((LANES - MLA_ROPE,), F32)]).reshape(1, LANES)

        proj = _norm_matmul(x2, mix_norm[layer], w_proj, out_dtype=BF16, tm=tm_big,
                            tn=_pick(n_proj, (512, 256, 128)))
        proj3 = proj.reshape(nb, seq, n_proj)

        y_a = _rglru(proj, nb, seq, rg_conv_w[layer], rg_conv_b[layer], rg_w_a[layer], rg_b_a[layer],
                     rg_w_x[layer], rg_b_x[layer], rg_lambda[layer], ts=_pick(seq, (256, 128)))

        y_b = _sb_attention(proj3, sb_q_gain[layer], sb_k_gain[layer], heads=heads,
                            q_off=off["sb"], tq=_pick(seq, (256, 128)))

        mq = _mla_q(proj, mla_cq_gain[layer], w_q, q_gain2, tabs, heads=heads,
                    cq_blk=off["cq"] // q_rank, nb=nb, seq=seq, tm=ts_seq)
        mk, mv = _mla_kv(proj, mla_ckv_gain[layer], mla_w_ukv[layer].astype(BF16), k_gain_n,
                         k_gain_r, tabs, heads=heads, ckv_blk=off["ckv"] // kv_rank,
                         kr_blk=off["kr"] // LANES, nb=nb, seq=seq, tm=ts_seq)
        r3 = lambda a: a.reshape(nb, seq, a.shape[-1])
        y_c = _mla_attention(r3(mq), r3(mk), r3(mv), heads=heads, tq=ts_seq, tk=ts_seq)

        dn_qkv, dn_gb = _dn_prep(proj, dn_conv_w[layer], dn_a_log[layer], dn_dt_bias[layer],
                                 heads=heads, qkv_blk=off["dnqkv"] // (3 * bw), bd_blk=off["bd"] // LANES,
                                 nb=nb, seq=seq, ts=ts_seq)
        y_d = _delta_rule(dn_qkv, dn_gb, proj, dn_out_gain[layer], heads=heads, z_blk=off["dnz"] // bw,
                          nb=nb, seq=seq)

        merged = _gated_merge([y_a, y_b.reshape(t_tok, bw), y_c.reshape(t_tok, bw), y_d], proj,
                              w_branch[layer].astype(BF16), gate_off=off["gate"], tm=tm_big,
                              tn=_pick(d, (512, 256, 128)))
        x2 = _matmul_residual(merged, w_out[layer].astype(BF16), x2, tm=tm_mid,
                              tn=_pick(d, (1024, 512, 256, 128)), tk=d)

        mk, mv = _mem_kv(mem2, mem_norm[layer], xa_wkv[layer].astype(BF16), xa_k_gain[layer],
                         nb=nb, mlen=mlen, heads=xa_heads)
        x2 = _cross_attention(x2, xa_norm[layer], xa_wq[layer].astype(BF16), mk, mv, xa_q_gain[layer],
                              xa_wo[layer].astype(BF16), nb=nb, seq=seq, mlen=mlen, heads=xa_heads,
                              tm=ts_seq)

        if layer % 2 == 0:
            wgu = ffn_w_gu[layer // 2].astype(BF16)
            f = wgu.shape[1] // 2
            hmid = _norm_swiglu(x2, ffn_norm[layer], wgu, tm=tm_mid, tn=_pick(f, (512, 256, 128)))
            x2 = _matmul_residual(hmid, ffn_w_down[layer // 2].astype(BF16), x2, tm=tm_mid,
                                  tn=_pick(d, (512, 256, 128)), tk=f)
        else:
            wgu = moe_w_gu[layer // 2].astype(BF16)
            ne, _, f2 = wgu.shape
            f = f2 // 2
            rt = _router(x2, ffn_norm[layer], moe_router[layer // 2], tm=tm_mid)
            pos, tile_expert, n_used, p_rows = _moe_plan(rt, ne, tm_mid)
            tm_c = _pick(t_tok, (256, 128))
            xg = _moe_scatter(x2, pos.reshape(t_tok // tm_mid, 1, TOP_K * tm_mid), p_rows, tm=tm_mid)
            hmid = _moe_up(xg, ffn_norm[layer], wgu, tile_expert, n_used, tm=tm_mid,
                           tn=_pick(f, (1408, 512, 256, 128)))
            yg = _moe_down(hmid, moe_w_down[layer // 2].astype(BF16), tile_expert, n_used, tm=tm_mid)
            x2 = _moe_combine(x2, rt, pos.reshape(t_tok // tm_c, 1, TOP_K * tm_c), yg, tm=tm_c)
    return x2.reshape(nb, seq, d)
```

```python
import functools
import math

import jax
import jax.numpy as jnp
import numpy as np
from jax import lax
from jax.experimental import pallas as pl
from jax.experimental.pallas import tpu as pltpu

F32 = jnp.float32
BF16 = jnp.bfloat16
HIGHEST = lax.Precision.HIGHEST

HEAD_DIM = 128
LANES = 128
MLA_ROPE = 64
RG_C = 8.0
ROPE_BASE = 10000.0
NORM_EPS = 1e-6
NEG_INF = -1e30
EXP_UNDERFLOW = -104.0
DN_CHUNK = 128
TOP_K = 2
VMEM_LIMIT = 56 * 1024 * 1024


def _cparams(*sem):
    return pltpu.CompilerParams(dimension_semantics=sem, vmem_limit_bytes=VMEM_LIMIT)


def _pcall(body, **kwargs):
    fn = getattr(body, "func", body)
    return pl.pallas_call(body, name=fn.__name__.strip("_").replace("_body", ""), **kwargs)


def _pick(n, prefs):
    for p in prefs:
        if p <= n and n % p == 0:
            return p
    return n


def _rms(x, g):
    return x * lax.rsqrt(jnp.mean(x * x, axis=-1, keepdims=True) + NORM_EPS) * g


def _sigmoid(x):
    return 1.0 / (1.0 + jnp.exp(-x))


def _softplus(x):
    return jnp.maximum(x, 0.0) + jnp.log(1.0 + jnp.exp(-jnp.abs(x)))


def _dot(a, b, precision=None):
    return jnp.dot(a, b, preferred_element_type=F32, precision=precision)


def _dot_nt(a, b, precision=None):
    return lax.dot_general(a, b, (((1,), (1,)), ((), ())), preferred_element_type=F32,
                           precision=precision)


def _norm_mm_body(x_ref, g_ref, w_ref, o_ref, xn_ref, *, precision):
    @pl.when(pl.program_id(1) == 0)
    def _():
        xn_ref[...] = _rms(x_ref[...].astype(F32), g_ref[...]).astype(xn_ref.dtype)

    o_ref[...] = _dot(xn_ref[...], w_ref[...], precision).astype(o_ref.dtype)


def _norm_matmul(x, g, w, *, out_dtype, tm, tn, precision=None):
    m, k = x.shape
    n = w.shape[1]
    return _pcall(
        functools.partial(_norm_mm_body, precision=precision),
        grid=(m // tm, n // tn),
        in_specs=[pl.BlockSpec((tm, k), lambda i, j: (i, 0)),
                  pl.BlockSpec((1, k), lambda i, j: (0, 0)),
                  pl.BlockSpec((k, tn), lambda i, j: (0, j))],
        out_specs=pl.BlockSpec((tm, tn), lambda i, j: (i, j)),
        out_shape=jax.ShapeDtypeStruct((m, n), out_dtype),
        scratch_shapes=[pltpu.VMEM((tm, k), w.dtype)],
        compiler_params=_cparams("parallel", "arbitrary"),
    )(x, g.reshape(1, k), w)


def _swiglu_body(x_ref, g_ref, wg_ref, wu_ref, o_ref, xn_ref):
    @pl.when(pl.program_id(1) == 0)
    def _():
        xn_ref[...] = _rms(x_ref[...], g_ref[...]).astype(BF16)

    xn = xn_ref[...]
    a = _dot(xn, wg_ref[...])
    b = _dot(xn, wu_ref[...])
    o_ref[...] = (a * _sigmoid(a) * b).astype(o_ref.dtype)


def _norm_swiglu(x, g, w_gu, *, tm, tn):
    m, k = x.shape
    f = w_gu.shape[1] // 2
    nj = f // tn
    return _pcall(
        _swiglu_body,
        grid=(m // tm, nj),
        in_specs=[pl.BlockSpec((tm, k), lambda i, j: (i, 0)),
                  pl.BlockSpec((1, k), lambda i, j: (0, 0)),
                  pl.BlockSpec((k, tn), lambda i, j: (0, j)),
                  pl.BlockSpec((k, tn), lambda i, j: (0, j + nj))],
        out_specs=pl.BlockSpec((tm, tn), lambda i, j: (i, j)),
        out_shape=jax.ShapeDtypeStruct((m, f), BF16),
        scratch_shapes=[pltpu.VMEM((tm, k), BF16)],
        compiler_params=_cparams("parallel", "arbitrary"),
    )(x, g.reshape(1, k), w_gu, w_gu)


def _row_copies(pos_ref, n_rows, make_copy, make_bulk):
    def issue(r, c):
        for k in range(TOP_K):
            make_copy(r, k, pos_ref[0, TOP_K * r + k]).start()
        return c

    lax.fori_loop(0, n_rows, issue, 0, unroll=4)
    for _ in range(TOP_K):
        make_bulk().wait()


def _moe_scatter_body(pos_ref, x_ref, xg_init_hbm, xg_hbm, sem, *, tm):
    del xg_init_hbm
    _row_copies(pos_ref, tm,
                lambda r, k, p: pltpu.make_async_copy(x_ref.at[pl.ds(r, 1)], xg_hbm.at[pl.ds(p, 1)], sem),
                lambda: pltpu.make_async_copy(x_ref, xg_hbm.at[pl.ds(0, tm)], sem))


def _moe_scatter(x2, pos3, n_rows, *, tm):
    t_tok, d = x2.shape
    return _pcall(
        functools.partial(_moe_scatter_body, tm=tm),
        grid=(t_tok // tm,),
        in_specs=[pl.BlockSpec((None, 1, TOP_K * tm), lambda i: (i, 0, 0), memory_space=pltpu.SMEM),
                  pl.BlockSpec((tm, d), lambda i: (i, 0)),
                  pl.BlockSpec(memory_space=pl.ANY)],
        out_specs=pl.BlockSpec(memory_space=pl.ANY),
        out_shape=jax.ShapeDtypeStruct((n_rows, d), x2.dtype),
        scratch_shapes=[pltpu.SemaphoreType.DMA(())],
        input_output_aliases={2: 0},
        compiler_params=_cparams("arbitrary"),
    )(pos3, x2, jnp.zeros((n_rows, d), x2.dtype))


def _moe_up_body(te_ref, nu_ref, xg_ref, g_ref, wg_ref, wu_ref, o_ref):
    del te_ref
    used = pl.program_id(1) < nu_ref[0]

    @pl.when(used)
    def _():
        xn = _rms(xg_ref[...], g_ref[...]).astype(BF16)
        a = _dot(xn, wg_ref[...])
        b = _dot(xn, wu_ref[...])
        o_ref[...] = (a * _sigmoid(a) * b).astype(o_ref.dtype)

    @pl.when(jnp.logical_not(used))
    def _():
        o_ref[...] = jnp.zeros_like(o_ref)


def _moe_up(xg, g, w_gu, tile_expert, n_used, *, tm, tn):
    p_rows, d = xg.shape
    f = w_gu.shape[2] // 2
    nj = f // tn
    grid_spec = pltpu.PrefetchScalarGridSpec(
        num_scalar_prefetch=2, grid=(nj, p_rows // tm),
        in_specs=[pl.BlockSpec((tm, d), lambda j, i, te, nu: (i, 0)),
                  pl.BlockSpec((1, d), lambda j, i, te, nu: (0, 0)),
                  pl.BlockSpec((None, d, tn), lambda j, i, te, nu: (te[i], 0, j)),
                  pl.BlockSpec((None, d, tn), lambda j, i, te, nu: (te[i], 0, j + nj))],
        out_specs=pl.BlockSpec((tm, tn), lambda j, i, te, nu: (i, j)))
    return _pcall(
        _moe_up_body, grid_spec=grid_spec,
        out_shape=jax.ShapeDtypeStruct((p_rows, f), BF16),
        compiler_params=_cparams("arbitrary", "arbitrary"),
    )(tile_expert, n_used, xg, g.reshape(1, d), w_gu, w_gu)


def _moe_down_body(te_ref, nu_ref, h_ref, w_ref, o_ref):
    del te_ref
    used = pl.program_id(0) < nu_ref[0]

    @pl.when(used)
    def _():
        o_ref[...] = _dot(h_ref[...], w_ref[...])

    @pl.when(jnp.logical_not(used))
    def _():
        o_ref[...] = jnp.zeros_like(o_ref)


def _moe_down(hmid, w_down, tile_expert, n_used, *, tm):
    p_rows, f = hmid.shape
    d = w_down.shape[2]
    grid_spec = pltpu.PrefetchScalarGridSpec(
        num_scalar_prefetch=2, grid=(p_rows // tm,),
        in_specs=[pl.BlockSpec((tm, f), lambda i, te, nu: (i, 0)),
                  pl.BlockSpec((None, f, d), lambda i, te, nu: (te[i], 0, 0))],
        out_specs=pl.BlockSpec((tm, d), lambda i, te, nu: (i, 0)))
    return _pcall(
        _moe_down_body, grid_spec=grid_spec,
        out_shape=jax.ShapeDtypeStruct((p_rows, d), F32),
        compiler_params=_cparams("arbitrary"),
    )(tile_expert, n_used, hmid, w_down)


def _moe_combine_body(pos_ref, rt_ref, x_ref, y_hbm, o_ref, buf, sem, *, tm):
    _row_copies(pos_ref, tm,
                lambda r, k, p: pltpu.make_async_copy(y_hbm.at[pl.ds(p, 1)], buf.at[k, pl.ds(r, 1)], sem),
                lambda: pltpu.make_async_copy(y_hbm.at[pl.ds(0, tm)], buf.at[0], sem))
    w = rt_ref[...]
    o_ref[...] = x_ref[...] + w[:, 0:1] * buf[0] + w[:, 1:2] * buf[1]


def _moe_combine(x2, rt, pos3, y, *, tm):
    t_tok, d = x2.shape
    return _pcall(
        functools.partial(_moe_combine_body, tm=tm),
        grid=(t_tok // tm,),
        in_specs=[pl.BlockSpec((None, 1, TOP_K * tm), lambda i: (i, 0, 0), memory_space=pltpu.SMEM),
                  pl.BlockSpec((tm, LANES), lambda i: (i, 0)),
                  pl.BlockSpec((tm, d), lambda i: (i, 0)),
                  pl.BlockSpec(memory_space=pl.ANY)],
        out_specs=pl.BlockSpec((tm, d), lambda i: (i, 0)),
        out_shape=jax.ShapeDtypeStruct((t_tok, d), F32),
        scratch_shapes=[pltpu.VMEM((TOP_K, tm, d), F32), pltpu.SemaphoreType.DMA(())],
        compiler_params=_cparams("arbitrary"),
    )(pos3, rt, x2, y)


def _moe_plan(rt, n_experts, tm):
    t_tok = rt.shape[0]
    e_flat = rt[:, 2:2 + TOP_K].astype(jnp.int32).reshape(-1)
    onehot = (e_flat[:, None] == jnp.arange(n_experts, dtype=jnp.int32)[None, :]).astype(jnp.int32)
    csum = jnp.cumsum(onehot, axis=0)
    padded = -(-csum[-1] // tm) * tm
    gend = jnp.cumsum(padded)
    pos = jnp.sum(onehot * (csum - 1 + (gend - padded)[None, :]), axis=1)
    n_tiles = (TOP_K * t_tok) // tm + n_experts
    tile_start = jnp.arange(n_tiles, dtype=jnp.int32) * tm
    tile_expert = jnp.minimum(jnp.sum((tile_start[:, None] >= gend[None, :]).astype(jnp.int32), axis=1),
                              n_experts - 1)
    return pos, tile_expert, (gend[-1:] // tm).astype(jnp.int32), n_tiles * tm


def _mm_res_body(a_ref, w_ref, r_ref, o_ref, *scratch, nk):
    if nk == 1:
        o_ref[...] = r_ref[...] + _dot(a_ref[...], w_ref[...])
        return
    acc_ref, = scratch
    kk = pl.program_id(2)

    @pl.when(kk == 0)
    def _():
        acc_ref[...] = jnp.zeros_like(acc_ref)

    acc_ref[...] += _dot(a_ref[...], w_ref[...])

    @pl.when(kk == nk - 1)
    def _():
        o_ref[...] = r_ref[...] + acc_ref[...]


def _matmul_residual(a, w, r, *, tm, tn, tk):
    m, k = a.shape
    n = w.shape[1]
    nk = k // tk
    return _pcall(
        functools.partial(_mm_res_body, nk=nk),
        grid=(m // tm, n // tn, nk),
        in_specs=[pl.BlockSpec((tm, tk), lambda i, j, kk: (i, kk)),
                  pl.BlockSpec((tk, tn), lambda i, j, kk: (kk, j)),
                  pl.BlockSpec((tm, tn), lambda i, j, kk: (i, j))],
        out_specs=pl.BlockSpec((tm, tn), lambda i, j, kk: (i, j)),
        out_shape=jax.ShapeDtypeStruct((m, n), F32),
        scratch_shapes=[pltpu.VMEM((tm, tn), F32)] if nk > 1 else [],
        compiler_params=_cparams("parallel", "parallel", "arbitrary"),
    )(a, w, r)


def _causal_conv(xbuf, x, cw, ts):
    kk = cw.shape[0]
    xbuf[8:8 + ts, :] = x
    y = cw[kk - 1:kk, :] * x
    for d in range(1, kk):
        y = y + cw[kk - 1 - d:kk - d, :] * xbuf[8 - d:8 - d + ts, :]
    xbuf[0:8, :] = xbuf[ts:ts + 8, :]
    return y


def _rglru_body(x_ref, gate_ref, cw_ref, cb_ref, wa_ref, ba_ref, wx_ref, bx_ref, lam_ref, o_ref,
                xbuf, hcar, *, ts, width):
    @pl.when(pl.program_id(1) == 0)
    def _():
        xbuf[0:8, :] = jnp.zeros((8, width), F32)
        hcar[...] = jnp.zeros_like(hcar)

    u = _causal_conv(xbuf, x_ref[...].astype(F32), cw_ref[...], ts) + cb_ref[...]
    ub = u.astype(BF16)
    rs, is_ = [], []
    for g in range(width // HEAD_DIM):
        blk = ub[:, g * HEAD_DIM:(g + 1) * HEAD_DIM]
        rs.append(_dot(blk, wa_ref[g]))
        is_.append(_dot(blk, wx_ref[g]))
    r = _sigmoid(jnp.concatenate(rs, axis=-1) + ba_ref[...])
    ig = _sigmoid(jnp.concatenate(is_, axis=-1) + bx_ref[...])
    log_a = (-RG_C) * r * _softplus(-lam_ref[...])
    a = jnp.exp(log_a)
    b = jnp.sqrt(1.0 - jnp.exp(2.0 * log_a)) * (ig * u)
    rows = lax.broadcasted_iota(jnp.int32, (ts, 1), 0)
    d = 1
    while d < ts:
        keep = rows >= d
        b = jnp.where(keep, a * pltpu.roll(b, d, 0) + b, b)
        a = jnp.where(keep, a * pltpu.roll(a, d, 0), a)
        d *= 2
    h = a * hcar[0:1, :] + b
    hcar[0:1, :] = h[ts - 1:ts, :]
    gt = gate_ref[...].astype(F32)
    gelu = 0.5 * gt * (1.0 + jnp.tanh(math.sqrt(2.0 / math.pi) * (gt + 0.044715 * (gt * gt * gt))))
    o_ref[...] = (h * gelu).astype(o_ref.dtype)


def _rglru(proj, nb, seq, conv_w, conv_b, w_a, b_a, w_x, b_x, lam, *, ts):
    width = conv_w.shape[1]
    nt = seq // ts
    row = lambda v: v.reshape(1, width)
    full2 = lambda shp: pl.BlockSpec(shp, lambda b, t: (0,) * len(shp))
    return _pcall(
        functools.partial(_rglru_body, ts=ts, width=width),
        grid=(nb, nt),
        in_specs=[pl.BlockSpec((ts, width), lambda b, t: (b * nt + t, 0)),
                  pl.BlockSpec((ts, width), lambda b, t: (b * nt + t, 1)),
                  full2(conv_w.shape), full2((1, width)), full2(w_a.shape), full2((1, width)),
                  full2(w_x.shape), full2((1, width)), full2((1, width))],
        out_specs=pl.BlockSpec((ts, width), lambda b, t: (b * nt + t, 0)),
        out_shape=jax.ShapeDtypeStruct((nb * seq, width), BF16),
        scratch_shapes=[pltpu.VMEM((ts + 8, width), F32), pltpu.VMEM((8, width), F32)],
        compiler_params=_cparams("parallel", "arbitrary"),
    )(proj, proj, conv_w, row(conv_b), w_a.astype(BF16), row(b_a), w_x.astype(BF16), row(b_x), row(lam))


def _sb_body(q_ref, k_ref, v_ref, qg_ref, kg_ref, o_ref, kn_ref, *, tq, scale, hp):
    qi = pl.program_id(2)
    hs = range(hp)
    sl = lambda h: slice(h * HEAD_DIM, (h + 1) * HEAD_DIM)

    @pl.when(qi == 0)
    def _():
        for h in hs:
            kn_ref[:, sl(h)] = _rms(k_ref[:, sl(h)].astype(F32), kg_ref[...]).astype(BF16)

    qn = [(_rms(q_ref[:, sl(h)].astype(F32), qg_ref[...]) * scale).astype(BF16) for h in hs]
    rr = lax.broadcasted_iota(jnp.int32, (tq, tq), 0)
    cc = lax.broadcasted_iota(jnp.int32, (tq, tq), 1)
    upper = (rr > cc).astype(BF16)

    def block(j, accs, runs, diag):
        ks = pl.multiple_of(j * tq, tq)
        z = [_dot_nt(qn[h], kn_ref[pl.ds(ks, tq), sl(h)]) for h in hs]
        sp = [_softplus(z[h]) for h in hs]
        if diag:
            vis = cc < rr
            lk = [jnp.where(vis, -sp[h], 0.0) for h in hs]
        else:
            lk = [-sp[h] for h in hs]
        hi = [lk[h].astype(BF16) for h in hs]
        lo = [(lk[h] - hi[h].astype(F32)).astype(BF16) for h in hs]
        cs = [_dot(hi[h], upper) + _dot(lo[h], upper) for h in hs]
        w = [jnp.exp(z[h] - sp[h] + cs[h] + runs[h]) for h in hs]
        if diag:
            w = [jnp.where(vis, w[h], 0.0) for h in hs]
        accs = tuple(accs[h] + _dot(w[h].astype(BF16), v_ref[pl.ds(ks, tq), sl(h)]) for h in hs)
        runs = tuple(runs[h] + jnp.sum(lk[h], axis=-1, keepdims=True) for h in hs)
        return accs, runs

    accs, runs = block(qi, tuple(jnp.zeros((tq, HEAD_DIM), F32) for _ in hs),
                       tuple(jnp.zeros((tq, 1), F32) for _ in hs), True)

    def more(c):
        return (c[0] >= 0) & (jnp.max(functools.reduce(jnp.maximum, c[2])) > EXP_UNDERFLOW)

    def step(c):
        a, r = block(c[0], c[1], c[2], False)
        return c[0] - 1, a, r

    _, accs, _ = lax.while_loop(more, step, (qi - 1, accs, runs))
    for h in hs:
        o_ref[:, sl(h)] = accs[h].astype(o_ref.dtype)


def _sb_attention(proj3, q_gain, k_gain, *, heads, q_off, tq):
    nb, seq, _ = proj3.shape
    hp = _pick(heads, (4, 2, 1))
    wblk = hp * HEAD_DIM
    bw = heads * HEAD_DIM
    assert q_off % wblk == 0 and bw % wblk == 0
    qb, kb, vb = q_off // wblk, (q_off + bw) // wblk, (q_off + 2 * bw) // wblk
    return _pcall(
        functools.partial(_sb_body, tq=tq, scale=HEAD_DIM ** -0.5, hp=hp),
        grid=(nb, heads // hp, seq // tq),
        in_specs=[pl.BlockSpec((None, tq, wblk), lambda b, h, i: (b, i, qb + h)),
                  pl.BlockSpec((None, seq, wblk), lambda b, h, i: (b, 0, kb + h)),
                  pl.BlockSpec((None, seq, wblk), lambda b, h, i: (b, 0, vb + h)),
                  pl.BlockSpec((1, HEAD_DIM), lambda b, h, i: (0, 0)),
                  pl.BlockSpec((1, HEAD_DIM), lambda b, h, i: (0, 0))],
        out_specs=pl.BlockSpec((None, tq, wblk), lambda b, h, i: (b, i, h)),
        out_shape=jax.ShapeDtypeStruct((nb, seq, bw), BF16),
        scratch_shapes=[pltpu.VMEM((seq, wblk), BF16)],
        compiler_params=_cparams("parallel", "parallel", "arbitrary"),
    )(proj3, proj3, proj3, q_gain.reshape(1, HEAD_DIM), k_gain.reshape(1, HEAD_DIM))


def _rope_pad(r, gain, cos_t, sin_a, sin_b):
    ms = jnp.sum(r * r, axis=-1, keepdims=True) * (1.0 / MLA_ROPE)
    rn = r * lax.rsqrt(ms + NORM_EPS) * gain
    return rn * cos_t + pltpu.roll(rn, LANES - MLA_ROPE // 2, 1) * sin_a + pltpu.roll(rn, MLA_ROPE // 2, 1) * sin_b


def _mla_q_body(cq_ref, g_ref, w_ref, qg_ref, cos_ref, sa_ref, sb_ref, o_ref, *, heads):
    q = _dot(_rms(cq_ref[...].astype(F32), g_ref[...]).astype(BF16), w_ref[...])
    qg = qg_ref[...]
    for h in range(heads):
        c0 = h * 2 * HEAD_DIM
        o_ref[:, c0:c0 + HEAD_DIM] = _rms(q[:, c0:c0 + HEAD_DIM], qg[:, :HEAD_DIM]).astype(o_ref.dtype)
        o_ref[:, c0 + HEAD_DIM:c0 + 2 * HEAD_DIM] = _rope_pad(
            q[:, c0 + HEAD_DIM:c0 + 2 * HEAD_DIM], qg[:, HEAD_DIM:], cos_ref[...], sa_ref[...],
            sb_ref[...]).astype(o_ref.dtype)


def _mla_q(proj, cq_gain, w_q, q_gain2, tabs, *, heads, cq_blk, nb, seq, tm):
    rank = w_q.shape[0]
    nt = seq // tm
    wide = heads * 2 * HEAD_DIM
    tab = pl.BlockSpec((tm, LANES), lambda i: (i % nt, 0))
    return _pcall(
        functools.partial(_mla_q_body, heads=heads),
        grid=(nb * nt,),
        in_specs=[pl.BlockSpec((tm, rank), lambda i: (i, cq_blk)),
                  pl.BlockSpec((1, rank), lambda i: (0, 0)),
                  pl.BlockSpec((rank, wide), lambda i: (0, 0)),
                  pl.BlockSpec((1, 2 * HEAD_DIM), lambda i: (0, 0)),
                  tab, tab, tab],
        out_specs=pl.BlockSpec((tm, wide), lambda i: (i, 0)),
        out_shape=jax.ShapeDtypeStruct((nb * seq, wide), BF16),
        compiler_params=_cparams("parallel"),
    )(proj, cq_gain.reshape(1, rank), w_q, q_gain2, *tabs)


def _mla_kv_body(ckv_ref, g_ref, w_ref, kg_ref, kr_ref, krg_ref, cos_ref, sa_ref, sb_ref,
                 k_ref, v_ref, *, heads):
    kv = _dot(_rms(ckv_ref[...].astype(F32), g_ref[...]).astype(BF16), w_ref[...])
    kro = _rope_pad(kr_ref[...].astype(F32), krg_ref[...], cos_ref[...], sa_ref[...],
                    sb_ref[...]).astype(k_ref.dtype)
    for h in range(heads):
        c0 = h * 2 * HEAD_DIM
        k_ref[:, c0:c0 + HEAD_DIM] = _rms(kv[:, c0:c0 + HEAD_DIM], kg_ref[...]).astype(k_ref.dtype)
        k_ref[:, c0 + HEAD_DIM:c0 + 2 * HEAD_DIM] = kro
        v_ref[:, h * HEAD_DIM:(h + 1) * HEAD_DIM] = kv[:, c0 + HEAD_DIM:c0 + 2 * HEAD_DIM].astype(v_ref.dtype)


def _mla_kv(proj, ckv_gain, w_kv, k_gain_n, k_gain_r, tabs, *, heads, ckv_blk, kr_blk, nb, seq, tm):
    rank = w_kv.shape[0]
    nt = seq // tm
    wide = heads * 2 * HEAD_DIM
    tab = pl.BlockSpec((tm, LANES), lambda i: (i % nt, 0))
    return _pcall(
        functools.partial(_mla_kv_body, heads=heads),
        grid=(nb * nt,),
        in_specs=[pl.BlockSpec((tm, rank), lambda i: (i, ckv_blk)),
                  pl.BlockSpec((1, rank), lambda i: (0, 0)),
                  pl.BlockSpec((rank, wide), lambda i: (0, 0)),
                  pl.BlockSpec((1, HEAD_DIM), lambda i: (0, 0)),
                  pl.BlockSpec((tm, LANES), lambda i: (i, kr_blk)),
                  pl.BlockSpec((1, LANES), lambda i: (0, 0)),
                  tab, tab, tab],
        out_specs=[pl.BlockSpec((tm, wide), lambda i: (i, 0)),
                   pl.BlockSpec((tm, heads * HEAD_DIM), lambda i: (i, 0))],
        out_shape=[jax.ShapeDtypeStruct((nb * seq, wide), BF16),
                   jax.ShapeDtypeStruct((nb * seq, heads * HEAD_DIM), BF16)],
        compiler_params=_cparams("parallel"),
    )(proj, ckv_gain.reshape(1, rank), w_kv, k_gain_n, proj, k_gain_r, *tabs)


def _mla_attn_body(q_ref, k_ref, v_ref, o_ref, *, tq, tk, hp):
    qi = pl.program_id(2)
    hs = range(hp)
    qk = lambda h: slice(h * 2 * HEAD_DIM, (h + 1) * 2 * HEAD_DIM)
    vs = lambda h: slice(h * HEAD_DIM, (h + 1) * HEAD_DIM)
    q = [q_ref[:, qk(h)] for h in hs]
    rr = lax.broadcasted_iota(jnp.int32, (tq, tk), 0)
    cc = lax.broadcasted_iota(jnp.int32, (tq, tk), 1)
    nsub = tq // tk

    ones = jnp.ones((tk, HEAD_DIM), BF16)

    def block(j, carry, diag):
        m, acc = carry
        ks = pl.multiple_of(j * tk, tk)
        s = [_dot_nt(q[h], k_ref[pl.ds(ks, tk), qk(h)]) for h in hs]
        if diag:
            vis = cc + (j - qi * nsub) * tk <= rr
            s = [jnp.where(vis, s[h], NEG_INF) for h in hs]
        m_new = tuple(jnp.maximum(m[h], jnp.max(s[h], axis=-1, keepdims=True)) for h in hs)
        alpha = [jnp.exp(m[h] - m_new[h]) for h in hs]
        p = [jnp.exp((s[h] - m_new[h]).astype(BF16)) for h in hs]
        v1 = [jnp.concatenate([v_ref[pl.ds(ks, tk), vs(h)], ones], axis=1) for h in hs]
        acc = tuple(alpha[h] * acc[h] + _dot(p[h], v1[h]) for h in hs)
        return m_new, acc

    init = (tuple(jnp.full((tq, 1), NEG_INF, F32) for _ in hs),
            tuple(jnp.zeros((tq, 2 * HEAD_DIM), F32) for _ in hs))
    carry = lax.fori_loop(0, qi * nsub, lambda j, c: block(j, c, False), init)
    for d in range(nsub):
        carry = block(qi * nsub + d, carry, True)
    _, acc = carry
    for h in hs:
        o_ref[:, vs(h)] = (acc[h][:, :HEAD_DIM] / acc[h][:, HEAD_DIM:HEAD_DIM + 1]).astype(o_ref.dtype)


def _mla_attention(q, k, v, *, heads, tq, tk):
    nb, seq, _ = q.shape
    hp = _pick(heads, (4, 2, 1))
    return _pcall(
        functools.partial(_mla_attn_body, tq=tq, tk=tk, hp=hp),
        grid=(nb, heads // hp, seq // tq),
        in_specs=[pl.BlockSpec((None, tq, hp * 2 * HEAD_DIM), lambda b, h, i: (b, i, h)),
                  pl.BlockSpec((None, seq, hp * 2 * HEAD_DIM), lambda b, h, i: (b, 0, h)),
                  pl.BlockSpec((None, seq, hp * HEAD_DIM), lambda b, h, i: (b, 0, h))],
        out_specs=pl.BlockSpec((None, tq, hp * HEAD_DIM), lambda b, h, i: (b, i, h)),
        out_shape=jax.ShapeDtypeStruct((nb, seq, heads * HEAD_DIM), BF16),
        compiler_params=_cparams("parallel", "parallel", "arbitrary"),
    )(q, k, v)


def _bdot(a, b):
    return _dot(a.astype(BF16), b.astype(BF16))


def _dn_body(x_ref, bd_ref, z_ref, cw_ref, alog_ref, dtb_ref, og_ref, o_ref, state, xbuf, *, heads):
    c = DN_CHUNK
    hd = HEAD_DIM

    @pl.when(pl.program_id(1) == 0)
    def _():
        state[...] = jnp.zeros_like(state)
        xbuf[0:8, :] = jnp.zeros((8, 3 * heads * hd), F32)

    xbuf[8:8 + c, :] = x_ref[...].astype(F32)
    cw = cw_ref[...]
    taps = cw.shape[0]

    def head_block(hh):
        sl = slice(hh * hd, (hh + 1) * hd)
        blk = cw[taps - 1:taps, sl] * xbuf[8:8 + c, sl]
        for dd in range(1, taps):
            blk = blk + cw[taps - 1 - dd:taps - dd, sl] * xbuf[8 - dd:8 - dd + c, sl]
        blk = blk * _sigmoid(blk)
        if hh < 2 * heads:
            blk = blk * lax.rsqrt(jnp.sum(blk * blk, axis=-1, keepdims=True) + NORM_EPS)
            if hh < heads:
                blk = blk * hd ** -0.5
        return blk

    bd = bd_ref[...].astype(F32)
    beta = _sigmoid(bd)
    g = -jnp.exp(alog_ref[...]) * _softplus(bd + dtb_ref[...])
    crow = lax.broadcasted_iota(jnp.int32, (c, 1), 0)
    d = 1
    while d < c:
        g = g + jnp.where(crow >= d, pltpu.roll(g, d, 0), 0.0)
        d *= 2

    rr = lax.broadcasted_iota(jnp.int32, (c, c), 0)
    cc = lax.broadcasted_iota(jnp.int32, (c, c), 1)
    lower = rr >= cc
    strict = rr > cc
    eye = (rr == cc).astype(F32)
    same = lambda sh: jnp.right_shift(rr, sh) == jnp.right_shift(cc, sh)
    base_sh = 4
    in_base = strict & same(base_sh)
    off_levels = [strict & same(sh + 1) & jnp.logical_not(same(sh))
                  for sh in range(base_sh, int(math.log2(c)))]
    hs = range(heads)
    q = [head_block(h) for h in hs]
    k = [head_block(heads + h) for h in hs]
    v = [head_block(2 * heads + h) for h in hs]
    xbuf[0:8, :] = xbuf[c:c + 8, :]
    gcol = [g[:, heads + h:heads + h + 1] for h in hs]
    bcol = [beta[:, h:h + 1] for h in hs]
    gcb = [jnp.broadcast_to(gcol[h], (c, c)) for h in hs]
    dec = [jnp.where(lower, jnp.exp(jnp.where(lower, gcb[h] - gcb[h].T, 0.0)), 0.0) for h in hs]
    kb = [k[h] * bcol[h] for h in hs]
    kq = [_dot_nt(jnp.concatenate([kb[h], q[h]], axis=0).astype(BF16), k[h].astype(BF16)) for h in hs]
    a_mat = [jnp.where(strict, kq[h][:c] * dec[h], 0.0) for h in hs]
    intra = [kq[h][c:] * dec[h] for h in hs]
    pw = [-jnp.where(in_base, a_mat[h], 0.0) for h in hs]
    p_mat = [eye + pw[h] for h in hs]
    for _ in range(base_sh - 1):
        pw = [_bdot(pw[h], pw[h]) for h in hs]
        p_mat = [p_mat[h] + _bdot(p_mat[h], pw[h]) for h in hs]
    for off in off_levels:
        pc = [_bdot(p_mat[h], jnp.where(off, a_mat[h], 0.0)) for h in hs]
        p_mat = [p_mat[h] - _bdot(pc[h], p_mat[h]) for h in hs]
    eg = [jnp.exp(gcol[h]) for h in hs]
    uw = [_bdot(p_mat[h], jnp.concatenate([v[h] * bcol[h], kb[h] * eg[h]], axis=1)) for h in hs]
    glast = [gcol[h][c - 1:c, :] for h in hs]
    s_old = [state[h] for h in hs]
    ws = [_bdot(jnp.concatenate([uw[h][:, hd:], q[h] * eg[h]], axis=0), s_old[h]) for h in hs]
    v_new = [uw[h][:, :hd] - ws[h][:c] for h in hs]
    o = [ws[h][c:] + _bdot(intra[h], v_new[h]) for h in hs]
    k_dec = [k[h] * jnp.exp(glast[h] - gcol[h]) for h in hs]
    for h in hs:
        state[h] = s_old[h] * jnp.exp(glast[h]) + _bdot(k_dec[h].T, v_new[h])
    for h in hs:
        zz = z_ref[:, h * hd:(h + 1) * hd].astype(F32)
        o_ref[:, h * hd:(h + 1) * hd] = (_rms(o[h], og_ref[...]) * (zz * _sigmoid(zz))).astype(o_ref.dtype)


def _delta_rule(proj, conv_w, a_log, dt_bias, out_gain, *, heads, qkv_blk, bd_blk, z_blk, nb, seq):
    c = DN_CHUNK
    nt = seq // c
    hw = heads * HEAD_DIM
    pad = lambda v: jnp.zeros((1, LANES), F32).at[0, heads:2 * heads].set(v)
    row = lambda b, t: b * nt + t
    return _pcall(
        functools.partial(_dn_body, heads=heads),
        grid=(nb, nt),
        in_specs=[pl.BlockSpec((c, 3 * hw), lambda b, t: (row(b, t), qkv_blk)),
                  pl.BlockSpec((c, LANES), lambda b, t: (row(b, t), bd_blk)),
                  pl.BlockSpec((c, hw), lambda b, t: (row(b, t), z_blk)),
                  pl.BlockSpec(conv_w.shape, lambda b, t: (0, 0)),
                  pl.BlockSpec((1, LANES), lambda b, t: (0, 0)),
                  pl.BlockSpec((1, LANES), lambda b, t: (0, 0)),
                  pl.BlockSpec((1, HEAD_DIM), lambda b, t: (0, 0))],
        out_specs=pl.BlockSpec((c, hw), lambda b, t: (row(b, t), 0)),
        out_shape=jax.ShapeDtypeStruct((nb * seq, hw), BF16),
        scratch_shapes=[pltpu.VMEM((heads, HEAD_DIM, HEAD_DIM), F32), pltpu.VMEM((c + 8, 3 * hw), F32)],
        compiler_params=_cparams("parallel", "arbitrary"),
    )(proj, proj, proj, conv_w, pad(a_log), pad(dt_bias), out_gain.reshape(1, HEAD_DIM))


def _merge_body(ya_ref, yb_ref, yc_ref, yd_ref, g0_ref, g1_ref, g2_ref, g3_ref, wb_ref, o_ref):
    acc = None
    for i, (y_ref, gl_ref) in enumerate(((ya_ref, g0_ref), (yb_ref, g1_ref), (yc_ref, g2_ref),
                                         (yd_ref, g3_ref))):
        t = _sigmoid(gl_ref[...].astype(F32)) * _dot(y_ref[...], wb_ref[i])
        acc = t if acc is None else acc + t
    o_ref[...] = acc.astype(o_ref.dtype)


def _gated_merge(ys, proj, w_branch, *, gate_off, tm, tn):
    m, bw = ys[0].shape
    nbr, _, n = w_branch.shape
    yspec = pl.BlockSpec((tm, bw), lambda i, j: (i, 0))
    gspecs = [pl.BlockSpec((tm, tn), functools.partial(lambda i, j, o: (i, o + j), o=(gate_off + r * n) // tn))
              for r in range(nbr)]
    return _pcall(
        _merge_body,
        grid=(m // tm, n // tn),
        in_specs=[yspec] * nbr + gspecs + [pl.BlockSpec((nbr, bw, tn), lambda i, j: (0, 0, j))],
        out_specs=pl.BlockSpec((tm, tn), lambda i, j: (i, j)),
        out_shape=jax.ShapeDtypeStruct((m, n), BF16),
        compiler_params=_cparams("parallel", "arbitrary"),
    )(*ys, *([proj] * nbr), w_branch)


def _mem_kv_body(m_ref, g_ref, w_ref, kg_ref, k_ref, v_ref, *, heads):
    xn = _rms(m_ref[...], g_ref[...]).astype(BF16)
    kv = _dot(xn, w_ref[...])
    hw = heads * HEAD_DIM
    for h in range(heads):
        k_ref[:, h * HEAD_DIM:(h + 1) * HEAD_DIM] = _rms(kv[:, h * HEAD_DIM:(h + 1) * HEAD_DIM],
                                                         kg_ref[...]).astype(k_ref.dtype)
    v_ref[...] = kv[:, hw:].astype(v_ref.dtype)


def _mem_kv(mem2, g, wkv, k_gain, *, nb, mlen, heads):
    d = mem2.shape[1]
    hw = heads * HEAD_DIM
    out = pl.BlockSpec((mlen, hw), lambda b: (b, 0))
    shp = jax.ShapeDtypeStruct((nb * mlen, hw), BF16)
    return _pcall(
        functools.partial(_mem_kv_body, heads=heads),
        grid=(nb,),
        in_specs=[pl.BlockSpec((mlen, d), lambda b: (b, 0)),
                  pl.BlockSpec((1, d), lambda b: (0, 0)),
                  pl.BlockSpec((d, 2 * hw), lambda b: (0, 0)),
                  pl.BlockSpec((1, HEAD_DIM), lambda b: (0, 0))],
        out_specs=[out, out],
        out_shape=[shp, shp],
        compiler_params=_cparams("parallel"),
    )(mem2, g.reshape(1, d), wkv, k_gain.reshape(1, HEAD_DIM))


def _xattn_body(x_ref, g_ref, wq_ref, k_ref, v_ref, qg_ref, wo_ref, o_ref, *, heads):
    x = x_ref[...]
    q = _dot(_rms(x, g_ref[...]).astype(BF16), wq_ref[...])
    scale = HEAD_DIM ** -0.5
    outs = []
    for h in range(heads):
        sl = slice(h * HEAD_DIM, (h + 1) * HEAD_DIM)
        qh = (_rms(q[:, sl], qg_ref[...]) * scale).astype(BF16)
        s = _dot_nt(qh, k_ref[:, sl])
        p = jnp.exp(s - jnp.max(s, axis=-1, keepdims=True))
        l = jnp.sum(p, axis=-1, keepdims=True)
        outs.append(_dot(p.astype(BF16), v_ref[:, sl]) / l)
    o = jnp.concatenate(outs, axis=-1).astype(BF16)
    o_ref[...] = x + _dot(o, wo_ref[...])


def _cross_attention(x2, g, wq, k, v, q_gain, wo, *, nb, seq, mlen, heads, tm):
    d = x2.shape[1]
    hw = heads * HEAD_DIM
    nt = seq // tm
    xspec = pl.BlockSpec((tm, d), lambda b, t: (b * nt + t, 0))
    return _pcall(
        functools.partial(_xattn_body, heads=heads),
        grid=(nb, nt),
        in_specs=[xspec,
                  pl.BlockSpec((1, d), lambda b, t: (0, 0)),
                  pl.BlockSpec((d, hw), lambda b, t: (0, 0)),
                  pl.BlockSpec((mlen, hw), lambda b, t: (b, 0)),
                  pl.BlockSpec((mlen, hw), lambda b, t: (b, 0)),
                  pl.BlockSpec((1, HEAD_DIM), lambda b, t: (0, 0)),
                  pl.BlockSpec((hw, d), lambda b, t: (0, 0))],
        out_specs=xspec,
        out_shape=jax.ShapeDtypeStruct(x2.shape, F32),
        compiler_params=_cparams("parallel", "arbitrary"),
    )(x2, g.reshape(1, d), wq, k, v, q_gain.reshape(1, HEAD_DIM), wo)


def _router_body(x_ref, g_ref, w_ref, o_ref, *, n_experts):
    logits = _dot(_rms(x_ref[...], g_ref[...]), w_ref[...], HIGHEST)
    lane = lax.broadcasted_iota(jnp.int32, logits.shape, 1)
    lg = jnp.where(lane < n_experts, logits, -jnp.inf)
    m1 = jnp.max(lg, axis=-1, keepdims=True)
    i1 = jnp.min(jnp.where(lg == m1, lane, LANES), axis=-1, keepdims=True)
    lg2 = jnp.where(lane == i1, -jnp.inf, lg)
    m2 = jnp.max(lg2, axis=-1, keepdims=True)
    i2 = jnp.min(jnp.where(lg2 == m2, lane, LANES), axis=-1, keepdims=True)
    e2 = jnp.exp(m2 - m1)
    den = 1.0 + e2
    o_ref[...] = (jnp.where(lane == 0, 1.0 / den, 0.0) + jnp.where(lane == 1, e2 / den, 0.0)
                  + jnp.where(lane == 2, i1.astype(F32), 0.0) + jnp.where(lane == 3, i2.astype(F32), 0.0))


def _router(x2, g, w_router, *, tm):
    m, d = x2.shape
    ne = w_router.shape[1]
    wpad = jnp.zeros((d, LANES), F32).at[:, :ne].set(w_router)
    return _pcall(
        functools.partial(_router_body, n_experts=ne),
        grid=(m // tm,),
        in_specs=[pl.BlockSpec((tm, d), lambda i: (i, 0)),
                  pl.BlockSpec((1, d), lambda i: (0, 0)),
                  pl.BlockSpec((d, LANES), lambda i: (0, 0))],
        out_specs=pl.BlockSpec((tm, LANES), lambda i: (i, 0)),
        out_shape=jax.ShapeDtypeStruct((m, LANES), F32),
        compiler_params=_cparams("parallel"),
    )(x2, g.reshape(1, d), wpad)


def _rope_tables(seq):
    half = MLA_ROPE // 2
    inv_freq = ROPE_BASE ** (-jnp.arange(half, dtype=F32) / half)
    ang = jnp.arange(seq, dtype=jnp.int32).astype(F32)[:, None] * inv_freq[None, :]
    cos, sin = jnp.cos(ang), jnp.sin(ang)
    z = jnp.zeros((seq, half), F32)
    cos_t = jnp.concatenate([cos, cos, z, z], axis=-1)
    sin_a = jnp.concatenate([-sin, z, z, z], axis=-1)
    sin_b = jnp.concatenate([z, sin, z, z], axis=-1)
    return cos_t, sin_a, sin_b


def kernel(x, mem, mix_norm, w_in, rg_conv_w, rg_conv_b, rg_w_a, rg_b_a, rg_w_x, rg_b_x, rg_lambda, sb_q_gain, sb_k_gain, mla_cq_gain, mla_w_uq, mla_ckv_gain, mla_w_ukv, mla_q_gain, mla_k_gain, dn_conv_w, dn_a_log, dn_dt_bias, dn_out_gain, w_branch, w_out, xa_norm, mem_norm, xa_wq, xa_wkv, xa_q_gain, xa_k_gain, xa_wo, ffn_norm, ffn_w_gu, ffn_w_down, moe_router, moe_w_gu, moe_w_down):
    nb, seq, d = x.shape
    depth = w_in.shape[0]
    mlen = mem.shape[1]
    bw = rg_conv_w.shape[2]
    heads = bw // HEAD_DIM
    q_rank = mla_cq_gain.shape[1]
    kv_rank = mla_ckv_gain.shape[1]
    xa_heads = xa_wq.shape[2] // HEAD_DIM
    t_tok = nb * seq
    assert bw % HEAD_DIM == 0 and seq % 512 == 0 and d % 512 == 0
    assert mla_q_gain.shape[1] == HEAD_DIM + MLA_ROPE

    src_widths = (("rg", 2 * bw), ("sb", 3 * bw), ("cq", q_rank), ("ckv", kv_rank), ("kr", MLA_ROPE),
                  ("dnqkv", 3 * bw), ("dnz", bw), ("bd", 2 * heads), ("gate", 4 * d))
    src, s_pos = {}, 0
    for name, width in src_widths:
        src[name] = (s_pos, s_pos + width)
        s_pos += width
    assert s_pos == w_in.shape[2] and 2 * heads <= LANES
    packed = (("rg", bw), ("sb", _pick(heads, (4, 2, 1)) * HEAD_DIM), ("cq", q_rank), ("ckv", kv_rank),
              ("kr", LANES), ("bd", LANES), ("dnqkv", 3 * bw), ("dnz", bw), ("gate", d))
    off, pieces, p_pos = {}, [], 0
    for name, align in packed:
        width = src[name][1] - src[name][0]
        start = -(-p_pos // align) * align
        pieces.append((start - p_pos, src[name]))
        off[name] = start
        p_pos = start + width
        if name in ("kr", "bd"):
            pieces.append((LANES - width, None))
            p_pos += LANES - width
    n_proj = -(-p_pos // 512) * 512
    pieces.append((n_proj - p_pos, None))

    x2 = x.reshape(t_tok, d)
    mem2 = mem.reshape(nb * mlen, d)
    tabs = _rope_tables(seq)
    mla_scale = (HEAD_DIM + MLA_ROPE) ** -0.5

    tm_big = _pick(t_tok, (1024, 512, 256, 128))
    tm_mid = _pick(t_tok, (512, 256, 128))
    ts_seq = _pick(seq, (512, 256, 128))

    for layer in range(depth):
        wi = w_in[layer]
        zc = lambda n: jnp.zeros((d, n), wi.dtype)
        cols = []
        for gap, rng in pieces:
            if gap:
                cols.append(zc(gap))
            if rng is not None:
                cols.append(wi[:, rng[0]:rng[1]])
        w_proj = jnp.concatenate(cols, axis=1).astype(BF16)
        wq3 = mla_w_uq[layer].reshape(q_rank, heads, HEAD_DIM + MLA_ROPE)
        w_q = jnp.concatenate([wq3, jnp.zeros((q_rank, heads, HEAD_DIM - MLA_ROPE), F32)],
                              axis=-1).reshape(q_rank, heads * 2 * HEAD_DIM).astype(BF16)
        qg = mla_q_gain[layer] * mla_scale
        q_gain2 = jnp.concatenate([qg, jnp.zeros((HEAD_DIM - MLA_ROPE,), F32)]).reshape(1, 2 * HEAD_DIM)
        kgl = mla_k_gain[layer]
        k_gain_n = kgl[:HEAD_DIM].reshape(1, HEAD_DIM)
        k_gain_r = jnp.concatenate([kgl[HEAD_DIM:], jnp.zeros((LANES - MLA_ROPE,), F32)]).reshape(1, LANES)

        proj = _norm_matmul(x2, mix_norm[layer], w_proj, out_dtype=BF16, tm=tm_big,
                            tn=_pick(n_proj, (512, 256, 128)))
        proj3 = proj.reshape(nb, seq, n_proj)

        y_a = _rglru(proj, nb, seq, rg_conv_w[layer], rg_conv_b[layer], rg_w_a[layer], rg_b_a[layer],
                     rg_w_x[layer], rg_b_x[layer], rg_lambda[layer], ts=_pick(seq, (256, 128)))

        y_b = _sb_attention(proj3, sb_q_gain[layer], sb_k_gain[layer], heads=heads,
                            q_off=off["sb"], tq=_pick(seq, (256, 128)))

        mq = _mla_q(proj, mla_cq_gain[layer], w_q, q_gain2, tabs, heads=heads,
                    cq_blk=off["cq"] // q_rank, nb=nb, seq=seq, tm=ts_seq)
        mk, mv = _mla_kv(proj, mla_ckv_gain[layer], mla_w_ukv[layer].astype(BF16), k_gain_n,
                         k_gain_r, tabs, heads=heads, ckv_blk=off["ckv"] // kv_rank,
                         kr_blk=off["kr"] // LANES, nb=nb, seq=seq, tm=ts_seq)
        r3 = lambda a: a.reshape(nb, seq, a.shape[-1])
        y_c = _mla_attention(r3(mq), r3(mk), r3(mv), heads=heads, tq=ts_seq, tk=ts_seq)

        y_d = _delta_rule(proj, dn_conv_w[layer], dn_a_log[layer], dn_dt_bias[layer], dn_out_gain[layer],
                          heads=heads, qkv_blk=off["dnqkv"] // (3 * bw), bd_blk=off["bd"] // LANES,
                          z_blk=off["dnz"] // bw, nb=nb, seq=seq)

        merged = _gated_merge([y_a, y_b.reshape(t_tok, bw), y_c.reshape(t_tok, bw), y_d], proj,
                              w_branch[layer].astype(BF16), gate_off=off["gate"], tm=tm_big,
                              tn=_pick(d, (512, 256, 128)))
        x2 = _matmul_residual(merged, w_out[layer].astype(BF16), x2, tm=tm_big,
                              tn=_pick(d, (1024, 512, 256, 128)), tk=d)

        mk, mv = _mem_kv(mem2, mem_norm[layer], xa_wkv[layer].astype(BF16), xa_k_gain[layer],
                         nb=nb, mlen=mlen, heads=xa_heads)
        x2 = _cross_attention(x2, xa_norm[layer], xa_wq[layer].astype(BF16), mk, mv, xa_q_gain[layer],
                              xa_wo[layer].astype(BF16), nb=nb, seq=seq, mlen=mlen, heads=xa_heads,
                              tm=ts_seq)

        if layer % 2 == 0:
            wgu = ffn_w_gu[layer // 2].astype(BF16)
            f = wgu.shape[1] // 2
            hmid = _norm_swiglu(x2, ffn_norm[layer], wgu, tm=tm_big, tn=_pick(f, (512, 256, 128)))
            x2 = _matmul_residual(hmid, ffn_w_down[layer // 2].astype(BF16), x2, tm=tm_big,
                                  tn=_pick(d, (512, 256, 128)), tk=f)
        else:
            wgu = moe_w_gu[layer // 2].astype(BF16)
            ne, _, f2 = wgu.shape
            f = f2 // 2
            rt = _router(x2, ffn_norm[layer], moe_router[layer // 2], tm=tm_mid)
            pos, tile_expert, n_used, p_rows = _moe_plan(rt, ne, tm_mid)
            tm_c = _pick(t_tok, (256, 128))
            xg = _moe_scatter(x2, pos.reshape(t_tok // tm_mid, 1, TOP_K * tm_mid), p_rows, tm=tm_mid)
            hmid = _moe_up(xg, ffn_norm[layer], wgu, tile_expert, n_used, tm=tm_mid,
                           tn=_pick(f, (1408, 512, 256, 128)))
            yg = _moe_down(hmid, moe_w_down[layer // 2].astype(BF16), tile_expert, n_used, tm=tm_mid)
            x2 = _moe_combine(x2, rt, pos.reshape(t_tok // tm_c, 1, TOP_K * tm_c), yg, tm=tm_c)
    return x2.reshape(nb, seq, d)
```

```python
import functools
import math

import jax
import jax.numpy as jnp
import numpy as np
from jax import lax
from jax.experimental import pallas as pl
from jax.experimental.pallas import tpu as pltpu

F32 = jnp.float32
BF16 = jnp.bfloat16
HIGHEST = lax.Precision.HIGHEST

HEAD_DIM = 128
LANES = 128
MLA_ROPE = 64
RG_C = 8.0
ROPE_BASE = 10000.0
NORM_EPS = 1e-6
NEG_INF = -1e30
EXP_UNDERFLOW = -104.0
DN_CHUNK = 128
TOP_K = 2
VMEM_LIMIT = 56 * 1024 * 1024


def _cparams(*sem):
    return pltpu.CompilerParams(dimension_semantics=sem, vmem_limit_bytes=VMEM_LIMIT)


def _pcall(body, **kwargs):
    fn = getattr(body, "func", body)
    return pl.pallas_call(body, name=fn.__name__.strip("_").replace("_body", ""), **kwargs)


def _pick(n, prefs):
    for p in prefs:
        if p <= n and n % p == 0:
            return p
    return n


def _rms(x, g):
    return x * lax.rsqrt(jnp.mean(x * x, axis=-1, keepdims=True) + NORM_EPS) * g


def _sigmoid(x):
    return 1.0 / (1.0 + jnp.exp(-x))


def _softplus(x):
    return jnp.maximum(x, 0.0) + jnp.log(1.0 + jnp.exp(-jnp.abs(x)))


def _dot(a, b, precision=None):
    return jnp.dot(a, b, preferred_element_type=F32, precision=precision)


def _dot_nt(a, b, precision=None):
    return lax.dot_general(a, b, (((1,), (1,)), ((), ())), preferred_element_type=F32,
                           precision=precision)


def _norm_mm_body(x_ref, g_ref, w_ref, o_ref, xn_ref, *, precision):
    @pl.when(pl.program_id(1) == 0)
    def _():
        xn_ref[...] = _rms(x_ref[...].astype(F32), g_ref[...]).astype(xn_ref.dtype)

    o_ref[...] = _dot(xn_ref[...], w_ref[...], precision).astype(o_ref.dtype)


def _norm_matmul(x, g, w, *, layer, out_dtype, tm, tn, precision=None):
    m, k = x.shape
    n = w.shape[2]
    return _pcall(
        functools.partial(_norm_mm_body, precision=precision),
        grid=(m // tm, n // tn),
        in_specs=[pl.BlockSpec((tm, k), lambda i, j: (i, 0)),
                  pl.BlockSpec((1, k), lambda i, j: (0, 0)),
                  pl.BlockSpec((None, k, tn), lambda i, j: (layer, 0, j))],
        out_specs=pl.BlockSpec((tm, tn), lambda i, j: (i, j)),
        out_shape=jax.ShapeDtypeStruct((m, n), out_dtype),
        scratch_shapes=[pltpu.VMEM((tm, k), w.dtype)],
        compiler_params=_cparams("parallel", "arbitrary"),
    )(x, g.reshape(1, k), w)


def _swiglu_body(x_ref, g_ref, wg_ref, wu_ref, o_ref, xn_ref):
    @pl.when(pl.program_id(1) == 0)
    def _():
        xn_ref[...] = _rms(x_ref[...], g_ref[...]).astype(BF16)

    xn = xn_ref[...]
    a = _dot(xn, wg_ref[...])
    b = _dot(xn, wu_ref[...])
    o_ref[...] = (a * _sigmoid(a) * b).astype(o_ref.dtype)


def _norm_swiglu(x, g, w_gu, *, tm, tn):
    m, k = x.shape
    f = w_gu.shape[1] // 2
    nj = f // tn
    return _pcall(
        _swiglu_body,
        grid=(m // tm, nj),
        in_specs=[pl.BlockSpec((tm, k), lambda i, j: (i, 0)),
                  pl.BlockSpec((1, k), lambda i, j: (0, 0)),
                  pl.BlockSpec((k, tn), lambda i, j: (0, j)),
                  pl.BlockSpec((k, tn), lambda i, j: (0, j + nj))],
        out_specs=pl.BlockSpec((tm, tn), lambda i, j: (i, j)),
        out_shape=jax.ShapeDtypeStruct((m, f), BF16),
        scratch_shapes=[pltpu.VMEM((tm, k), BF16)],
        compiler_params=_cparams("parallel", "arbitrary"),
    )(x, g.reshape(1, k), w_gu, w_gu)


def _row_copies(pos_ref, n_rows, make_copy, make_bulk):
    def issue(r, c):
        for k in range(TOP_K):
            make_copy(r, k, pos_ref[0, TOP_K * r + k]).start()
        return c

    lax.fori_loop(0, n_rows, issue, 0, unroll=4)
    for _ in range(TOP_K):
        make_bulk().wait()


def _moe_scatter_body(pos_ref, x_ref, xg_init_hbm, xg_hbm, sem, *, tm):
    del xg_init_hbm
    _row_copies(pos_ref, tm,
                lambda r, k, p: pltpu.make_async_copy(x_ref.at[pl.ds(r, 1)], xg_hbm.at[pl.ds(p, 1)], sem),
                lambda: pltpu.make_async_copy(x_ref, xg_hbm.at[pl.ds(0, tm)], sem))


def _moe_scatter(x2, pos3, n_rows, *, tm):
    t_tok, d = x2.shape
    return _pcall(
        functools.partial(_moe_scatter_body, tm=tm),
        grid=(t_tok // tm,),
        in_specs=[pl.BlockSpec((None, 1, TOP_K * tm), lambda i: (i, 0, 0), memory_space=pltpu.SMEM),
                  pl.BlockSpec((tm, d), lambda i: (i, 0)),
                  pl.BlockSpec(memory_space=pl.ANY)],
        out_specs=pl.BlockSpec(memory_space=pl.ANY),
        out_shape=jax.ShapeDtypeStruct((n_rows, d), x2.dtype),
        scratch_shapes=[pltpu.SemaphoreType.DMA(())],
        input_output_aliases={2: 0},
        compiler_params=_cparams("arbitrary"),
    )(pos3, x2, jnp.zeros((n_rows, d), x2.dtype))


def _moe_up_body(te_ref, nu_ref, xg_ref, g_ref, wg_ref, wu_ref, o_ref):
    del te_ref
    used = pl.program_id(1) < nu_ref[0]

    @pl.when(used)
    def _():
        xn = _rms(xg_ref[...], g_ref[...]).astype(BF16)
        a = _dot(xn, wg_ref[...])
        b = _dot(xn, wu_ref[...])
        o_ref[...] = (a * _sigmoid(a) * b).astype(o_ref.dtype)

    @pl.when(jnp.logical_not(used))
    def _():
        o_ref[...] = jnp.zeros_like(o_ref)


def _moe_up(xg, g, w_gu, tile_expert, n_used, *, tm, tn):
    p_rows, d = xg.shape
    f = w_gu.shape[2] // 2
    nj = f // tn
    grid_spec = pltpu.PrefetchScalarGridSpec(
        num_scalar_prefetch=2, grid=(nj, p_rows // tm),
        in_specs=[pl.BlockSpec((tm, d), lambda j, i, te, nu: (i, 0)),
                  pl.BlockSpec((1, d), lambda j, i, te, nu: (0, 0)),
                  pl.BlockSpec((None, d, tn), lambda j, i, te, nu: (te[i], 0, j)),
                  pl.BlockSpec((None, d, tn), lambda j, i, te, nu: (te[i], 0, j + nj))],
        out_specs=pl.BlockSpec((tm, tn), lambda j, i, te, nu: (i, j)))
    return _pcall(
        _moe_up_body, grid_spec=grid_spec,
        out_shape=jax.ShapeDtypeStruct((p_rows, f), BF16),
        compiler_params=_cparams("arbitrary", "arbitrary"),
    )(tile_expert, n_used, xg, g.reshape(1, d), w_gu, w_gu)


def _moe_down_body(te_ref, nu_ref, h_ref, w_ref, o_ref):
    del te_ref
    used = pl.program_id(0) < nu_ref[0]

    @pl.when(used)
    def _():
        o_ref[...] = _dot(h_ref[...], w_ref[...])

    @pl.when(jnp.logical_not(used))
    def _():
        o_ref[...] = jnp.zeros_like(o_ref)


def _moe_down(hmid, w_down, tile_expert, n_used, *, tm):
    p_rows, f = hmid.shape
    d = w_down.shape[2]
    grid_spec = pltpu.PrefetchScalarGridSpec(
        num_scalar_prefetch=2, grid=(p_rows // tm,),
        in_specs=[pl.BlockSpec((tm, f), lambda i, te, nu: (i, 0)),
                  pl.BlockSpec((None, f, d), lambda i, te, nu: (te[i], 0, 0))],
        out_specs=pl.BlockSpec((tm, d), lambda i, te, nu: (i, 0)))
    return _pcall(
        _moe_down_body, grid_spec=grid_spec,
        out_shape=jax.ShapeDtypeStruct((p_rows, d), F32),
        compiler_params=_cparams("arbitrary"),
    )(tile_expert, n_used, hmid, w_down)


def _moe_combine_body(pos_ref, rt_ref, x_ref, y_hbm, o_ref, buf, sem, *, tm):
    _row_copies(pos_ref, tm,
                lambda r, k, p: pltpu.make_async_copy(y_hbm.at[pl.ds(p, 1)], buf.at[k, pl.ds(r, 1)], sem),
                lambda: pltpu.make_async_copy(y_hbm.at[pl.ds(0, tm)], buf.at[0], sem))
    w = rt_ref[...]
    o_ref[...] = x_ref[...] + w[:, 0:1] * buf[0] + w[:, 1:2] * buf[1]


def _moe_combine(x2, rt, pos3, y, *, tm):
    t_tok, d = x2.shape
    return _pcall(
        functools.partial(_moe_combine_body, tm=tm),
        grid=(t_tok // tm,),
        in_specs=[pl.BlockSpec((None, 1, TOP_K * tm), lambda i: (i, 0, 0), memory_space=pltpu.SMEM),
                  pl.BlockSpec((tm, LANES), lambda i: (i, 0)),
                  pl.BlockSpec((tm, d), lambda i: (i, 0)),
                  pl.BlockSpec(memory_space=pl.ANY)],
        out_specs=pl.BlockSpec((tm, d), lambda i: (i, 0)),
        out_shape=jax.ShapeDtypeStruct((t_tok, d), F32),
        scratch_shapes=[pltpu.VMEM((TOP_K, tm, d), F32), pltpu.SemaphoreType.DMA(())],
        compiler_params=_cparams("arbitrary"),
    )(pos3, rt, x2, y)


def _moe_plan(rt, n_experts, tm):
    t_tok = rt.shape[0]
    e_flat = rt[:, 2:2 + TOP_K].astype(jnp.int32).reshape(-1)
    onehot = (e_flat[:, None] == jnp.arange(n_experts, dtype=jnp.int32)[None, :]).astype(jnp.int32)
    csum = jnp.cumsum(onehot, axis=0)
    padded = -(-csum[-1] // tm) * tm
    gend = jnp.cumsum(padded)
    pos = jnp.sum(onehot * (csum - 1 + (gend - padded)[None, :]), axis=1)
    n_tiles = (TOP_K * t_tok) // tm + n_experts
    tile_start = jnp.arange(n_tiles, dtype=jnp.int32) * tm
    tile_expert = jnp.minimum(jnp.sum((tile_start[:, None] >= gend[None, :]).astype(jnp.int32), axis=1),
                              n_experts - 1)
    return pos, tile_expert, (gend[-1:] // tm).astype(jnp.int32), n_tiles * tm


def _mm_res_body(a_ref, w_ref, r_ref, o_ref, *scratch, nk):
    if nk == 1:
        o_ref[...] = r_ref[...] + _dot(a_ref[...], w_ref[...])
        return
    acc_ref, = scratch
    kk = pl.program_id(2)

    @pl.when(kk == 0)
    def _():
        acc_ref[...] = jnp.zeros_like(acc_ref)

    acc_ref[...] += _dot(a_ref[...], w_ref[...])

    @pl.when(kk == nk - 1)
    def _():
        o_ref[...] = r_ref[...] + acc_ref[...]


def _matmul_residual(a, w, r, *, tm, tn, tk):
    m, k = a.shape
    n = w.shape[1]
    nk = k // tk
    return _pcall(
        functools.partial(_mm_res_body, nk=nk),
        grid=(m // tm, n // tn, nk),
        in_specs=[pl.BlockSpec((tm, tk), lambda i, j, kk: (i, kk)),
                  pl.BlockSpec((tk, tn), lambda i, j, kk: (kk, j)),
                  pl.BlockSpec((tm, tn), lambda i, j, kk: (i, j))],
        out_specs=pl.BlockSpec((tm, tn), lambda i, j, kk: (i, j)),
        out_shape=jax.ShapeDtypeStruct((m, n), F32),
        scratch_shapes=[pltpu.VMEM((tm, tn), F32)] if nk > 1 else [],
        compiler_params=_cparams("parallel", "parallel", "arbitrary"),
    )(a, w, r)


def _causal_conv(xbuf, x, cw, ts):
    kk = cw.shape[0]
    xbuf[8:8 + ts, :] = x
    y = cw[kk - 1:kk, :] * x
    for d in range(1, kk):
        y = y + cw[kk - 1 - d:kk - d, :] * xbuf[8 - d:8 - d + ts, :]
    xbuf[0:8, :] = xbuf[ts:ts + 8, :]
    return y


def _rglru_body(x_ref, gate_ref, cw_ref, cb_ref, wa_ref, ba_ref, wx_ref, bx_ref, lam_ref, o_ref,
                xbuf, hcar, *, ts, width):
    @pl.when(pl.program_id(1) == 0)
    def _():
        xbuf[0:8, :] = jnp.zeros((8, width), F32)
        hcar[...] = jnp.zeros_like(hcar)

    u = _causal_conv(xbuf, x_ref[...].astype(F32), cw_ref[...], ts) + cb_ref[...]
    ub = u.astype(BF16)
    rs, is_ = [], []
    for g in range(width // HEAD_DIM):
        blk = ub[:, g * HEAD_DIM:(g + 1) * HEAD_DIM]
        rs.append(_dot(blk, wa_ref[g]))
        is_.append(_dot(blk, wx_ref[g]))
    r = _sigmoid(jnp.concatenate(rs, axis=-1) + ba_ref[...])
    ig = _sigmoid(jnp.concatenate(is_, axis=-1) + bx_ref[...])
    log_a = (-RG_C) * r * _softplus(-lam_ref[...])
    a = jnp.exp(log_a)
    b = jnp.sqrt(1.0 - jnp.exp(2.0 * log_a)) * (ig * u)
    rows = lax.broadcasted_iota(jnp.int32, (ts, 1), 0)
    d = 1
    while d < ts:
        keep = rows >= d
        b = jnp.where(keep, a * pltpu.roll(b, d, 0) + b, b)
        a = jnp.where(keep, a * pltpu.roll(a, d, 0), a)
        d *= 2
    h = a * hcar[0:1, :] + b
    hcar[0:1, :] = h[ts - 1:ts, :]
    gt = gate_ref[...].astype(F32)
    gelu = 0.5 * gt * (1.0 + jnp.tanh(math.sqrt(2.0 / math.pi) * (gt + 0.044715 * (gt * gt * gt))))
    o_ref[...] = (h * gelu).astype(o_ref.dtype)


def _rglru(proj, nb, seq, conv_w, conv_b, w_a, b_a, w_x, b_x, lam, *, ts):
    width = conv_w.shape[1]
    nt = seq // ts
    row = lambda v: v.reshape(1, width)
    full2 = lambda shp: pl.BlockSpec(shp, lambda b, t: (0,) * len(shp))
    return _pcall(
        functools.partial(_rglru_body, ts=ts, width=width),
        grid=(nb, nt),
        in_specs=[pl.BlockSpec((ts, width), lambda b, t: (b * nt + t, 0)),
                  pl.BlockSpec((ts, width), lambda b, t: (b * nt + t, 1)),
                  full2(conv_w.shape), full2((1, width)), full2(w_a.shape), full2((1, width)),
                  full2(w_x.shape), full2((1, width)), full2((1, width))],
        out_specs=pl.BlockSpec((ts, width), lambda b, t: (b * nt + t, 0)),
        out_shape=jax.ShapeDtypeStruct((nb * seq, width), BF16),
        scratch_shapes=[pltpu.VMEM((ts + 8, width), F32), pltpu.VMEM((8, width), F32)],
        compiler_params=_cparams("parallel", "arbitrary"),
    )(proj, proj, conv_w, row(conv_b), w_a.astype(BF16), row(b_a), w_x.astype(BF16), row(b_x), row(lam))


def _sb_body(q_ref, k_ref, v_ref, qg_ref, kg_ref, o_ref, kn_ref, *, tq, scale, hp):
    qi = pl.program_id(2)
    hs = range(hp)
    sl = lambda h: slice(h * HEAD_DIM, (h + 1) * HEAD_DIM)

    @pl.when(qi == 0)
    def _():
        for h in hs:
            kn_ref[:, sl(h)] = _rms(k_ref[:, sl(h)].astype(F32), kg_ref[...]).astype(BF16)

    qn = [(_rms(q_ref[:, sl(h)].astype(F32), qg_ref[...]) * scale).astype(BF16) for h in hs]
    rr = lax.broadcasted_iota(jnp.int32, (tq, tq), 0)
    cc = lax.broadcasted_iota(jnp.int32, (tq, tq), 1)
    upper = (rr > cc).astype(BF16)

    def block(j, accs, runs, diag):
        ks = pl.multiple_of(j * tq, tq)
        z = [_dot_nt(qn[h], kn_ref[pl.ds(ks, tq), sl(h)]) for h in hs]
        sp = [_softplus(z[h]) for h in hs]
        if diag:
            vis = cc < rr
            lk = [jnp.where(vis, -sp[h], 0.0) for h in hs]
        else:
            lk = [-sp[h] for h in hs]
        hi = [lk[h].astype(BF16) for h in hs]
        lo = [(lk[h] - hi[h].astype(F32)).astype(BF16) for h in hs]
        cs = [_dot(hi[h], upper) + _dot(lo[h], upper) for h in hs]
        w = [jnp.exp(z[h] - sp[h] + cs[h] + runs[h]) for h in hs]
        if diag:
            w = [jnp.where(vis, w[h], 0.0) for h in hs]
        accs = tuple(accs[h] + _dot(w[h].astype(BF16), v_ref[pl.ds(ks, tq), sl(h)]) for h in hs)
        runs = tuple(runs[h] + jnp.sum(lk[h], axis=-1, keepdims=True) for h in hs)
        return accs, runs

    accs, runs = block(qi, tuple(jnp.zeros((tq, HEAD_DIM), F32) for _ in hs),
                       tuple(jnp.zeros((tq, 1), F32) for _ in hs), True)

    def more(c):
        return (c[0] >= 0) & (jnp.max(functools.reduce(jnp.maximum, c[2])) > EXP_UNDERFLOW)

    def step(c):
        a, r = block(c[0], c[1], c[2], False)
        return c[0] - 1, a, r

    _, accs, _ = lax.while_loop(more, step, (qi - 1, accs, runs))
    for h in hs:
        o_ref[:, sl(h)] = accs[h].astype(o_ref.dtype)


def _sb_attention(proj3, q_gain, k_gain, *, heads, q_off, tq):
    nb, seq, _ = proj3.shape
    hp = _pick(heads, (4, 2, 1))
    wblk = hp * HEAD_DIM
    bw = heads * HEAD_DIM
    assert q_off % wblk == 0 and bw % wblk == 0
    qb, kb, vb = q_off // wblk, (q_off + bw) // wblk, (q_off + 2 * bw) // wblk
    return _pcall(
        functools.partial(_sb_body, tq=tq, scale=HEAD_DIM ** -0.5, hp=hp),
        grid=(nb, heads // hp, seq // tq),
        in_specs=[pl.BlockSpec((None, tq, wblk), lambda b, h, i: (b, i, qb + h)),
                  pl.BlockSpec((None, seq, wblk), lambda b, h, i: (b, 0, kb + h)),
                  pl.BlockSpec((None, seq, wblk), lambda b, h, i: (b, 0, vb + h)),
                  pl.BlockSpec((1, HEAD_DIM), lambda b, h, i: (0, 0)),
                  pl.BlockSpec((1, HEAD_DIM), lambda b, h, i: (0, 0))],
        out_specs=pl.BlockSpec((None, tq, wblk), lambda b, h, i: (b, i, h)),
        out_shape=jax.ShapeDtypeStruct((nb, seq, bw), BF16),
        scratch_shapes=[pltpu.VMEM((seq, wblk), BF16)],
        compiler_params=_cparams("parallel", "parallel", "arbitrary"),
    )(proj3, proj3, proj3, q_gain.reshape(1, HEAD_DIM), k_gain.reshape(1, HEAD_DIM))


def _rope_pad(r, gain, cos_t, sin_a, sin_b):
    ms = jnp.sum(r * r, axis=-1, keepdims=True) * (1.0 / MLA_ROPE)
    rn = r * lax.rsqrt(ms + NORM_EPS) * gain
    return rn * cos_t + pltpu.roll(rn, LANES - MLA_ROPE // 2, 1) * sin_a + pltpu.roll(rn, MLA_ROPE // 2, 1) * sin_b


def _mla_q_body(cq_ref, g_ref, w_ref, qg_ref, cos_ref, sa_ref, sb_ref, o_ref, *, heads):
    q = _dot(_rms(cq_ref[...].astype(F32), g_ref[...]).astype(BF16), w_ref[...])
    qg = qg_ref[...]
    for h in range(heads):
        c0 = h * 2 * HEAD_DIM
        o_ref[:, c0:c0 + HEAD_DIM] = _rms(q[:, c0:c0 + HEAD_DIM], qg[:, :HEAD_DIM]).astype(o_ref.dtype)
        o_ref[:, c0 + HEAD_DIM:c0 + 2 * HEAD_DIM] = _rope_pad(
            q[:, c0 + HEAD_DIM:c0 + 2 * HEAD_DIM], qg[:, HEAD_DIM:], cos_ref[...], sa_ref[...],
            sb_ref[...]).astype(o_ref.dtype)


def _mla_q(proj, cq_gain, w_q, q_gain2, tabs, *, heads, cq_blk, nb, seq, tm):
    rank = w_q.shape[0]
    nt = seq // tm
    wide = heads * 2 * HEAD_DIM
    tab = pl.BlockSpec((tm, LANES), lambda i: (i % nt, 0))
    return _pcall(
        functools.partial(_mla_q_body, heads=heads),
        grid=(nb * nt,),
        in_specs=[pl.BlockSpec((tm, rank), lambda i: (i, cq_blk)),
                  pl.BlockSpec((1, rank), lambda i: (0, 0)),
                  pl.BlockSpec((rank, wide), lambda i: (0, 0)),
                  pl.BlockSpec((1, 2 * HEAD_DIM), lambda i: (0, 0)),
                  tab, tab, tab],
        out_specs=pl.BlockSpec((tm, wide), lambda i: (i, 0)),
        out_shape=jax.ShapeDtypeStruct((nb * seq, wide), BF16),
        compiler_params=_cparams("parallel"),
    )(proj, cq_gain.reshape(1, rank), w_q, q_gain2, *tabs)


def _mla_kv_body(ckv_ref, g_ref, w_ref, kg_ref, kr_ref, krg_ref, cos_ref, sa_ref, sb_ref,
                 k_ref, v_ref, *, heads):
    kv = _dot(_rms(ckv_ref[...].astype(F32), g_ref[...]).astype(BF16), w_ref[...])
    kro = _rope_pad(kr_ref[...].astype(F32), krg_ref[...], cos_ref[...], sa_ref[...],
                    sb_ref[...]).astype(k_ref.dtype)
    for h in range(heads):
        c0 = h * 2 * HEAD_DIM
        k_ref[:, c0:c0 + HEAD_DIM] = _rms(kv[:, c0:c0 + HEAD_DIM], kg_ref[...]).astype(k_ref.dtype)
        k_ref[:, c0 + HEAD_DIM:c0 + 2 * HEAD_DIM] = kro
        v_ref[:, h * HEAD_DIM:(h + 1) * HEAD_DIM] = kv[:, c0 + HEAD_DIM:c0 + 2 * HEAD_DIM].astype(v_ref.dtype)


def _mla_kv(proj, ckv_gain, w_kv, k_gain_n, k_gain_r, tabs, *, heads, ckv_blk, kr_blk, nb, seq, tm):
    rank = w_kv.shape[0]
    nt = seq // tm
    wide = heads * 2 * HEAD_DIM
    tab = pl.BlockSpec((tm, LANES), lambda i: (i % nt, 0))
    return _pcall(
        functools.partial(_mla_kv_body, heads=heads),
        grid=(nb * nt,),
        in_specs=[pl.BlockSpec((tm, rank), lambda i: (i, ckv_blk)),
                  pl.BlockSpec((1, rank), lambda i: (0, 0)),
                  pl.BlockSpec((rank, wide), lambda i: (0, 0)),
                  pl.BlockSpec((1, HEAD_DIM), lambda i: (0, 0)),
                  pl.BlockSpec((tm, LANES), lambda i: (i, kr_blk)),
                  pl.BlockSpec((1, LANES), lambda i: (0, 0)),
                  tab, tab, tab],
        out_specs=[pl.BlockSpec((tm, wide), lambda i: (i, 0)),
                   pl.BlockSpec((tm, heads * HEAD_DIM), lambda i: (i, 0))],
        out_shape=[jax.ShapeDtypeStruct((nb * seq, wide), BF16),
                   jax.ShapeDtypeStruct((nb * seq, heads * HEAD_DIM), BF16)],
        compiler_params=_cparams("parallel"),
    )(proj, ckv_gain.reshape(1, rank), w_kv, k_gain_n, proj, k_gain_r, *tabs)


def _mla_attn_body(q_ref, k_ref, v_ref, o_ref, *, tq, tk, hp):
    qi = pl.program_id(2)
    hs = range(hp)
    qk = lambda h: slice(h * 2 * HEAD_DIM, (h + 1) * 2 * HEAD_DIM)
    vs = lambda h: slice(h * HEAD_DIM, (h + 1) * HEAD_DIM)
    q = [q_ref[:, qk(h)] for h in hs]
    rr = lax.broadcasted_iota(jnp.int32, (tq, tk), 0)
    cc = lax.broadcasted_iota(jnp.int32, (tq, tk), 1)
    nsub = tq // tk

    ones = jnp.ones((tk, HEAD_DIM), BF16)

    def block(j, carry, diag):
        m, acc = carry
        ks = pl.multiple_of(j * tk, tk)
        s = [_dot_nt(q[h], k_ref[pl.ds(ks, tk), qk(h)]) for h in hs]
        if diag:
            vis = cc + (j - qi * nsub) * tk <= rr
            s = [jnp.where(vis, s[h], NEG_INF) for h in hs]
        m_new = tuple(jnp.maximum(m[h], jnp.max(s[h], axis=-1, keepdims=True)) for h in hs)
        alpha = [jnp.exp(m[h] - m_new[h]) for h in hs]
        p = [jnp.exp((s[h] - m_new[h]).astype(BF16)) for h in hs]
        v1 = [jnp.concatenate([v_ref[pl.ds(ks, tk), vs(h)], ones], axis=1) for h in hs]
        acc = tuple(alpha[h] * acc[h] + _dot(p[h], v1[h]) for h in hs)
        return m_new, acc

    init = (tuple(jnp.full((tq, 1), NEG_INF, F32) for _ in hs),
            tuple(jnp.zeros((tq, 2 * HEAD_DIM), F32) for _ in hs))
    carry = lax.fori_loop(0, qi * nsub, lambda j, c: block(j, c, False), init)
    for d in range(nsub):
        carry = block(qi * nsub + d, carry, True)
    _, acc = carry
    for h in hs:
        o_ref[:, vs(h)] = (acc[h][:, :HEAD_DIM] / acc[h][:, HEAD_DIM:HEAD_DIM + 1]).astype(o_ref.dtype)


def _mla_attention(q, k, v, *, heads, tq, tk):
    nb, seq, _ = q.shape
    hp = _pick(heads, (4, 2, 1))
    return _pcall(
        functools.partial(_mla_attn_body, tq=tq, tk=tk, hp=hp),
        grid=(nb, heads // hp, seq // tq),
        in_specs=[pl.BlockSpec((None, tq, hp * 2 * HEAD_DIM), lambda b, h, i: (b, i, h)),
                  pl.BlockSpec((None, seq, hp * 2 * HEAD_DIM), lambda b, h, i: (b, 0, h)),
                  pl.BlockSpec((None, seq, hp * HEAD_DIM), lambda b, h, i: (b, 0, h))],
        out_specs=pl.BlockSpec((None, tq, hp * HEAD_DIM), lambda b, h, i: (b, i, h)),
        out_shape=jax.ShapeDtypeStruct((nb, seq, heads * HEAD_DIM), BF16),
        compiler_params=_cparams("parallel", "parallel", "arbitrary"),
    )(q, k, v)


def _bdot(a, b):
    return _dot(a.astype(BF16), b.astype(BF16))


def _dn_body(x_ref, bd_ref, z_ref, cw_ref, alog_ref, dtb_ref, og_ref, o_ref, state, xbuf, *, heads):
    c = DN_CHUNK
    hd = HEAD_DIM

    @pl.when(pl.program_id(1) == 0)
    def _():
        state[...] = jnp.zeros_like(state)
        xbuf[0:8, :] = jnp.zeros((8, 3 * heads * hd), F32)

    xbuf[8:8 + c, :] = x_ref[...].astype(F32)
    cw = cw_ref[...]
    taps = cw.shape[0]

    def head_block(hh):
        sl = slice(hh * hd, (hh + 1) * hd)
        blk = cw[taps - 1:taps, sl] * xbuf[8:8 + c, sl]
        for dd in range(1, taps):
            blk = blk + cw[taps - 1 - dd:taps - dd, sl] * xbuf[8 - dd:8 - dd + c, sl]
        blk = blk * _sigmoid(blk)
        if hh < 2 * heads:
            blk = blk * lax.rsqrt(jnp.sum(blk * blk, axis=-1, keepdims=True) + NORM_EPS)
            if hh < heads:
                blk = blk * hd ** -0.5
        return blk

    bd = bd_ref[...].astype(F32)
    beta = _sigmoid(bd)
    g = -jnp.exp(alog_ref[...]) * _softplus(bd + dtb_ref[...])
    crow = lax.broadcasted_iota(jnp.int32, (c, 1), 0)
    d = 1
    while d < c:
        g = g + jnp.where(crow >= d, pltpu.roll(g, d, 0), 0.0)
        d *= 2

    rr = lax.broadcasted_iota(jnp.int32, (c, c), 0)
    cc = lax.broadcasted_iota(jnp.int32, (c, c), 1)
    lower = rr >= cc
    strict = rr > cc
    eye = (rr == cc).astype(F32)
    same = lambda sh: jnp.right_shift(rr, sh) == jnp.right_shift(cc, sh)
    base_sh = 4
    in_base = strict & same(base_sh)
    off_levels = [strict & same(sh + 1) & jnp.logical_not(same(sh))
                  for sh in range(base_sh, int(math.log2(c)))]
    hs = range(heads)
    q = [head_block(h) for h in hs]
    k = [head_block(heads + h) for h in hs]
    v = [head_block(2 * heads + h) for h in hs]
    xbuf[0:8, :] = xbuf[c:c + 8, :]
    gcol = [g[:, heads + h:heads + h + 1] for h in hs]
    bcol = [beta[:, h:h + 1] for h in hs]
    gcb = [jnp.broadcast_to(gcol[h], (c, c)) for h in hs]
    dec = [jnp.where(lower, jnp.exp(jnp.where(lower, gcb[h] - gcb[h].T, 0.0)), 0.0) for h in hs]
    kb = [k[h] * bcol[h] for h in hs]
    kq = [_dot_nt(jnp.concatenate([kb[h], q[h]], axis=0).astype(BF16), k[h].astype(BF16)) for h in hs]
    a_mat = [jnp.where(strict, kq[h][:c] * dec[h], 0.0) for h in hs]
    intra = [kq[h][c:] * dec[h] for h in hs]
    pw = [-jnp.where(in_base, a_mat[h], 0.0) for h in hs]
    p_mat = [eye + pw[h] for h in hs]
    for _ in range(base_sh - 1):
        pw = [_bdot(pw[h], pw[h]) for h in hs]
        p_mat = [p_mat[h] + _bdot(p_mat[h], pw[h]) for h in hs]
    for off in off_levels:
        pc = [_bdot(p_mat[h], jnp.where(off, a_mat[h], 0.0)) for h in hs]
        p_mat = [p_mat[h] - _bdot(pc[h], p_mat[h]) for h in hs]
    eg = [jnp.exp(gcol[h]) for h in hs]
    uw = [_bdot(p_mat[h], jnp.concatenate([v[h] * bcol[h], kb[h] * eg[h]], axis=1)) for h in hs]
    glast = [gcol[h][c - 1:c, :] for h in hs]
    s_old = [state[h] for h in hs]
    ws = [_bdot(jnp.concatenate([uw[h][:, hd:], q[h] * eg[h]], axis=0), s_old[h]) for h in hs]
    v_new = [uw[h][:, :hd] - ws[h][:c] for h in hs]
    o = [ws[h][c:] + _bdot(intra[h], v_new[h]) for h in hs]
    k_dec = [k[h] * jnp.exp(glast[h] - gcol[h]) for h in hs]
    for h in hs:
        state[h] = s_old[h] * jnp.exp(glast[h]) + _bdot(k_dec[h].T, v_new[h])
    for h in hs:
        zz = z_ref[:, h * hd:(h + 1) * hd].astype(F32)
        o_ref[:, h * hd:(h + 1) * hd] = (_rms(o[h], og_ref[...]) * (zz * _sigmoid(zz))).astype(o_ref.dtype)


def _delta_rule(proj, conv_w, a_log, dt_bias, out_gain, *, heads, qkv_blk, bd_blk, z_blk, nb, seq):
    c = DN_CHUNK
    nt = seq // c
    hw = heads * HEAD_DIM
    pad = lambda v: jnp.zeros((1, LANES), F32).at[0, heads:2 * heads].set(v)
    row = lambda b, t: b * nt + t
    return _pcall(
        functools.partial(_dn_body, heads=heads),
        grid=(nb, nt),
        in_specs=[pl.BlockSpec((c, 3 * hw), lambda b, t: (row(b, t), qkv_blk)),
                  pl.BlockSpec((c, LANES), lambda b, t: (row(b, t), bd_blk)),
                  pl.BlockSpec((c, hw), lambda b, t: (row(b, t), z_blk)),
                  pl.BlockSpec(conv_w.shape, lambda b, t: (0, 0)),
                  pl.BlockSpec((1, LANES), lambda b, t: (0, 0)),
                  pl.BlockSpec((1, LANES), lambda b, t: (0, 0)),
                  pl.BlockSpec((1, HEAD_DIM), lambda b, t: (0, 0))],
        out_specs=pl.BlockSpec((c, hw), lambda b, t: (row(b, t), 0)),
        out_shape=jax.ShapeDtypeStruct((nb * seq, hw), BF16),
        scratch_shapes=[pltpu.VMEM((heads, HEAD_DIM, HEAD_DIM), F32), pltpu.VMEM((c + 8, 3 * hw), F32)],
        compiler_params=_cparams("parallel", "arbitrary"),
    )(proj, proj, proj, conv_w, pad(a_log), pad(dt_bias), out_gain.reshape(1, HEAD_DIM))


def _merge_body(ya_ref, yb_ref, yc_ref, yd_ref, g0_ref, g1_ref, g2_ref, g3_ref, wb_ref, o_ref):
    acc = None
    for i, (y_ref, gl_ref) in enumerate(((ya_ref, g0_ref), (yb_ref, g1_ref), (yc_ref, g2_ref),
                                         (yd_ref, g3_ref))):
        t = _sigmoid(gl_ref[...].astype(F32)) * _dot(y_ref[...], wb_ref[i])
        acc = t if acc is None else acc + t
    o_ref[...] = acc.astype(o_ref.dtype)


def _gated_merge(ys, proj, w_branch, *, gate_off, tm, tn):
    m, bw = ys[0].shape
    nbr, _, n = w_branch.shape
    yspec = pl.BlockSpec((tm, bw), lambda i, j: (i, 0))
    gspecs = [pl.BlockSpec((tm, tn), functools.partial(lambda i, j, o: (i, o + j), o=(gate_off + r * n) // tn))
              for r in range(nbr)]
    return _pcall(
        _merge_body,
        grid=(m // tm, n // tn),
        in_specs=[yspec] * nbr + gspecs + [pl.BlockSpec((nbr, bw, tn), lambda i, j: (0, 0, j))],
        out_specs=pl.BlockSpec((tm, tn), lambda i, j: (i, j)),
        out_shape=jax.ShapeDtypeStruct((m, n), BF16),
        compiler_params=_cparams("parallel", "arbitrary"),
    )(*ys, *([proj] * nbr), w_branch)


def _mem_kv_body(m_ref, g_ref, w_ref, kg_ref, k_ref, v_ref, *, heads):
    xn = _rms(m_ref[...], g_ref[...]).astype(BF16)
    kv = _dot(xn, w_ref[...])
    hw = heads * HEAD_DIM
    for h in range(heads):
        k_ref[:, h * HEAD_DIM:(h + 1) * HEAD_DIM] = _rms(kv[:, h * HEAD_DIM:(h + 1) * HEAD_DIM],
                                                         kg_ref[...]).astype(k_ref.dtype)
    v_ref[...] = kv[:, hw:].astype(v_ref.dtype)


def _mem_kv(mem2, g, wkv, k_gain, *, nb, mlen, heads):
    d = mem2.shape[1]
    hw = heads * HEAD_DIM
    out = pl.BlockSpec((mlen, hw), lambda b: (b, 0))
    shp = jax.ShapeDtypeStruct((nb * mlen, hw), BF16)
    return _pcall(
        functools.partial(_mem_kv_body, heads=heads),
        grid=(nb,),
        in_specs=[pl.BlockSpec((mlen, d), lambda b: (b, 0)),
                  pl.BlockSpec((1, d), lambda b: (0, 0)),
                  pl.BlockSpec((d, 2 * hw), lambda b: (0, 0)),
                  pl.BlockSpec((1, HEAD_DIM), lambda b: (0, 0))],
        out_specs=[out, out],
        out_shape=[shp, shp],
        compiler_params=_cparams("parallel"),
    )(mem2, g.reshape(1, d), wkv, k_gain.reshape(1, HEAD_DIM))


def _xattn_body(x_ref, g_ref, wq_ref, k_ref, v_ref, qg_ref, wo_ref, o_ref, *, heads):
    x = x_ref[...]
    q = _dot(_rms(x, g_ref[...]).astype(BF16), wq_ref[...])
    scale = HEAD_DIM ** -0.5
    outs = []
    for h in range(heads):
        sl = slice(h * HEAD_DIM, (h + 1) * HEAD_DIM)
        qh = (_rms(q[:, sl], qg_ref[...]) * scale).astype(BF16)
        s = _dot_nt(qh, k_ref[:, sl])
        p = jnp.exp(s - jnp.max(s, axis=-1, keepdims=True))
        l = jnp.sum(p, axis=-1, keepdims=True)
        outs.append(_dot(p.astype(BF16), v_ref[:, sl]) / l)
    o = jnp.concatenate(outs, axis=-1).astype(BF16)
    o_ref[...] = x + _dot(o, wo_ref[...])


def _cross_attention(x2, g, wq, k, v, q_gain, wo, *, nb, seq, mlen, heads, tm):
    d = x2.shape[1]
    hw = heads * HEAD_DIM
    nt = seq // tm
    xspec = pl.BlockSpec((tm, d), lambda b, t: (b * nt + t, 0))
    return _pcall(
        functools.partial(_xattn_body, heads=heads),
        grid=(nb, nt),
        in_specs=[xspec,
                  pl.BlockSpec((1, d), lambda b, t: (0, 0)),
                  pl.BlockSpec((d, hw), lambda b, t: (0, 0)),
                  pl.BlockSpec((mlen, hw), lambda b, t: (b, 0)),
                  pl.BlockSpec((mlen, hw), lambda b, t: (b, 0)),
                  pl.BlockSpec((1, HEAD_DIM), lambda b, t: (0, 0)),
                  pl.BlockSpec((hw, d), lambda b, t: (0, 0))],
        out_specs=xspec,
        out_shape=jax.ShapeDtypeStruct(x2.shape, F32),
        compiler_params=_cparams("parallel", "arbitrary"),
    )(x2, g.reshape(1, d), wq, k, v, q_gain.reshape(1, HEAD_DIM), wo)


def _router_body(x_ref, g_ref, w_ref, o_ref, *, n_experts):
    logits = _dot(_rms(x_ref[...], g_ref[...]), w_ref[...], HIGHEST)
    lane = lax.broadcasted_iota(jnp.int32, logits.shape, 1)
    lg = jnp.where(lane < n_experts, logits, -jnp.inf)
    m1 = jnp.max(lg, axis=-1, keepdims=True)
    i1 = jnp.min(jnp.where(lg == m1, lane, LANES), axis=-1, keepdims=True)
    lg2 = jnp.where(lane == i1, -jnp.inf, lg)
    m2 = jnp.max(lg2, axis=-1, keepdims=True)
    i2 = jnp.min(jnp.where(lg2 == m2, lane, LANES), axis=-1, keepdims=True)
    e2 = jnp.exp(m2 - m1)
    den = 1.0 + e2
    o_ref[...] = (jnp.where(lane == 0, 1.0 / den, 0.0) + jnp.where(lane == 1, e2 / den, 0.0)
                  + jnp.where(lane == 2, i1.astype(F32), 0.0) + jnp.where(lane == 3, i2.astype(F32), 0.0))


def _router(x2, g, w_router, *, tm):
    m, d = x2.shape
    ne = w_router.shape[1]
    wpad = jnp.zeros((d, LANES), F32).at[:, :ne].set(w_router)
    return _pcall(
        functools.partial(_router_body, n_experts=ne),
        grid=(m // tm,),
        in_specs=[pl.BlockSpec((tm, d), lambda i: (i, 0)),
                  pl.BlockSpec((1, d), lambda i: (0, 0)),
                  pl.BlockSpec((d, LANES), lambda i: (0, 0))],
        out_specs=pl.BlockSpec((tm, LANES), lambda i: (i, 0)),
        out_shape=jax.ShapeDtypeStruct((m, LANES), F32),
        compiler_params=_cparams("parallel"),
    )(x2, g.reshape(1, d), wpad)


def _rope_tables(seq):
    half = MLA_ROPE // 2
    inv_freq = ROPE_BASE ** (-jnp.arange(half, dtype=F32) / half)
    ang = jnp.arange(seq, dtype=jnp.int32).astype(F32)[:, None] * inv_freq[None, :]
    cos, sin = jnp.cos(ang), jnp.sin(ang)
    z = jnp.zeros((seq, half), F32)
    cos_t = jnp.concatenate([cos, cos, z, z], axis=-1)
    sin_a = jnp.concatenate([-sin, z, z, z], axis=-1)
    sin_b = jnp.concatenate([z, sin, z, z], axis=-1)
    return cos_t, sin_a, sin_b


def kernel(x, mem, mix_norm, w_in, rg_conv_w, rg_conv_b, rg_w_a, rg_b_a, rg_w_x, rg_b_x, rg_lambda, sb_q_gain, sb_k_gain, mla_cq_gain, mla_w_uq, mla_ckv_gain, mla_w_ukv, mla_q_gain, mla_k_gain, dn_conv_w, dn_a_log, dn_dt_bias, dn_out_gain, w_branch, w_out, xa_norm, mem_norm, xa_wq, xa_wkv, xa_q_gain, xa_k_gain, xa_wo, ffn_norm, ffn_w_gu, ffn_w_down, moe_router, moe_w_gu, moe_w_down):
    nb, seq, d = x.shape
    depth = w_in.shape[0]
    mlen = mem.shape[1]
    bw = rg_conv_w.shape[2]
    heads = bw // HEAD_DIM
    q_rank = mla_cq_gain.shape[1]
    kv_rank = mla_ckv_gain.shape[1]
    xa_heads = xa_wq.shape[2] // HEAD_DIM
    t_tok = nb * seq
    assert bw % HEAD_DIM == 0 and seq % 512 == 0 and d % 512 == 0
    assert mla_q_gain.shape[1] == HEAD_DIM + MLA_ROPE

    src_widths = (("rg", 2 * bw), ("sb", 3 * bw), ("cq", q_rank), ("ckv", kv_rank), ("kr", MLA_ROPE),
                  ("dnqkv", 3 * bw), ("dnz", bw), ("bd", 2 * heads), ("gate", 4 * d))
    src, s_pos = {}, 0
    for name, width in src_widths:
        src[name] = (s_pos, s_pos + width)
        s_pos += width
    assert s_pos == w_in.shape[2] and 2 * heads <= LANES
    packed = (("rg", bw), ("sb", _pick(heads, (4, 2, 1)) * HEAD_DIM), ("cq", q_rank), ("ckv", kv_rank),
              ("kr", LANES), ("bd", LANES), ("dnqkv", 3 * bw), ("dnz", bw), ("gate", d))
    off, pieces, p_pos = {}, [], 0
    for name, align in packed:
        width = src[name][1] - src[name][0]
        start = -(-p_pos // align) * align
        pieces.append((start - p_pos, src[name]))
        off[name] = start
        p_pos = start + width
        if name in ("kr", "bd"):
            pieces.append((LANES - width, None))
            p_pos += LANES - width
    n_proj = -(-p_pos // 512) * 512
    pieces.append((n_proj - p_pos, None))

    x2 = x.reshape(t_tok, d)
    mem2 = mem.reshape(nb * mlen, d)
    tabs = _rope_tables(seq)
    mla_scale = (HEAD_DIM + MLA_ROPE) ** -0.5

    cols = []
    for gap, rng in pieces:
        if gap:
            cols.append(jnp.zeros((depth, d, gap), BF16))
        if rng is not None:
            cols.append(w_in[:, :, rng[0]:rng[1]].astype(BF16))
    w_proj = jnp.concatenate(cols, axis=2)

    tm_big = _pick(t_tok, (1024, 512, 256, 128))
    tm_mid = _pick(t_tok, (512, 256, 128))
    ts_seq = _pick(seq, (512, 256, 128))

    for layer in range(depth):
        wq3 =mla_w_uq[layer].reshape(q_rank, heads, HEAD_DIM + MLA_ROPE)
        w_q = jnp.concatenate([wq3, jnp.zeros((q_rank, heads, HEAD_DIM - MLA_ROPE), F32)],
                              axis=-1).reshape(q_rank, heads * 2 * HEAD_DIM).astype(BF16)
        qg = mla_q_gain[layer] * mla_scale
        q_gain2 = jnp.concatenate([qg, jnp.zeros((HEAD_DIM - MLA_ROPE,), F32)]).reshape(1, 2 * HEAD_DIM)
        kgl = mla_k_gain[layer]
        k_gain_n = kgl[:HEAD_DIM].reshape(1, HEAD_DIM)
        k_gain_r = jnp.concatenate([kgl[HEAD_DIM:], jnp.zeros((LANES - MLA_ROPE,), F32)]).reshape(1, LANES)

        proj = _norm_matmul(x2, mix_norm[layer], w_proj, layer=layer, out_dtype=BF16, tm=tm_big,
                            tn=_pick(n_proj, (1024, 512, 256, 128)))
        proj3 = proj.reshape(nb, seq, n_proj)

        y_a = _rglru(proj, nb, seq, rg_conv_w[layer], rg_conv_b[layer], rg_w_a[layer], rg_b_a[layer],
                     rg_w_x[layer], rg_b_x[layer], rg_lambda[layer], ts=_pick(seq, (256, 128)))

        y_b = _sb_attention(proj3, sb_q_gain[layer], sb_k_gain[layer], heads=heads,
                            q_off=off["sb"], tq=_pick(seq, (256, 128)))

        mq = _mla_q(proj, mla_cq_gain[layer], w_q, q_gain2, tabs, heads=heads,
                    cq_blk=off["cq"] // q_rank, nb=nb, seq=seq, tm=ts_seq)
        mk, mv = _mla_kv(proj, mla_ckv_gain[layer], mla_w_ukv[layer].astype(BF16), k_gain_n,
                         k_gain_r, tabs, heads=heads, ckv_blk=off["ckv"] // kv_rank,
                         kr_blk=off["kr"] // LANES, nb=nb, seq=seq, tm=ts_seq)
        r3 = lambda a: a.reshape(nb, seq, a.shape[-1])
        y_c = _mla_attention(r3(mq), r3(mk), r3(mv), heads=heads, tq=ts_seq, tk=ts_seq)

        y_d = _delta_rule(proj, dn_conv_w[layer], dn_a_log[layer], dn_dt_bias[layer], dn_out_gain[layer],
                          heads=heads, qkv_blk=off["dnqkv"] // (3 * bw), bd_blk=off["bd"] // LANES,
                          z_blk=off["dnz"] // bw, nb=nb, seq=seq)

        merged = _gated_merge([y_a, y_b.reshape(t_tok, bw), y_c.reshape(t_tok, bw), y_d], proj,
                              w_branch[layer].astype(BF16), gate_off=off["gate"], tm=tm_big,
                              tn=_pick(d, (512, 256, 128)))
        x2 = _matmul_residual(merged, w_out[layer].astype(BF16), x2, tm=tm_big,
                              tn=_pick(d, (1024, 512, 256, 128)), tk=d)

        mk, mv = _mem_kv(mem2, mem_norm[layer], xa_wkv[layer].astype(BF16), xa_k_gain[layer],
                         nb=nb, mlen=mlen, heads=xa_heads)
        x2 = _cross_attention(x2, xa_norm[layer], xa_wq[layer].astype(BF16), mk, mv, xa_q_gain[layer],
                              xa_wo[layer].astype(BF16), nb=nb, seq=seq, mlen=mlen, heads=xa_heads,
                              tm=ts_seq)

        if layer % 2 == 0:
            wgu = ffn_w_gu[layer // 2].astype(BF16)
            f = wgu.shape[1] // 2
            hmid = _norm_swiglu(x2, ffn_norm[layer], wgu, tm=tm_big, tn=_pick(f, (512, 256, 128)))
            x2 = _matmul_residual(hmid, ffn_w_down[layer // 2].astype(BF16), x2, tm=tm_big,
                                  tn=_pick(d, (512, 256, 128)), tk=f)
        else:
            wgu = moe_w_gu[layer // 2].astype(BF16)
            ne, _, f2 = wgu.shape
            f = f2 // 2
            rt = _router(x2, ffn_norm[layer], moe_router[layer // 2], tm=tm_mid)
            pos, tile_expert, n_used, p_rows = _moe_plan(rt, ne, tm_mid)
            tm_c = _pick(t_tok, (256, 128))
            xg = _moe_scatter(x2, pos.reshape(t_tok // tm_mid, 1, TOP_K * tm_mid), p_rows, tm=tm_mid)
            hmid = _moe_up(xg, ffn_norm[layer], wgu, tile_expert, n_used, tm=tm_mid,
                           tn=_pick(f, (1408, 512, 256, 128)))
            yg = _moe_down(hmid, moe_w_down[layer // 2].astype(BF16), tile_expert, n_used, tm=tm_mid)
            x2 = _moe_combine(x2, rt, pos.reshape(t_tok // tm_c, 1, TOP_K * tm_c), yg, tm=tm_c)
    return x2.reshape(nb, seq, d)
```

```python
import functools
import math

import jax
import jax.numpy as jnp
import numpy as np
from jax import lax
from jax.experimental import pallas as pl
from jax.experimental.pallas import tpu as pltpu

F32 = jnp.float32
BF16 = jnp.bfloat16
HIGHEST = lax.Precision.HIGHEST

HEAD_DIM = 128
LANES = 128
MLA_ROPE = 64
RG_C = 8.0
ROPE_BASE = 10000.0
NORM_EPS = 1e-6
NEG_INF = -1e30
EXP_UNDERFLOW = -104.0
DN_CHUNK = 128
TOP_K = 2
VMEM_LIMIT = 56 * 1024 * 1024


def _cparams(*sem):
    return pltpu.CompilerParams(dimension_semantics=sem, vmem_limit_bytes=VMEM_LIMIT)


def _pcall(body, **kwargs):
    fn = getattr(body, "func", body)
    return pl.pallas_call(body, name=fn.__name__.strip("_").replace("_body", ""), **kwargs)


def _pick(n, prefs):
    for p in prefs:
        if p <= n and n % p == 0:
            return p
    return n


def _rms(x, g):
    return x * lax.rsqrt(jnp.mean(x * x, axis=-1, keepdims=True) + NORM_EPS) * g


def _sigmoid(x):
    return 1.0 / (1.0 + jnp.exp(-x))


def _softplus(x):
    return jnp.maximum(x, 0.0) + jnp.log(1.0 + jnp.exp(-jnp.abs(x)))


def _dot(a, b, precision=None):
    return jnp.dot(a, b, preferred_element_type=F32, precision=precision)


def _dot_nt(a, b, precision=None):
    return lax.dot_general(a, b, (((1,), (1,)), ((), ())), preferred_element_type=F32,
                           precision=precision)


def _norm_mm_body(x_ref, g_ref, w_ref, o_ref, xn_ref, *, precision):
    @pl.when(pl.program_id(1) == 0)
    def _():
        xn_ref[...] = _rms(x_ref[...].astype(F32), g_ref[...]).astype(xn_ref.dtype)

    o_ref[...] = _dot(xn_ref[...], w_ref[...], precision).astype(o_ref.dtype)


def _norm_matmul(x, g, w, *, layer, out_dtype, tm, tn, precision=None):
    m, k = x.shape
    n = w.shape[2]
    return _pcall(
        functools.partial(_norm_mm_body, precision=precision),
        grid=(m // tm, n // tn),
        in_specs=[pl.BlockSpec((tm, k), lambda i, j: (i, 0)),
                  pl.BlockSpec((1, k), lambda i, j: (0, 0)),
                  pl.BlockSpec((None, k, tn), lambda i, j: (layer, 0, j))],
        out_specs=pl.BlockSpec((tm, tn), lambda i, j: (i, j)),
        out_shape=jax.ShapeDtypeStruct((m, n), out_dtype),
        scratch_shapes=[pltpu.VMEM((tm, k), w.dtype)],
        compiler_params=_cparams("parallel", "arbitrary"),
    )(x, g.reshape(1, k), w)


def _swiglu_body(x_ref, g_ref, wg_ref, wu_ref, o_ref, xn_ref):
    @pl.when(pl.program_id(1) == 0)
    def _():
        xn_ref[...] = _rms(x_ref[...], g_ref[...]).astype(BF16)

    xn = xn_ref[...]
    a = _dot(xn, wg_ref[...])
    b = _dot(xn, wu_ref[...])
    o_ref[...] = (a * _sigmoid(a) * b).astype(o_ref.dtype)


def _norm_swiglu(x, g, w_gu, *, tm, tn):
    m, k = x.shape
    f = w_gu.shape[1] // 2
    nj = f // tn
    return _pcall(
        _swiglu_body,
        grid=(m // tm, nj),
        in_specs=[pl.BlockSpec((tm, k), lambda i, j: (i, 0)),
                  pl.BlockSpec((1, k), lambda i, j: (0, 0)),
                  pl.BlockSpec((k, tn), lambda i, j: (0, j)),
                  pl.BlockSpec((k, tn), lambda i, j: (0, j + nj))],
        out_specs=pl.BlockSpec((tm, tn), lambda i, j: (i, j)),
        out_shape=jax.ShapeDtypeStruct((m, f), BF16),
        scratch_shapes=[pltpu.VMEM((tm, k), BF16)],
        compiler_params=_cparams("parallel", "arbitrary"),
    )(x, g.reshape(1, k), w_gu, w_gu)


def _row_copies(pos_ref, n_rows, make_copy, make_bulk):
    def issue(r, c):
        for k in range(TOP_K):
            make_copy(r, k, pos_ref[0, TOP_K * r + k]).start()
        return c

    lax.fori_loop(0, n_rows, issue, 0, unroll=4)
    for _ in range(TOP_K):
        make_bulk().wait()


def _moe_scatter_body(pos_ref, x_ref, xg_init_hbm, xg_hbm, sem, *, tm):
    del xg_init_hbm
    _row_copies(pos_ref, tm,
                lambda r, k, p: pltpu.make_async_copy(x_ref.at[pl.ds(r, 1)], xg_hbm.at[pl.ds(p, 1)], sem),
                lambda: pltpu.make_async_copy(x_ref, xg_hbm.at[pl.ds(0, tm)], sem))


def _moe_scatter(x2, pos3, n_rows, *, tm):
    t_tok, d = x2.shape
    return _pcall(
        functools.partial(_moe_scatter_body, tm=tm),
        grid=(t_tok // tm,),
        in_specs=[pl.BlockSpec((None, 1, TOP_K * tm), lambda i: (i, 0, 0), memory_space=pltpu.SMEM),
                  pl.BlockSpec((tm, d), lambda i: (i, 0)),
                  pl.BlockSpec(memory_space=pl.ANY)],
        out_specs=pl.BlockSpec(memory_space=pl.ANY),
        out_shape=jax.ShapeDtypeStruct((n_rows, d), x2.dtype),
        scratch_shapes=[pltpu.SemaphoreType.DMA(())],
        input_output_aliases={2: 0},
        compiler_params=_cparams("arbitrary"),
    )(pos3, x2, jnp.zeros((n_rows, d), x2.dtype))


def _moe_up_body(te_ref, nu_ref, xg_ref, g_ref, wg_ref, wu_ref, o_ref):
    del te_ref
    used = pl.program_id(1) < nu_ref[0]

    @pl.when(used)
    def _():
        xn = _rms(xg_ref[...], g_ref[...]).astype(BF16)
        a = _dot(xn, wg_ref[...])
        b = _dot(xn, wu_ref[...])
        o_ref[...] = (a * _sigmoid(a) * b).astype(o_ref.dtype)

    @pl.when(jnp.logical_not(used))
    def _():
        o_ref[...] = jnp.zeros_like(o_ref)


def _moe_up(xg, g, w_gu, tile_expert, n_used, *, tm, tn):
    p_rows, d = xg.shape
    f = w_gu.shape[2] // 2
    nj = f // tn
    grid_spec = pltpu.PrefetchScalarGridSpec(
        num_scalar_prefetch=2, grid=(nj, p_rows // tm),
        in_specs=[pl.BlockSpec((tm, d), lambda j, i, te, nu: (i, 0)),
                  pl.BlockSpec((1, d), lambda j, i, te, nu: (0, 0)),
                  pl.BlockSpec((None, d, tn), lambda j, i, te, nu: (te[i], 0, j)),
                  pl.BlockSpec((None, d, tn), lambda j, i, te, nu: (te[i], 0, j + nj))],
        out_specs=pl.BlockSpec((tm, tn), lambda j, i, te, nu: (i, j)))
    return _pcall(
        _moe_up_body, grid_spec=grid_spec,
        out_shape=jax.ShapeDtypeStruct((p_rows, f), BF16),
        compiler_params=_cparams("arbitrary", "arbitrary"),
    )(tile_expert, n_used, xg, g.reshape(1, d), w_gu, w_gu)


def _moe_down_body(te_ref, nu_ref, h_ref, w_ref, o_ref):
    del te_ref
    used = pl.program_id(0) < nu_ref[0]

    @pl.when(used)
    def _():
        o_ref[...] = _dot(h_ref[...], w_ref[...])

    @pl.when(jnp.logical_not(used))
    def _():
        o_ref[...] = jnp.zeros_like(o_ref)


def _moe_down(hmid, w_down, tile_expert, n_used, *, tm):
    p_rows, f = hmid.shape
    d = w_down.shape[2]
    grid_spec = pltpu.PrefetchScalarGridSpec(
        num_scalar_prefetch=2, grid=(p_rows // tm,),
        in_specs=[pl.BlockSpec((tm, f), lambda i, te, nu: (i, 0)),
                  pl.BlockSpec((None, f, d), lambda i, te, nu: (te[i], 0, 0))],
        out_specs=pl.BlockSpec((tm, d), lambda i, te, nu: (i, 0)))
    return _pcall(
        _moe_down_body, grid_spec=grid_spec,
        out_shape=jax.ShapeDtypeStruct((p_rows, d), F32),
        compiler_params=_cparams("arbitrary"),
    )(tile_expert, n_used, hmid, w_down)


def _moe_combine_body(pos_ref, pos_next_ref, rt_ref, x_ref, y_hbm, o_ref, buf, sem, *, tm, n_steps):
    i = pl.program_id(0)
    slot = lax.rem(i, 2)

    def gather(pref, s):
        def issue(r, c):
            for k in range(TOP_K):
                pltpu.make_async_copy(y_hbm.at[pl.ds(pref[0, TOP_K * r + k], 1)],
                                      buf.at[s, k, pl.ds(r, 1)], sem.at[s]).start()
            return c

        lax.fori_loop(0, tm, issue, 0, unroll=4)

    @pl.when(i == 0)
    def _():
        gather(pos_ref, 0)

    @pl.when(i + 1 < n_steps)
    def _():
        gather(pos_next_ref, 1 - slot)

    for k in range(TOP_K):
        pltpu.make_async_copy(y_hbm.at[pl.ds(0, tm)], buf.at[slot, k], sem.at[slot]).wait()
    w = rt_ref[...]
    o_ref[...] = x_ref[...] + w[:, 0:1] * buf[slot, 0] + w[:, 1:2] * buf[slot, 1]


def _moe_combine(x2, rt, pos3, y, *, tm):
    t_tok, d = x2.shape
    n_steps = t_tok // tm
    pos_spec = lambda imap: pl.BlockSpec((None, 1, TOP_K * tm), imap, memory_space=pltpu.SMEM)
    return _pcall(
        functools.partial(_moe_combine_body, tm=tm, n_steps=n_steps),
        grid=(n_steps,),
        in_specs=[pos_spec(lambda i: (i, 0, 0)),
                  pos_spec(lambda i: (jnp.minimum(i + 1, n_steps - 1), 0, 0)),
                  pl.BlockSpec((tm, LANES), lambda i: (i, 0)),
                  pl.BlockSpec((tm, d), lambda i: (i, 0)),
                  pl.BlockSpec(memory_space=pl.ANY)],
        out_specs=pl.BlockSpec((tm, d), lambda i: (i, 0)),
        out_shape=jax.ShapeDtypeStruct((t_tok, d), F32),
        scratch_shapes=[pltpu.VMEM((2, TOP_K, tm, d), F32), pltpu.SemaphoreType.DMA((2,))],
        compiler_params=_cparams("arbitrary"),
    )(pos3, pos3, rt, x2, y)


def _moe_plan(rt, n_experts, tm):
    t_tok = rt.shape[0]
    e_flat = rt[:, 2:2 + TOP_K].astype(jnp.int32).reshape(-1)
    onehot = (e_flat[:, None] == jnp.arange(n_experts, dtype=jnp.int32)[None, :]).astype(jnp.int32)
    csum = jnp.cumsum(onehot, axis=0)
    padded = -(-csum[-1] // tm) * tm
    gend = jnp.cumsum(padded)
    pos = jnp.sum(onehot * (csum - 1 + (gend - padded)[None, :]), axis=1)
    n_tiles = (TOP_K * t_tok) // tm + n_experts
    tile_start = jnp.arange(n_tiles, dtype=jnp.int32) * tm
    tile_expert = jnp.minimum(jnp.sum((tile_start[:, None] >= gend[None, :]).astype(jnp.int32), axis=1),
                              n_experts - 1)
    return pos, tile_expert, (gend[-1:] // tm).astype(jnp.int32), n_tiles * tm


def _mm_res_body(a_ref, w_ref, r_ref, o_ref, *scratch, nk):
    if nk == 1:
        o_ref[...] = r_ref[...] + _dot(a_ref[...], w_ref[...])
        return
    acc_ref, = scratch
    kk = pl.program_id(2)

    @pl.when(kk == 0)
    def _():
        acc_ref[...] = jnp.zeros_like(acc_ref)

    acc_ref[...] += _dot(a_ref[...], w_ref[...])

    @pl.when(kk == nk - 1)
    def _():
        o_ref[...] = r_ref[...] + acc_ref[...]


def _matmul_residual(a, w, r, *, tm, tn, tk):
    m, k = a.shape
    n = w.shape[1]
    nk = k // tk
    return _pcall(
        functools.partial(_mm_res_body, nk=nk),
        grid=(m // tm, n // tn, nk),
        in_specs=[pl.BlockSpec((tm, tk), lambda i, j, kk: (i, kk)),
                  pl.BlockSpec((tk, tn), lambda i, j, kk: (kk, j)),
                  pl.BlockSpec((tm, tn), lambda i, j, kk: (i, j))],
        out_specs=pl.BlockSpec((tm, tn), lambda i, j, kk: (i, j)),
        out_shape=jax.ShapeDtypeStruct((m, n), F32),
        scratch_shapes=[pltpu.VMEM((tm, tn), F32)] if nk > 1 else [],
        compiler_params=_cparams("parallel", "parallel", "arbitrary"),
    )(a, w, r)


def _causal_conv(xbuf, x, cw, ts):
    kk = cw.shape[0]
    xbuf[8:8 + ts, :] = x
    y = cw[kk - 1:kk, :] * x
    for d in range(1, kk):
        y = y + cw[kk - 1 - d:kk - d, :] * xbuf[8 - d:8 - d + ts, :]
    xbuf[0:8, :] = xbuf[ts:ts + 8, :]
    return y


def _rglru_body(x_ref, gate_ref, cw_ref, cb_ref, wa_ref, ba_ref, wx_ref, bx_ref, lam_ref, o_ref,
                xbuf, hcar, *, ts, width):
    @pl.when(pl.program_id(1) == 0)
    def _():
        xbuf[0:8, :] = jnp.zeros((8, width), F32)
        hcar[...] = jnp.zeros_like(hcar)

    u = _causal_conv(xbuf, x_ref[...].astype(F32), cw_ref[...], ts) + cb_ref[...]
    ub = u.astype(BF16)
    rs, is_ = [], []
    for g in range(width // HEAD_DIM):
        blk = ub[:, g * HEAD_DIM:(g + 1) * HEAD_DIM]
        rs.append(_dot(blk, wa_ref[g]))
        is_.append(_dot(blk, wx_ref[g]))
    r = _sigmoid(jnp.concatenate(rs, axis=-1) + ba_ref[...])
    ig = _sigmoid(jnp.concatenate(is_, axis=-1) + bx_ref[...])
    log_a = (-RG_C) * r * _softplus(-lam_ref[...])
    a = jnp.exp(log_a)
    b = jnp.sqrt(1.0 - jnp.exp(2.0 * log_a)) * (ig * u)
    rows = lax.broadcasted_iota(jnp.int32, (ts, 1), 0)
    d = 1
    while d < ts:
        keep = rows >= d
        b = jnp.where(keep, a * pltpu.roll(b, d, 0) + b, b)
        a = jnp.where(keep, a * pltpu.roll(a, d, 0), a)
        d *= 2
    h = a * hcar[0:1, :] + b
    hcar[0:1, :] = h[ts - 1:ts, :]
    gt = gate_ref[...].astype(F32)
    gelu = 0.5 * gt * (1.0 + jnp.tanh(math.sqrt(2.0 / math.pi) * (gt + 0.044715 * (gt * gt * gt))))
    o_ref[...] = (h * gelu).astype(o_ref.dtype)


def _rglru(proj, nb, seq, conv_w, conv_b, w_a, b_a, w_x, b_x, lam, *, ts):
    width = conv_w.shape[1]
    nt = seq // ts
    row = lambda v: v.reshape(1, width)
    full2 = lambda shp: pl.BlockSpec(shp, lambda b, t: (0,) * len(shp))
    return _pcall(
        functools.partial(_rglru_body, ts=ts, width=width),
        grid=(nb, nt),
        in_specs=[pl.BlockSpec((ts, width), lambda b, t: (b * nt + t, 0)),
                  pl.BlockSpec((ts, width), lambda b, t: (b * nt + t, 1)),
                  full2(conv_w.shape), full2((1, width)), full2(w_a.shape), full2((1, width)),
                  full2(w_x.shape), full2((1, width)), full2((1, width))],
        out_specs=pl.BlockSpec((ts, width), lambda b, t: (b * nt + t, 0)),
        out_shape=jax.ShapeDtypeStruct((nb * seq, width), BF16),
        scratch_shapes=[pltpu.VMEM((ts + 8, width), F32), pltpu.VMEM((8, width), F32)],
        compiler_params=_cparams("parallel", "arbitrary"),
    )(proj, proj, conv_w, row(conv_b), w_a.astype(BF16), row(b_a), w_x.astype(BF16), row(b_x), row(lam))


def _sb_body(q_ref, k_ref, v_ref, qg_ref, kg_ref, o_ref, kn_ref, *, tq, scale, hp):
    qi = pl.program_id(2)
    hs = range(hp)
    sl = lambda h: slice(h * HEAD_DIM, (h + 1) * HEAD_DIM)

    @pl.when(qi == 0)
    def _():
        for h in hs:
            kn_ref[:, sl(h)] = _rms(k_ref[:, sl(h)].astype(F32), kg_ref[...]).astype(BF16)

    qn = [(_rms(q_ref[:, sl(h)].astype(F32), qg_ref[...]) * scale).astype(BF16) for h in hs]
    rr = lax.broadcasted_iota(jnp.int32, (tq, tq), 0)
    cc = lax.broadcasted_iota(jnp.int32, (tq, tq), 1)
    upper = (rr > cc).astype(BF16)

    def block(j, accs, runs, diag):
        ks = pl.multiple_of(j * tq, tq)
        z = [_dot_nt(qn[h], kn_ref[pl.ds(ks, tq), sl(h)]) for h in hs]
        sp = [_softplus(z[h]) for h in hs]
        if diag:
            vis = cc < rr
            lk = [jnp.where(vis, -sp[h], 0.0) for h in hs]
        else:
            lk = [-sp[h] for h in hs]
        hi = [lk[h].astype(BF16) for h in hs]
        lo = [(lk[h] - hi[h].astype(F32)).astype(BF16) for h in hs]
        cs = [_dot(hi[h], upper) + _dot(lo[h], upper) for h in hs]
        w = [jnp.exp(z[h] - sp[h] + cs[h] + runs[h]) for h in hs]
        if diag:
            w = [jnp.where(vis, w[h], 0.0) for h in hs]
        accs = tuple(accs[h] + _dot(w[h].astype(BF16), v_ref[pl.ds(ks, tq), sl(h)]) for h in hs)
        runs = tuple(runs[h] + jnp.sum(lk[h], axis=-1, keepdims=True) for h in hs)
        return accs, runs

    accs, runs = block(qi, tuple(jnp.zeros((tq, HEAD_DIM), F32) for _ in hs),
                       tuple(jnp.zeros((tq, 1), F32) for _ in hs), True)

    def more(c):
        return (c[0] >= 0) & (jnp.max(functools.reduce(jnp.maximum, c[2])) > EXP_UNDERFLOW)

    def step(c):
        a, r = block(c[0], c[1], c[2], False)
        return c[0] - 1, a, r

    _, accs, _ = lax.while_loop(more, step, (qi - 1, accs, runs))
    for h in hs:
        o_ref[:, sl(h)] = accs[h].astype(o_ref.dtype)


def _sb_attention(proj3, q_gain, k_gain, *, heads, q_off, tq):
    nb, seq, _ = proj3.shape
    hp = _pick(heads, (4, 2, 1))
    wblk = hp * HEAD_DIM
    bw = heads * HEAD_DIM
    assert q_off % wblk == 0 and bw % wblk == 0
    qb, kb, vb = q_off // wblk, (q_off + bw) // wblk, (q_off + 2 * bw) // wblk
    return _pcall(
        functools.partial(_sb_body, tq=tq, scale=HEAD_DIM ** -0.5, hp=hp),
        grid=(nb, heads // hp, seq // tq),
        in_specs=[pl.BlockSpec((None, tq, wblk), lambda b, h, i: (b, i, qb + h)),
                  pl.BlockSpec((None, seq, wblk), lambda b, h, i: (b, 0, kb + h)),
                  pl.BlockSpec((None, seq, wblk), lambda b, h, i: (b, 0, vb + h)),
                  pl.BlockSpec((1, HEAD_DIM), lambda b, h, i: (0, 0)),
                  pl.BlockSpec((1, HEAD_DIM), lambda b, h, i: (0, 0))],
        out_specs=pl.BlockSpec((None, tq, wblk), lambda b, h, i: (b, i, h)),
        out_shape=jax.ShapeDtypeStruct((nb, seq, bw), BF16),
        scratch_shapes=[pltpu.VMEM((seq, wblk), BF16)],
        compiler_params=_cparams("parallel", "parallel", "arbitrary"),
    )(proj3, proj3, proj3, q_gain.reshape(1, HEAD_DIM), k_gain.reshape(1, HEAD_DIM))


def _rope_pad(r, gain, cos_t, sin_a, sin_b):
    ms = jnp.sum(r * r, axis=-1, keepdims=True) * (1.0 / MLA_ROPE)
    rn = r * lax.rsqrt(ms + NORM_EPS) * gain
    return rn * cos_t + pltpu.roll(rn, LANES - MLA_ROPE // 2, 1) * sin_a + pltpu.roll(rn, MLA_ROPE // 2, 1) * sin_b


def _mla_q_body(cq_ref, g_ref, w_ref, qg_ref, cos_ref, sa_ref, sb_ref, o_ref, *, heads):
    q = _dot(_rms(cq_ref[...].astype(F32), g_ref[...]).astype(BF16), w_ref[...])
    qg = qg_ref[...]
    for h in range(heads):
        c0 = h * 2 * HEAD_DIM
        o_ref[:, c0:c0 + HEAD_DIM] = _rms(q[:, c0:c0 + HEAD_DIM], qg[:, :HEAD_DIM]).astype(o_ref.dtype)
        o_ref[:, c0 + HEAD_DIM:c0 + 2 * HEAD_DIM] = _rope_pad(
            q[:, c0 + HEAD_DIM:c0 + 2 * HEAD_DIM], qg[:, HEAD_DIM:], cos_ref[...], sa_ref[...],
            sb_ref[...]).astype(o_ref.dtype)


def _mla_q(proj, cq_gain, w_q, q_gain2, tabs, *, heads, cq_blk, nb, seq, tm):
    rank = w_q.shape[0]
    nt = seq // tm
    wide = heads * 2 * HEAD_DIM
    tab = pl.BlockSpec((tm, LANES), lambda i: (i % nt, 0))
    return _pcall(
        functools.partial(_mla_q_body, heads=heads),
        grid=(nb * nt,),
        in_specs=[pl.BlockSpec((tm, rank), lambda i: (i, cq_blk)),
                  pl.BlockSpec((1, rank), lambda i: (0, 0)),
                  pl.BlockSpec((rank, wide), lambda i: (0, 0)),
                  pl.BlockSpec((1, 2 * HEAD_DIM), lambda i: (0, 0)),
                  tab, tab, tab],
        out_specs=pl.BlockSpec((tm, wide), lambda i: (i, 0)),
        out_shape=jax.ShapeDtypeStruct((nb * seq, wide), BF16),
        compiler_params=_cparams("parallel"),
    )(proj, cq_gain.reshape(1, rank), w_q, q_gain2, *tabs)


def _mla_kv_body(ckv_ref, g_ref, w_ref, kg_ref, kr_ref, krg_ref, cos_ref, sa_ref, sb_ref,
                 k_ref, v_ref, *, heads):
    kv = _dot(_rms(ckv_ref[...].astype(F32), g_ref[...]).astype(BF16), w_ref[...])
    kro = _rope_pad(kr_ref[...].astype(F32), krg_ref[...], cos_ref[...], sa_ref[...],
                    sb_ref[...]).astype(k_ref.dtype)
    for h in range(heads):
        c0 = h * 2 * HEAD_DIM
        k_ref[:, c0:c0 + HEAD_DIM] = _rms(kv[:, c0:c0 + HEAD_DIM], kg_ref[...]).astype(k_ref.dtype)
        k_ref[:, c0 + HEAD_DIM:c0 + 2 * HEAD_DIM] = kro
        v_ref[:, h * HEAD_DIM:(h + 1) * HEAD_DIM] = kv[:, c0 + HEAD_DIM:c0 + 2 * HEAD_DIM].astype(v_ref.dtype)


def _mla_kv(proj, ckv_gain, w_kv, k_gain_n, k_gain_r, tabs, *, heads, ckv_blk, kr_blk, nb, seq, tm):
    rank = w_kv.shape[0]
    nt = seq // tm
    wide = heads * 2 * HEAD_DIM
    tab = pl.BlockSpec((tm, LANES), lambda i: (i % nt, 0))
    return _pcall(
        functools.partial(_mla_kv_body, heads=heads),
        grid=(nb * nt,),
        in_specs=[pl.BlockSpec((tm, rank), lambda i: (i, ckv_blk)),
                  pl.BlockSpec((1, rank), lambda i: (0, 0)),
                  pl.BlockSpec((rank, wide), lambda i: (0, 0)),
                  pl.BlockSpec((1, HEAD_DIM), lambda i: (0, 0)),
                  pl.BlockSpec((tm, LANES), lambda i: (i, kr_blk)),
                  pl.BlockSpec((1, LANES), lambda i: (0, 0)),
                  tab, tab, tab],
        out_specs=[pl.BlockSpec((tm, wide), lambda i: (i, 0)),
                   pl.BlockSpec((tm, heads * HEAD_DIM), lambda i: (i, 0))],
        out_shape=[jax.ShapeDtypeStruct((nb * seq, wide), BF16),
                   jax.ShapeDtypeStruct((nb * seq, heads * HEAD_DIM), BF16)],
        compiler_params=_cparams("parallel"),
    )(proj, ckv_gain.reshape(1, rank), w_kv, k_gain_n, proj, k_gain_r, *tabs)


def _mla_attn_body(q_ref, k_ref, v_ref, o_ref, *, tq, tk, hp):
    qi = pl.program_id(2)
    hs = range(hp)
    qk = lambda h: slice(h * 2 * HEAD_DIM, (h + 1) * 2 * HEAD_DIM)
    vs = lambda h: slice(h * HEAD_DIM, (h + 1) * HEAD_DIM)
    q = [q_ref[:, qk(h)] for h in hs]
    rr = lax.broadcasted_iota(jnp.int32, (tq, tk), 0)
    cc = lax.broadcasted_iota(jnp.int32, (tq, tk), 1)
    nsub = tq // tk

    ones = jnp.ones((tk, HEAD_DIM), BF16)

    def block(j, carry, diag):
        m, acc = carry
        ks = pl.multiple_of(j * tk, tk)
        s = [_dot_nt(q[h], k_ref[pl.ds(ks, tk), qk(h)]) for h in hs]
        if diag:
            vis = cc + (j - qi * nsub) * tk <= rr
            s = [jnp.where(vis, s[h], NEG_INF) for h in hs]
        m_new = tuple(jnp.maximum(m[h], jnp.max(s[h], axis=-1, keepdims=True)) for h in hs)
        alpha = [jnp.exp(m[h] - m_new[h]) for h in hs]
        p = [jnp.exp((s[h] - m_new[h]).astype(BF16)) for h in hs]
        v1 = [jnp.concatenate([v_ref[pl.ds(ks, tk), vs(h)], ones], axis=1) for h in hs]
        acc = tuple(alpha[h] * acc[h] + _dot(p[h], v1[h]) for h in hs)
        return m_new, acc

    init = (tuple(jnp.full((tq, 1), NEG_INF, F32) for _ in hs),
            tuple(jnp.zeros((tq, 2 * HEAD_DIM), F32) for _ in hs))
    carry = lax.fori_loop(0, qi * nsub, lambda j, c: block(j, c, False), init)
    for d in range(nsub):
        carry = block(qi * nsub + d, carry, True)
    _, acc = carry
    for h in hs:
        o_ref[:, vs(h)] = (acc[h][:, :HEAD_DIM] / acc[h][:, HEAD_DIM:HEAD_DIM + 1]).astype(o_ref.dtype)


def _mla_attention(q, k, v, *, heads, tq, tk):
    nb, seq, _ = q.shape
    hp = _pick(heads, (4, 2, 1))
    return _pcall(
        functools.partial(_mla_attn_body, tq=tq, tk=tk, hp=hp),
        grid=(nb, heads // hp, seq // tq),
        in_specs=[pl.BlockSpec((None, tq, hp * 2 * HEAD_DIM), lambda b, h, i: (b, i, h)),
                  pl.BlockSpec((None, seq, hp * 2 * HEAD_DIM), lambda b, h, i: (b, 0, h)),
                  pl.BlockSpec((None, seq, hp * HEAD_DIM), lambda b, h, i: (b, 0, h))],
        out_specs=pl.BlockSpec((None, tq, hp * HEAD_DIM), lambda b, h, i: (b, i, h)),
        out_shape=jax.ShapeDtypeStruct((nb, seq, heads * HEAD_DIM), BF16),
        compiler_params=_cparams("parallel", "parallel", "arbitrary"),
    )(q, k, v)


def _bdot(a, b):
    return _dot(a.astype(BF16), b.astype(BF16))


def _dn_body(x_ref, bd_ref, z_ref, cw_ref, alog_ref, dtb_ref, og_ref, o_ref, state, xbuf, *, heads):
    c = DN_CHUNK
    hd = HEAD_DIM

    @pl.when(pl.program_id(1) == 0)
    def _():
        state[...] = jnp.zeros_like(state)
        xbuf[0:8, :] = jnp.zeros((8, 3 * heads * hd), F32)

    xbuf[8:8 + c, :] = x_ref[...].astype(F32)
    cw = cw_ref[...]
    taps = cw.shape[0]

    def head_block(hh):
        sl = slice(hh * hd, (hh + 1) * hd)
        blk = cw[taps - 1:taps, sl] * xbuf[8:8 + c, sl]
        for dd in range(1, taps):
            blk = blk + cw[taps - 1 - dd:taps - dd, sl] * xbuf[8 - dd:8 - dd + c, sl]
        blk = blk * _sigmoid(blk)
        if hh < 2 * heads:
            blk = blk * lax.rsqrt(jnp.sum(blk * blk, axis=-1, keepdims=True) + NORM_EPS)
            if hh < heads:
                blk = blk * hd ** -0.5
        return blk

    bd = bd_ref[...].astype(F32)
    beta = _sigmoid(bd)
    g = -jnp.exp(alog_ref[...]) * _softplus(bd + dtb_ref[...])
    crow = lax.broadcasted_iota(jnp.int32, (c, 1), 0)
    d = 1
    while d < c:
        g = g + jnp.where(crow >= d, pltpu.roll(g, d, 0), 0.0)
        d *= 2

    rr = lax.broadcasted_iota(jnp.int32, (c, c), 0)
    cc = lax.broadcasted_iota(jnp.int32, (c, c), 1)
    lower = rr >= cc
    strict = rr > cc
    eye = (rr == cc).astype(F32)
    same = lambda sh: jnp.right_shift(rr, sh) == jnp.right_shift(cc, sh)
    base_sh = 4
    in_base = strict & same(base_sh)
    off_levels = [strict & same(sh + 1) & jnp.logical_not(same(sh))
                  for sh in range(base_sh, int(math.log2(c)))]
    hs = range(heads)
    q = [head_block(h) for h in hs]
    k = [head_block(heads + h) for h in hs]
    v = [head_block(2 * heads + h) for h in hs]
    xbuf[0:8, :] = xbuf[c:c + 8, :]
    gcol = [g[:, heads + h:heads + h + 1] for h in hs]
    bcol = [beta[:, h:h + 1] for h in hs]
    gcb = [jnp.broadcast_to(gcol[h], (c, c)) for h in hs]
    dec = [jnp.where(lower, jnp.exp(jnp.where(lower, gcb[h] - gcb[h].T, 0.0)), 0.0) for h in hs]
    kb = [k[h] * bcol[h] for h in hs]
    kq = [_dot_nt(jnp.concatenate([kb[h], q[h]], axis=0).astype(BF16), k[h].astype(BF16)) for h in hs]
    a_mat = [jnp.where(strict, kq[h][:c] * dec[h], 0.0) for h in hs]
    intra = [kq[h][c:] * dec[h] for h in hs]
    pw = [-jnp.where(in_base, a_mat[h], 0.0) for h in hs]
    p_mat = [eye + pw[h] for h in hs]
    for _ in range(base_sh - 1):
        pw = [_bdot(pw[h], pw[h]) for h in hs]
        p_mat = [p_mat[h] + _bdot(p_mat[h], pw[h]) for h in hs]
    for off in off_levels:
        pc = [_bdot(p_mat[h], jnp.where(off, a_mat[h], 0.0)) for h in hs]
        p_mat = [p_mat[h] - _bdot(pc[h], p_mat[h]) for h in hs]
    eg = [jnp.exp(gcol[h]) for h in hs]
    uw = [_bdot(p_mat[h], jnp.concatenate([v[h] * bcol[h], kb[h] * eg[h]], axis=1)) for h in hs]
    glast = [gcol[h][c - 1:c, :] for h in hs]
    s_old = [state[h] for h in hs]
    ws = [_bdot(jnp.concatenate([uw[h][:, hd:], q[h] * eg[h]], axis=0), s_old[h]) for h in hs]
    v_new = [uw[h][:, :hd] - ws[h][:c] for h in hs]
    o = [ws[h][c:] + _bdot(intra[h], v_new[h]) for h in hs]
    k_dec = [k[h] * jnp.exp(glast[h] - gcol[h]) for h in hs]
    for h in hs:
        state[h] = s_old[h] * jnp.exp(glast[h]) + _bdot(k_dec[h].T, v_new[h])
    for h in hs:
        zz = z_ref[:, h * hd:(h + 1) * hd].astype(F32)
        o_ref[:, h * hd:(h + 1) * hd] = (_rms(o[h], og_ref[...]) * (zz * _sigmoid(zz))).astype(o_ref.dtype)


def _delta_rule(proj, conv_w, a_log, dt_bias, out_gain, *, heads, qkv_blk, bd_blk, z_blk, nb, seq):
    c = DN_CHUNK
    nt = seq // c
    hw = heads * HEAD_DIM
    pad = lambda v: jnp.zeros((1, LANES), F32).at[0, heads:2 * heads].set(v)
    row = lambda b, t: b * nt + t
    return _pcall(
        functools.partial(_dn_body, heads=heads),
        grid=(nb, nt),
        in_specs=[pl.BlockSpec((c, 3 * hw), lambda b, t: (row(b, t), qkv_blk)),
                  pl.BlockSpec((c, LANES), lambda b, t: (row(b, t), bd_blk)),
                  pl.BlockSpec((c, hw), lambda b, t: (row(b, t), z_blk)),
                  pl.BlockSpec(conv_w.shape, lambda b, t: (0, 0)),
                  pl.BlockSpec((1, LANES), lambda b, t: (0, 0)),
                  pl.BlockSpec((1, LANES), lambda b, t: (0, 0)),
                  pl.BlockSpec((1, HEAD_DIM), lambda b, t: (0, 0))],
        out_specs=pl.BlockSpec((c, hw), lambda b, t: (row(b, t), 0)),
        out_shape=jax.ShapeDtypeStruct((nb * seq, hw), BF16),
        scratch_shapes=[pltpu.VMEM((heads, HEAD_DIM, HEAD_DIM), F32), pltpu.VMEM((c + 8, 3 * hw), F32)],
        compiler_params=_cparams("parallel", "arbitrary"),
    )(proj, proj, proj, conv_w, pad(a_log), pad(dt_bias), out_gain.reshape(1, HEAD_DIM))


def _merge_body(ya_ref, yb_ref, yc_ref, yd_ref, g0_ref, g1_ref, g2_ref, g3_ref, wb_ref, o_ref):
    acc = None
    for i, (y_ref, gl_ref) in enumerate(((ya_ref, g0_ref), (yb_ref, g1_ref), (yc_ref, g2_ref),
                                         (yd_ref, g3_ref))):
        t = _sigmoid(gl_ref[...].astype(F32)) * _dot(y_ref[...], wb_ref[i])
        acc = t if acc is None else acc + t
    o_ref[...] = acc.astype(o_ref.dtype)


def _gated_merge(ys, proj, w_branch, *, gate_off, tm, tn):
    m, bw = ys[0].shape
    nbr, _, n = w_branch.shape
    yspec = pl.BlockSpec((tm, bw), lambda i, j: (i, 0))
    gspecs = [pl.BlockSpec((tm, tn), functools.partial(lambda i, j, o: (i, o + j), o=(gate_off + r * n) // tn))
              for r in range(nbr)]
    return _pcall(
        _merge_body,
        grid=(m // tm, n // tn),
        in_specs=[yspec] * nbr + gspecs + [pl.BlockSpec((nbr, bw, tn), lambda i, j: (0, 0, j))],
        out_specs=pl.BlockSpec((tm, tn), lambda i, j: (i, j)),
        out_shape=jax.ShapeDtypeStruct((m, n), BF16),
        compiler_params=_cparams("parallel", "arbitrary"),
    )(*ys, *([proj] * nbr), w_branch)


def _mem_kv_body(m_ref, g_ref, w_ref, kg_ref, k_ref, v_ref, *, heads):
    xn = _rms(m_ref[...], g_ref[...]).astype(BF16)
    kv = _dot(xn, w_ref[...])
    hw = heads * HEAD_DIM
    for h in range(heads):
        k_ref[:, h * HEAD_DIM:(h + 1) * HEAD_DIM] = _rms(kv[:, h * HEAD_DIM:(h + 1) * HEAD_DIM],
                                                         kg_ref[...]).astype(k_ref.dtype)
    v_ref[...] = kv[:, hw:].astype(v_ref.dtype)


def _mem_kv(mem2, g, wkv, k_gain, *, nb, mlen, heads):
    d = mem2.shape[1]
    hw = heads * HEAD_DIM
    out = pl.BlockSpec((mlen, hw), lambda b: (b, 0))
    shp = jax.ShapeDtypeStruct((nb * mlen, hw), BF16)
    return _pcall(
        functools.partial(_mem_kv_body, heads=heads),
        grid=(nb,),
        in_specs=[pl.BlockSpec((mlen, d), lambda b: (b, 0)),
                  pl.BlockSpec((1, d), lambda b: (0, 0)),
                  pl.BlockSpec((d, 2 * hw), lambda b: (0, 0)),
                  pl.BlockSpec((1, HEAD_DIM), lambda b: (0, 0))],
        out_specs=[out, out],
        out_shape=[shp, shp],
        compiler_params=_cparams("parallel"),
    )(mem2, g.reshape(1, d), wkv, k_gain.reshape(1, HEAD_DIM))


def _xattn_body(x_ref, g_ref, wq_ref, k_ref, v_ref, qg_ref, wo_ref, o_ref, *, heads):
    x = x_ref[...]
    q = _dot(_rms(x, g_ref[...]).astype(BF16), wq_ref[...])
    scale = HEAD_DIM ** -0.5
    outs = []
    for h in range(heads):
        sl = slice(h * HEAD_DIM, (h + 1) * HEAD_DIM)
        qh = (_rms(q[:, sl], qg_ref[...]) * scale).astype(BF16)
        s = _dot_nt(qh, k_ref[:, sl])
        p = jnp.exp(s - jnp.max(s, axis=-1, keepdims=True))
        l = jnp.sum(p, axis=-1, keepdims=True)
        outs.append(_dot(p.astype(BF16), v_ref[:, sl]) / l)
    o = jnp.concatenate(outs, axis=-1).astype(BF16)
    o_ref[...] = x + _dot(o, wo_ref[...])


def _cross_attention(x2, g, wq, k, v, q_gain, wo, *, nb, seq, mlen, heads, tm):
    d = x2.shape[1]
    hw = heads * HEAD_DIM
    nt = seq // tm
    xspec = pl.BlockSpec((tm, d), lambda b, t: (b * nt + t, 0))
    return _pcall(
        functools.partial(_xattn_body, heads=heads),
        grid=(nb, nt),
        in_specs=[xspec,
                  pl.BlockSpec((1, d), lambda b, t: (0, 0)),
                  pl.BlockSpec((d, hw), lambda b, t: (0, 0)),
                  pl.BlockSpec((mlen, hw), lambda b, t: (b, 0)),
                  pl.BlockSpec((mlen, hw), lambda b, t: (b, 0)),
                  pl.BlockSpec((1, HEAD_DIM), lambda b, t: (0, 0)),
                  pl.BlockSpec((hw, d), lambda b, t: (0, 0))],
        out_specs=xspec,
        out_shape=jax.ShapeDtypeStruct(x2.shape, F32),
        compiler_params=_cparams("parallel", "arbitrary"),
    )(x2, g.reshape(1, d), wq, k, v, q_gain.reshape(1, HEAD_DIM), wo)


def _router_body(x_ref, g_ref, w_ref, o_ref, *, n_experts):
    logits = _dot(_rms(x_ref[...], g_ref[...]), w_ref[...], HIGHEST)
    lane = lax.broadcasted_iota(jnp.int32, logits.shape, 1)
    lg = jnp.where(lane < n_experts, logits, -jnp.inf)
    m1 = jnp.max(lg, axis=-1, keepdims=True)
    i1 = jnp.min(jnp.where(lg == m1, lane, LANES), axis=-1, keepdims=True)
    lg2 = jnp.where(lane == i1, -jnp.inf, lg)
    m2 = jnp.max(lg2, axis=-1, keepdims=True)
    i2 = jnp.min(jnp.where(lg2 == m2, lane, LANES), axis=-1, keepdims=True)
    e2 = jnp.exp(m2 - m1)
    den = 1.0 + e2
    o_ref[...] = (jnp.where(lane == 0, 1.0 / den, 0.0) + jnp.where(lane == 1, e2 / den, 0.0)
                  + jnp.where(lane == 2, i1.astype(F32), 0.0) + jnp.where(lane == 3, i2.astype(F32), 0.0))


def _router(x2, g, w_router, *, tm):
    m, d = x2.shape
    ne = w_router.shape[1]
    wpad = jnp.zeros((d, LANES), F32).at[:, :ne].set(w_router)
    return _pcall(
        functools.partial(_router_body, n_experts=ne),
        grid=(m // tm,),
        in_specs=[pl.BlockSpec((tm, d), lambda i: (i, 0)),
                  pl.BlockSpec((1, d), lambda i: (0, 0)),
                  pl.BlockSpec((d, LANES), lambda i: (0, 0))],
        out_specs=pl.BlockSpec((tm, LANES), lambda i: (i, 0)),
        out_shape=jax.ShapeDtypeStruct((m, LANES), F32),
        compiler_params=_cparams("parallel"),
    )(x2, g.reshape(1, d), wpad)


def _rope_tables(seq):
    half = MLA_ROPE // 2
    inv_freq = ROPE_BASE ** (-jnp.arange(half, dtype=F32) / half)
    ang = jnp.arange(seq, dtype=jnp.int32).astype(F32)[:, None] * inv_freq[None, :]
    cos, sin = jnp.cos(ang), jnp.sin(ang)
    z = jnp.zeros((seq, half), F32)
    cos_t = jnp.concatenate([cos, cos, z, z], axis=-1)
    sin_a = jnp.concatenate([-sin, z, z, z], axis=-1)
    sin_b = jnp.concatenate([z, sin, z, z], axis=-1)
    return cos_t, sin_a, sin_b


def kernel(x, mem, mix_norm, w_in, rg_conv_w, rg_conv_b, rg_w_a, rg_b_a, rg_w_x, rg_b_x, rg_lambda, sb_q_gain, sb_k_gain, mla_cq_gain, mla_w_uq, mla_ckv_gain, mla_w_ukv, mla_q_gain, mla_k_gain, dn_conv_w, dn_a_log, dn_dt_bias, dn_out_gain, w_branch, w_out, xa_norm, mem_norm, xa_wq, xa_wkv, xa_q_gain, xa_k_gain, xa_wo, ffn_norm, ffn_w_gu, ffn_w_down, moe_router, moe_w_gu, moe_w_down):
    nb, seq, d = x.shape
    depth = w_in.shape[0]
    mlen = mem.shape[1]
    bw = rg_conv_w.shape[2]
    heads = bw // HEAD_DIM
    q_rank = mla_cq_gain.shape[1]
    kv_rank = mla_ckv_gain.shape[1]
    xa_heads = xa_wq.shape[2] // HEAD_DIM
    t_tok = nb * seq
    assert bw % HEAD_DIM == 0 and seq % 512 == 0 and d % 512 == 0
    assert mla_q_gain.shape[1] == HEAD_DIM + MLA_ROPE

    src_widths = (("rg", 2 * bw), ("sb", 3 * bw), ("cq", q_rank), ("ckv", kv_rank), ("kr", MLA_ROPE),
                  ("dnqkv", 3 * bw), ("dnz", bw), ("bd", 2 * heads), ("gate", 4 * d))
    src, s_pos = {}, 0
    for name, width in src_widths:
        src[name] = (s_pos, s_pos + width)
        s_pos += width
    assert s_pos == w_in.shape[2] and 2 * heads <= LANES
    packed = (("rg", bw), ("sb", _pick(heads, (4, 2, 1)) * HEAD_DIM), ("cq", q_rank), ("ckv", kv_rank),
              ("kr", LANES), ("bd", LANES), ("dnqkv", 3 * bw), ("dnz", bw), ("gate", d))
    off, pieces, p_pos = {}, [], 0
    for name, align in packed:
        width = src[name][1] - src[name][0]
        start = -(-p_pos // align) * align
        pieces.append((start - p_pos, src[name]))
        off[name] = start
        p_pos = start + width
        if name in ("kr", "bd"):
            pieces.append((LANES - width, None))
            p_pos += LANES - width
    n_proj = -(-p_pos // 512) * 512
    pieces.append((n_proj - p_pos, None))

    x2 = x.reshape(t_tok, d)
    mem2 = mem.reshape(nb * mlen, d)
    tabs = _rope_tables(seq)
    mla_scale = (HEAD_DIM + MLA_ROPE) ** -0.5

    cols = []
    for gap, rng in pieces:
        if gap:
            cols.append(jnp.zeros((depth, d, gap), BF16))
        if rng is not None:
            cols.append(w_in[:, :, rng[0]:rng[1]].astype(BF16))
    w_proj = jnp.concatenate(cols, axis=2)

    tm_big = _pick(t_tok, (1024, 512, 256, 128))
    tm_mid = _pick(t_tok, (512, 256, 128))
    ts_seq = _pick(seq, (512, 256, 128))

    for layer in range(depth):
        wq3 =mla_w_uq[layer].reshape(q_rank, heads, HEAD_DIM + MLA_ROPE)
        w_q = jnp.concatenate([wq3, jnp.zeros((q_rank, heads, HEAD_DIM - MLA_ROPE), F32)],
                              axis=-1).reshape(q_rank, heads * 2 * HEAD_DIM).astype(BF16)
        qg = mla_q_gain[layer] * mla_scale
        q_gain2 = jnp.concatenate([qg, jnp.zeros((HEAD_DIM - MLA_ROPE,), F32)]).reshape(1, 2 * HEAD_DIM)
        kgl = mla_k_gain[layer]
        k_gain_n = kgl[:HEAD_DIM].reshape(1, HEAD_DIM)
        k_gain_r = jnp.concatenate([kgl[HEAD_DIM:], jnp.zeros((LANES - MLA_ROPE,), F32)]).reshape(1, LANES)

        proj = _norm_matmul(x2, mix_norm[layer], w_proj, layer=layer, out_dtype=BF16, tm=tm_big,
                            tn=_pick(n_proj, (1024, 512, 256, 128)))
        proj3 = proj.reshape(nb, seq, n_proj)

        y_a = _rglru(proj, nb, seq, rg_conv_w[layer], rg_conv_b[layer], rg_w_a[layer], rg_b_a[layer],
                     rg_w_x[layer], rg_b_x[layer], rg_lambda[layer], ts=_pick(seq, (256, 128)))

        y_b = _sb_attention(proj3, sb_q_gain[layer], sb_k_gain[layer], heads=heads,
                            q_off=off["sb"], tq=_pick(seq, (256, 128)))

        mq = _mla_q(proj, mla_cq_gain[layer], w_q, q_gain2, tabs, heads=heads,
                    cq_blk=off["cq"] // q_rank, nb=nb, seq=seq, tm=ts_seq)
        mk, mv = _mla_kv(proj, mla_ckv_gain[layer], mla_w_ukv[layer].astype(BF16), k_gain_n,
                         k_gain_r, tabs, heads=heads, ckv_blk=off["ckv"] // kv_rank,
                         kr_blk=off["kr"] // LANES, nb=nb, seq=seq, tm=ts_seq)
        r3 = lambda a: a.reshape(nb, seq, a.shape[-1])
        y_c = _mla_attention(r3(mq), r3(mk), r3(mv), heads=heads, tq=ts_seq, tk=ts_seq)

        y_d = _delta_rule(proj, dn_conv_w[layer], dn_a_log[layer], dn_dt_bias[layer], dn_out_gain[layer],
                          heads=heads, qkv_blk=off["dnqkv"] // (3 * bw), bd_blk=off["bd"] // LANES,
                          z_blk=off["dnz"] // bw, nb=nb, seq=seq)

        merged = _gated_merge([y_a, y_b.reshape(t_tok, bw), y_c.reshape(t_tok, bw), y_d], proj,
                              w_branch[layer].astype(BF16), gate_off=off["gate"], tm=tm_big,
                              tn=_pick(d, (512, 256, 128)))
        x2 = _matmul_residual(merged, w_out[layer].astype(BF16), x2, tm=tm_big,
                              tn=_pick(d, (1024, 512, 256, 128)), tk=d)

        mk, mv = _mem_kv(mem2, mem_norm[layer], xa_wkv[layer].astype(BF16), xa_k_gain[layer],
                         nb=nb, mlen=mlen, heads=xa_heads)
        x2 = _cross_attention(x2, xa_norm[layer], xa_wq[layer].astype(BF16), mk, mv, xa_q_gain[layer],
                              xa_wo[layer].astype(BF16), nb=nb, seq=seq, mlen=mlen, heads=xa_heads,
                              tm=ts_seq)

        if layer % 2 == 0:
            wgu = ffn_w_gu[layer // 2].astype(BF16)
            f = wgu.shape[1] // 2
            hmid = _norm_swiglu(x2, ffn_norm[layer], wgu, tm=tm_big, tn=_pick(f, (512, 256, 128)))
            x2 = _matmul_residual(hmid, ffn_w_down[layer // 2].astype(BF16), x2, tm=tm_big,
                                  tn=_pick(d, (512, 256, 128)), tk=f)
        else:
            wgu = moe_w_gu[layer // 2].astype(BF16)
            ne, _, f2 = wgu.shape
            f = f2 // 2
            rt = _router(x2, ffn_norm[layer], moe_router[layer // 2], tm=tm_mid)
            pos, tile_expert, n_used, p_rows = _moe_plan(rt, ne, tm_mid)
            tm_c = _pick(t_tok, (256, 128))
            xg = _moe_scatter(x2, pos.reshape(t_tok // tm_mid, 1, TOP_K * tm_mid), p_rows, tm=tm_mid)
            hmid = _moe_up(xg, ffn_norm[layer], wgu, tile_expert, n_used, tm=tm_mid,
                           tn=_pick(f, (1408, 512, 256, 128)))
            yg = _moe_down(hmid, moe_w_down[layer // 2].astype(BF16), tile_expert, n_used, tm=tm_mid)
            x2 = _moe_combine(x2, rt, pos.reshape(t_tok // tm_c, 1, TOP_K * tm_c), yg, tm=tm_c)
    return x2.reshape(nb, seq, d)
```
